```python
import math
import jax
import jax.numpy as jnp
from jax import lax
import numpy as np

D_MODEL = 1024
BATCH = 4
SEQ = 8192
DEPTH = 4

GRID_W = 64
CTX_LEN = 256
EPS = 1e-6
NEG_INF = -1e30
ROPE_BASE = 10000.0
BLOCK = 128
F32 = jnp.float32

S5_CH = 256
S5_GROUP = 16
S5_GROUPS = S5_CH // S5_GROUP
S5_STATE = 64
SWA_HEADS = 4
SWA_KV_HEADS = 2
SWA_HEAD_DIM = 64
SWA_WINDOW = 128
HG_HEADS = 4
HG_DK = 64
HG_DV = 64
HG_CHUNK = 32
MLA_HEADS = 4
MLA_Q_RANK = 256
MLA_KV_RANK = 128
MLA_NOPE = 64
MLA_ROPE = 32
MLA_V = 64
MLA_SCALE = (MLA_NOPE + MLA_ROPE) ** -0.5
FFN_HIDDEN = -(-8 * D_MODEL // (3 * 256)) * 256

IN_SIZES = (
    S5_CH,
    SWA_HEADS * SWA_HEAD_DIM,
    SWA_KV_HEADS * SWA_HEAD_DIM,
    SWA_KV_HEADS * SWA_HEAD_DIM,
    HG_HEADS * HG_DK,
    HG_HEADS * HG_DK,
    HG_HEADS * HG_DK,
    HG_HEADS * HG_DV,
    HG_HEADS * HG_DV,
    MLA_Q_RANK,
    MLA_KV_RANK,
    MLA_ROPE,
)
N_IN = sum(IN_SIZES)
D_MIX = S5_CH + SWA_HEADS * SWA_HEAD_DIM + HG_HEADS * HG_DV + MLA_HEADS * MLA_V

kernel_name = 'hybrid_prefix_dit_block'


def rms_norm(x, g):
    xf = x.astype(F32)
    y = xf * lax.rsqrt(jnp.mean(xf * xf, axis=-1, keepdims=True) + EPS)
    return (y * g.astype(F32)).astype(x.dtype)


def split_cols(p):
    parts, start = [], 0
    for n in IN_SIZES:
        parts.append(p[..., start:start + n])
        start += n
    return parts


def swiglu(h, w_up, w_down):
    gate, up = jnp.split(h @ w_up, 2, axis=-1)
    return (jax.nn.silu(gate) * up) @ w_down


def axial_rope_tables(length, dim):
    rows = length // GRID_W
    row = jnp.repeat(jnp.arange(rows, dtype=F32), GRID_W)
    col = jnp.tile(jnp.arange(GRID_W, dtype=F32), rows)
    n_freq = dim // 4
    inv = ROPE_BASE ** (-jnp.arange(n_freq, dtype=F32) / n_freq)
    ang = jnp.stack([row[:, None] * inv, col[:, None] * inv], axis=1)
    return jnp.cos(ang), jnp.sin(ang)


def apply_axial_rope(x, cos, sin):
    b, l, h, dim = x.shape
    xr = x.astype(F32).reshape(b, l, h, 2, 2, dim // 4)
    x1, x2 = xr[..., 0, :], xr[..., 1, :]
    c = cos[None, :, None]
    s = sin[None, :, None]
    out = jnp.stack([x1 * c - x2 * s, x2 * c + x1 * s], axis=-2)
    return out.reshape(b, l, h, dim).astype(x.dtype)


def s5_linear_scan(a_bar, bu, s0):
    bu = bu.at[:, 0].add(a_bar * s0)
    a = jnp.broadcast_to(a_bar, bu.shape)

    def combine(e1, e2):
        a1, b1 = e1
        a2, b2 = e2
        return a1 * a2, a2 * b1 + b2

    _, states = lax.associative_scan(combine, (a, bu), axis=1)
    return states


def s5_mixer(u, uc, lam_re, lam_im, log_dt, b_re, b_im, c_re, c_im, d_skip, w_glu, b_glu, need_ctx):
    lam = lax.complex(lam_re.astype(F32), lam_im.astype(F32))
    a_bar = jnp.exp(lam * jnp.exp(log_dt.astype(F32)))
    b_bar = ((a_bar - 1.0) / lam)[..., None] * lax.complex(b_re.astype(F32), b_im.astype(F32))
    c_mat = lax.complex(c_re.astype(F32), c_im.astype(F32))
    d_g = d_skip.astype(F32).reshape(S5_GROUPS, S5_GROUP)

    def grouped(t):
        return t.astype(F32).reshape(t.shape[0], t.shape[1], S5_GROUPS, S5_GROUP)

    def drive(ug, k):
        return jnp.einsum('btgh,gph->btgp', ug.astype(jnp.complex64), b_bar[k])

    def readout(s_f, s_b, ug):
        y = jnp.real(jnp.einsum('btgp,ghp->btgh', s_f, c_mat[0])
                     + jnp.einsum('btgp,ghp->btgh', s_b, c_mat[1])) + d_g * ug
        y = jax.nn.gelu(y.reshape(y.shape[0], y.shape[1], S5_CH))
        return y * jax.nn.sigmoid(y @ w_glu.astype(F32) + b_glu.astype(F32))

    ug, ucg = grouped(u), grouped(uc)
    zero = jnp.zeros((u.shape[0], S5_GROUPS, S5_STATE), jnp.complex64)
    sc_f = s5_linear_scan(a_bar[0], drive(ucg, 0), zero)
    sc_b = jnp.flip(s5_linear_scan(a_bar[1], jnp.flip(drive(ucg, 1), 1), zero), 1)
    sx_f = s5_linear_scan(a_bar[0], drive(ug, 0), sc_f[:, -1])
    sx_b = jnp.flip(s5_linear_scan(a_bar[1], jnp.flip(drive(ug, 1), 1), sc_b[:, 0]), 1)
    y = readout(sx_f, sx_b, ug).astype(u.dtype)
    y_c = readout(sc_f, sc_b, ucg).astype(u.dtype) if need_ctx else None
    return y, y_c


def softmax_with_sink(logits, sink):
    full = jnp.concatenate([logits, jnp.broadcast_to(sink, logits.shape[:-1] + (1,))], axis=-1)
    return jax.nn.softmax(full, axis=-1)[..., :-1]


def window_attention(q, k, v, kc, vc, sink):
    b, l, hq, dh = q.shape
    hkv = k.shape[2]
    grp = hq // hkv
    nb = l // BLOCK
    scale = dh ** -0.5
    qb = q.reshape(b, nb, BLOCK, hkv, grp, dh)

    def band(t):
        tp = jnp.pad(t, ((0, 0), (BLOCK, BLOCK), (0, 0), (0, 0))).reshape(b, nb + 2, BLOCK, hkv, dh)
        return jnp.concatenate([tp[:, :-2], tp[:, 1:-1], tp[:, 2:]], axis=2)

    kw, vw = band(k), band(v)
    qpos = jnp.arange(l).reshape(nb, BLOCK)
    kpos = (jnp.arange(nb) * BLOCK - BLOCK)[:, None] + jnp.arange(3 * BLOCK)[None, :]
    valid = ((jnp.abs(qpos[:, :, None] - kpos[:, None, :]) <= SWA_WINDOW)
             & (kpos >= 0)[:, None, :] & (kpos < l)[:, None, :])
    s_win = jnp.einsum('bnqhgd,bnkhd->bnhgqk', qb, kw).astype(F32) * scale
    s_win = jnp.where(valid[None, :, None, None], s_win, NEG_INF)
    s_ctx = jnp.einsum('bnqhgd,bchd->bnhgqc', qb, kc).astype(F32) * scale
    sink_b = sink.astype(F32).reshape(hkv, grp)[:, :, None, None]
    p = softmax_with_sink(jnp.concatenate([s_win, s_ctx], axis=-1), sink_b).astype(v.dtype)
    o = (jnp.einsum('bnhgqk,bnkhd->bnqhgd', p[..., :3 * BLOCK], vw)
         + jnp.einsum('bnhgqc,bchd->bnqhgd', p[..., 3 * BLOCK:], vc))
    return o.reshape(b, l, hq * dh)


def context_gqa(qc, kc, vc, sink):
    b, lc, hq, dh = qc.shape
    hkv = kc.shape[2]
    grp = hq // hkv
    qg = qc.reshape(b, lc, hkv, grp, dh)
    s = jnp.einsum('bqhgd,bkhd->bhgqk', qg, kc).astype(F32) * dh ** -0.5
    p = softmax_with_sink(s, sink.astype(F32).reshape(hkv, grp)[:, :, None, None]).astype(vc.dtype)
    return jnp.einsum('bhgqk,bkhd->bqhgd', p, vc).reshape(b, lc, hq * dh)


def swa_mixer(q, k, v, qc, kc, vc, sink, rope, need_ctx):
    b, l, _ = q.shape
    lc = kc.shape[1]
    q = apply_axial_rope(q.reshape(b, l, SWA_HEADS, SWA_HEAD_DIM), *rope)
    k = apply_axial_rope(k.reshape(b, l, SWA_KV_HEADS, SWA_HEAD_DIM), *rope)
    v = v.reshape(b, l, SWA_KV_HEADS, SWA_HEAD_DIM)
    kc = kc.reshape(b, lc, SWA_KV_HEADS, SWA_HEAD_DIM)
    vc = vc.reshape(b, lc, SWA_KV_HEADS, SWA_HEAD_DIM)
    y = window_attention(q, k, v, kc, vc, sink)
    y_c = context_gqa(qc.reshape(b, lc, SWA_HEADS, SWA_HEAD_DIM), kc, vc, sink) if need_ctx else None
    return y, y_c


def hgrn2_gates(z, lb):
    z = z.astype(F32)
    log_f = jnp.logaddexp(jnp.log(lb), jnp.log1p(-lb) + jax.nn.log_sigmoid(z))
    k = (1.0 - lb) * jax.nn.sigmoid(-z)
    shape = z.shape[:2] + (HG_HEADS, HG_DK)
    return log_f.reshape(shape), k.reshape(shape)


def hgrn2_chunked(q, log_f, k, v, s0):
    b, t, h, _ = q.shape
    dv = v.shape[-1]
    n = t // HG_CHUNK

    def chunks(a):
        return a.reshape(b, n, HG_CHUNK, h, a.shape[-1])

    q, log_f, k, v = chunks(q), chunks(log_f), chunks(k), chunks(v)
    cum = jnp.cumsum(log_f, axis=2)
    last = cum[:, :, -1]
    q_dec = q * jnp.exp(cum)
    k_inv = k * jnp.exp(-cum)
    k_end = k * jnp.exp(last[:, :, None] - cum)
    lower_tri = jnp.tril(jnp.ones((HG_CHUNK, HG_CHUNK), dtype=bool))
    att = jnp.where(lower_tri, jnp.einsum('bnthd,bnshd->bnhts', q_dec, k_inv), 0.0)
    o_intra = jnp.einsum('bnhts,bnshv->bnthv', att, v)
    kv = jnp.einsum('bnshd,bnshv->bnhdv', k_end, v)

    def step(state, inp):
        dec, kv_c = inp
        return dec[..., None] * state + kv_c, state

    s_final, s_prev = lax.scan(step, s0, (jnp.exp(last).swapaxes(0, 1), kv.swapaxes(0, 1)))
    o_inter = jnp.einsum('bnthd,bnhdv->bnthv', q_dec, s_prev.swapaxes(0, 1))
    return (o_intra + o_inter).reshape(b, t, h, dv), s_final


def hgrn2_final_state(log_f, k, v):
    cum = jnp.cumsum(log_f, axis=1)
    return jnp.einsum('bthd,bthv->bhdv', k * jnp.exp(cum[:, -1:] - cum), v)


def hgrn2_mixer(q, z_f, z_b, i_in, g, q_c, z_f_c, z_b_c, i_c, g_c, lb_f, lb_b, norm_g, need_ctx):
    def heads(a, d):
        return a.astype(F32).reshape(a.shape[0], a.shape[1], HG_HEADS, d)

    def flip(a):
        return jnp.flip(a, 1)

    def gate_out(o, gate):
        o = rms_norm(o, norm_g)
        return o.reshape(o.shape[0], o.shape[1], HG_HEADS * HG_DV) * jax.nn.silu(gate.astype(F32))

    lf_f, k_f = hgrn2_gates(z_f, lb_f)
    lf_b, k_b = hgrn2_gates(z_b, lb_b)
    lfc_f, kc_f = hgrn2_gates(z_f_c, lb_f)
    lfc_b, kc_b = hgrn2_gates(z_b_c, lb_b)
    qh, vh = heads(q, HG_DK), heads(i_in, HG_DV)
    vch = heads(i_c, HG_DV)
    if need_ctx:
        qch = heads(q_c, HG_DK)
        zero = jnp.zeros((q.shape[0], HG_HEADS, HG_DK, HG_DV), F32)
        oc_f, sc_f = hgrn2_chunked(qch, lfc_f, kc_f, vch, zero)
        oc_b, sc_b = hgrn2_chunked(flip(qch), flip(lfc_b), flip(kc_b), flip(vch), zero)
        y_c = gate_out(oc_f + flip(oc_b), g_c).astype(q_c.dtype)
    else:
        sc_f = hgrn2_final_state(lfc_f, kc_f, vch)
        sc_b = hgrn2_final_state(flip(lfc_b), flip(kc_b), flip(vch))
        y_c = None
    o_f, _ = hgrn2_chunked(qh, lf_f, k_f, vh, sc_f)
    o_b, _ = hgrn2_chunked(flip(qh), flip(lf_b), flip(k_b), flip(vh), sc_b)
    y = gate_out(o_f + flip(o_b), g).astype(q.dtype)
    return y, y_c


def mla_queries(cq, q_norm_g, w_qb, rope):
    b, t, _ = cq.shape
    q = (rms_norm(cq, q_norm_g) @ w_qb).reshape(b, t, MLA_HEADS, MLA_NOPE + MLA_ROPE)
    q_nope, q_rope = q[..., :MLA_NOPE], q[..., MLA_NOPE:]
    if rope is not None:
        q_rope = apply_axial_rope(q_rope, *rope)
    return q_nope, q_rope


def mla_keys(ckv, kr, kv_norm_g, w_kvb, rope):
    b, t, _ = ckv.shape
    kv = (rms_norm(ckv, kv_norm_g) @ w_kvb).reshape(b, t, MLA_HEADS, MLA_NOPE + MLA_V)
    k_rope = kr[:, :, None, :]
    if rope is not None:
        k_rope = apply_axial_rope(k_rope, *rope)
    return kv[..., :MLA_NOPE], k_rope[:, :, 0], kv[..., MLA_NOPE:]


def mla_attend(q_nope, q_rope, k_nope, k_rope, v):
    s = (jnp.einsum('bqhd,bkhd->bhqk', q_nope, k_nope)
         + jnp.einsum('bqhd,bkd->bhqk', q_rope, k_rope))
    p = jax.nn.softmax(s.astype(F32) * MLA_SCALE, axis=-1).astype(v.dtype)
    return jnp.einsum('bhqk,bkhd->bqhd', p, v)


def mla_mixer(cq, ckv, kr, cq_c, ckv_c, kr_c, q_norm_g, w_qb, kv_norm_g, w_kvb, rope, need_ctx):
    b, l, _ = cq.shape
    nb = l // BLOCK
    qn, qr = mla_queries(cq, q_norm_g, w_qb, rope)
    kn, krr, v = mla_keys(ckv, kr, kv_norm_g, w_kvb, rope)
    kn_c, krr_c, v_c = mla_keys(ckv_c, kr_c, kv_norm_g, w_kvb, None)
    kn_all = jnp.concatenate([kn_c, kn], axis=1)
    kr_all = jnp.concatenate([krr_c, krr], axis=1)
    v_all = jnp.concatenate([v_c, v], axis=1)

    def to_blocks(a):
        return a.reshape((b, nb, BLOCK) + a.shape[2:]).swapaxes(0, 1)

    o = lax.map(lambda qs: mla_attend(qs[0], qs[1], kn_all, kr_all, v_all),
                (to_blocks(qn), to_blocks(qr)))
    y = o.swapaxes(0, 1).reshape(b, l, MLA_HEADS * MLA_V)
    y_c = None
    if need_ctx:
        qn_c, qr_c = mla_queries(cq_c, q_norm_g, w_qb, None)
        y_c = mla_attend(qn_c, qr_c, kn_c, krr_c, v_c).reshape(b, -1, MLA_HEADS * MLA_V)
    return y, y_c


def hybrid_mixer(hx, hc, w_in, w_out,
                 s5_lam_re, s5_lam_im, s5_log_dt, s5_b_re, s5_b_im, s5_c_re, s5_c_im,
                 s5_d, s5_w_glu, s5_b_glu, swa_sink, lb_f, lb_b, hg_norm_g,
                 mla_q_norm_g, mla_w_qb, mla_kv_norm_g, mla_w_kvb,
                 rope_attn, rope_mla, need_ctx):
    px = split_cols(hx @ w_in)
    pc = split_cols(hc @ w_in)
    ya, ya_c = s5_mixer(px[0], pc[0], s5_lam_re, s5_lam_im, s5_log_dt, s5_b_re, s5_b_im,
                        s5_c_re, s5_c_im, s5_d, s5_w_glu, s5_b_glu, need_ctx)
    yb, yb_c = swa_mixer(px[1], px[2], px[3], pc[1], pc[2], pc[3], swa_sink, rope_attn, need_ctx)
    yc, yc_c = hgrn2_mixer(px[4], px[5], px[6], px[7], px[8], pc[4], pc[5], pc[6], pc[7], pc[8],
                           lb_f, lb_b, hg_norm_g, need_ctx)
    yd, yd_c = mla_mixer(px[9], px[10], px[11], pc[9], pc[10], pc[11],
                         mla_q_norm_g, mla_w_qb, mla_kv_norm_g, mla_w_kvb, rope_mla, need_ctx)
    dt = hx.dtype
    y = jnp.concatenate([ya.astype(dt), yb.astype(dt), yc.astype(dt), yd.astype(dt)], axis=-1) @ w_out
    y_c = None
    if need_ctx:
        y_c = jnp.concatenate([ya_c.astype(dt), yb_c.astype(dt), yc_c.astype(dt), yd_c.astype(dt)],
                              axis=-1) @ w_out
    return y, y_c


def setup_inputs(seed: int = 0) -> dict:
    key = jax.random.key(seed)
    ks = iter(jax.random.split(key, 32))

    def nrm(shape, scale):
        return jax.random.normal(next(ks), shape, F32) * scale

    L = DEPTH
    G, P = S5_GROUPS, S5_STATE
    n_idx = jnp.arange(P, dtype=F32)
    return {
        'x': nrm((BATCH, SEQ, D_MODEL), 1.0),
        'c': nrm((BATCH, D_MODEL), 1.0),
        'ctx': nrm((BATCH, CTX_LEN, D_MODEL), 1.0),
        'c_ctx': nrm((D_MODEL,), 1.0),
        'w_mod': nrm((L, D_MODEL, 6 * D_MODEL), 0.5 * D_MODEL ** -0.5),
        'b_mod': nrm((L, 6 * D_MODEL), 0.01),
        'norm1_g': 1.0 + nrm((L, D_MODEL), 0.02),
        'norm2_g': 1.0 + nrm((L, D_MODEL), 0.02),
        'w_in': nrm((L, D_MODEL, N_IN), D_MODEL ** -0.5),
        'w_out': nrm((L, D_MIX, D_MODEL), D_MIX ** -0.5),
        's5_lam_re': -0.5 + nrm((L, 2, G, P), 0.01),
        's5_lam_im': math.pi * n_idx + nrm((L, 2, G, P), 0.01),
        's5_log_dt': jax.random.uniform(next(ks), (L, 2, G, P), F32, math.log(1e-3), math.log(1e-1)),
        's5_b_re': nrm((L, 2, G, P, S5_GROUP), (2 * S5_GROUP) ** -0.5),
        's5_b_im': nrm((L, 2, G, P, S5_GROUP), (2 * S5_GROUP) ** -0.5),
        's5_c_re': nrm((L, 2, G, S5_GROUP, P), (2 * P) ** -0.5),
        's5_c_im': nrm((L, 2, G, S5_GROUP, P), (2 * P) ** -0.5),
        's5_d': nrm((L, S5_CH), 1.0),
        's5_w_glu': nrm((L, S5_CH, S5_CH), S5_CH ** -0.5),
        's5_b_glu': nrm((L, S5_CH), 0.01),
        'swa_sink': nrm((L, SWA_HEADS), 0.5),
        'hg_lb': nrm((2, L, HG_HEADS * HG_DK), 1.0),
        'hg_norm_g': 1.0 + nrm((L, HG_DV), 0.02),
        'mla_q_norm_g': 1.0 + nrm((L, MLA_Q_RANK), 0.02),
        'mla_w_qb': nrm((L, MLA_Q_RANK, MLA_HEADS * (MLA_NOPE + MLA_ROPE)), MLA_Q_RANK ** -0.5),
        'mla_kv_norm_g': 1.0 + nrm((L, MLA_KV_RANK), 0.02),
        'mla_w_kvb': nrm((L, MLA_KV_RANK, MLA_HEADS * (MLA_NOPE + MLA_V)), MLA_KV_RANK ** -0.5),
        'ffn_w_up': nrm((L, D_MODEL, 2 * FFN_HIDDEN), D_MODEL ** -0.5),
        'ffn_w_down': nrm((L, FFN_HIDDEN, D_MODEL), FFN_HIDDEN ** -0.5),
        'final_norm_g': 1.0 + nrm((D_MODEL,), 0.02),
    }


def reference(x, c, ctx, c_ctx, w_mod, b_mod, norm1_g, norm2_g, w_in, w_out,
              s5_lam_re, s5_lam_im, s5_log_dt, s5_b_re, s5_b_im, s5_c_re, s5_c_im,
              s5_d, s5_w_glu, s5_b_glu, swa_sink, hg_lb, hg_norm_g,
              mla_q_norm_g, mla_w_qb, mla_kv_norm_g, mla_w_kvb,
              ffn_w_up, ffn_w_down, final_norm_g):
    length = x.shape[1]
    rope_attn = axial_rope_tables(length, SWA_HEAD_DIM)
    rope_mla = axial_rope_tables(length, MLA_ROPE)
    lb_cum = jnp.cumsum(jax.nn.softmax(hg_lb.astype(F32), axis=1), axis=1)
    lb = lb_cum - lb_cum[:, :1]
    silu_c = jax.nn.silu(c)
    silu_cc = jax.nn.silu(c_ctx)
    for i in range(DEPTH):
        need_ctx = i < DEPTH - 1
        mod = (silu_c @ w_mod[i] + b_mod[i])[:, None, :]
        mod_c = silu_cc @ w_mod[i] + b_mod[i]
        sh1, sc1, g1, sh2, sc2, g2 = jnp.split(mod, 6, axis=-1)
        csh1, csc1, cg1, csh2, csc2, cg2 = jnp.split(mod_c, 6, axis=-1)
        hx = rms_norm(x, norm1_g[i]) * (1.0 + sc1) + sh1
        hc = rms_norm(ctx, norm1_g[i]) * (1.0 + csc1) + csh1
        y, y_c = hybrid_mixer(hx, hc, w_in[i], w_out[i],
                              s5_lam_re[i], s5_lam_im[i], s5_log_dt[i], s5_b_re[i], s5_b_im[i],
                              s5_c_re[i], s5_c_im[i], s5_d[i], s5_w_glu[i], s5_b_glu[i],
                              swa_sink[i], lb[0, i], lb[1, i], hg_norm_g[i],
                              mla_q_norm_g[i], mla_w_qb[i], mla_kv_norm_g[i], mla_w_kvb[i],
                              rope_attn, rope_mla, need_ctx)
        x = x + g1 * y
        hx = rms_norm(x, norm2_g[i]) * (1.0 + sc2) + sh2
        x = x + g2 * swiglu(hx, ffn_w_up[i], ffn_w_down[i])
        if need_ctx:
            ctx = ctx + cg1 * y_c
            hc = rms_norm(ctx, norm2_g[i]) * (1.0 + csc2) + csh2
            ctx = ctx + cg2 * swiglu(hc, ffn_w_up[i], ffn_w_down[i])
    return rms_norm(x, final_norm_g)
```

```python
import functools
import math

import jax
import jax.numpy as jnp
from jax import lax
from jax.experimental import pallas as pl
from jax.experimental.pallas import tpu as pltpu

F32 = jnp.float32
BF16 = jnp.bfloat16

EPS = 1e-6
NEG_INF = -1e30
ROPE_BASE = 10000.0
GRID_W = 64
LANE = 128
VMEM_LIMIT = 56 * 1024 * 1024

S5_CH, S5_GROUP, S5_STATE = 256, 16, 64
S5_GROUPS = S5_CH // S5_GROUP
S5_CHUNK = 16
SWA_HEADS, SWA_KV_HEADS, SWA_HEAD_DIM, SWA_WINDOW = 4, 2, 64, 128
HG_HEADS, HG_DK, HG_DV = 4, 64, 64
HG_CHUNK = 64
MLA_HEADS, MLA_Q_RANK, MLA_KV_RANK = 4, 256, 128
MLA_NOPE, MLA_ROPE, MLA_V = 64, 32, 64
MLA_SCALE = (MLA_NOPE + MLA_ROPE) ** -0.5


def _tile(n, pref):
    t = min(n, pref)
    assert n % t == 0, (n, pref)
    return t


def _params(sem):
    return pltpu.CompilerParams(dimension_semantics=sem, vmem_limit_bytes=VMEM_LIMIT)


def _full(shape):
    nd = len(shape)
    return pl.BlockSpec(shape, lambda *_: (0,) * nd)


def _mod_kernel(c_ref, w_ref, b_ref, o_ref):
    c = c_ref[...]
    s = c * (1.0 / (1.0 + jnp.exp(-c)))
    o_ref[...] = jnp.dot(s.astype(BF16), w_ref[...].astype(BF16), preferred_element_type=F32) + b_ref[...]


def _modulation(cc, w_mod, b_mod):
    depth, d, n = w_mod.shape
    tn = _tile(n, 1536)
    return pl.pallas_call(
        _mod_kernel,
        out_shape=jax.ShapeDtypeStruct((depth, 8, n), F32),
        grid=(depth, n // tn),
        in_specs=[pl.BlockSpec((8, d), lambda l, j: (0, 0)),
                  pl.BlockSpec((None, d, tn), lambda l, j: (l, 0, j)),
                  pl.BlockSpec((None, 1, tn), lambda l, j: (l, 0, j))],
        out_specs=pl.BlockSpec((None, 8, tn), lambda l, j: (l, 0, j)),
        compiler_params=_params(("parallel", "parallel")),
        name="modulation",
    )(cc, w_mod, b_mod.reshape(depth, 1, n))


def _rope_block(x, t_ref, half):
    return (x * t_ref[0] + pltpu.roll(x, LANE - half, 1) * t_ref[1]
            + pltpu.roll(x, half, 1) * t_ref[2])


def _rms(x, g):
    return x * lax.rsqrt(jnp.mean(x * x, axis=-1, keepdims=True) + EPS) * g


_C_U = 0
_C_SQ = 256
_C_SK = _C_SQ + 512
_C_SV = _C_SK + 256
_C_HG = _C_SV + 256
_C_CQ = _C_HG + 5 * 256
_C_CKV = _C_CQ + 256
_C_KR = _C_CKV + 128
_N_INP = _C_KR + 128


def _inproj_kernel(x_ref, sc_ref, sh_ref, g_ref, w_ref, ts_ref, tm_ref, qg_ref, wqb_ref, kvg_ref, wkv_ref,
                   u_ref, sq_ref, sk_ref, sv_ref, hq_ref, hzf_ref, hzb_ref, hi_ref, hg_ref,
                   mq_ref, mk_ref, mv_ref):
    h = _rms(x_ref[...], g_ref[...]) * (1.0 + sc_ref[...]) + sh_ref[...]
    hb = h.astype(BF16)

    def proj(lo, n):
        return jnp.dot(hb, w_ref[:, lo:lo + n], preferred_element_type=F32)

    u_ref[...] = proj(_C_U, 256)
    for hh in range(SWA_HEADS):
        sq_ref[:, hh * LANE:(hh + 1) * LANE] = _rope_block(proj(_C_SQ + hh * LANE, LANE), ts_ref, 16).astype(BF16)
    for hh in range(SWA_KV_HEADS):
        sk_ref[:, hh * LANE:(hh + 1) * LANE] = _rope_block(proj(_C_SK + hh * LANE, LANE), ts_ref, 16).astype(BF16)
    sv_ref[...] = proj(_C_SV, 256).astype(BF16)
    for i, r in enumerate((hq_ref, hzf_ref, hzb_ref, hi_ref, hg_ref)):
        r[...] = proj(_C_HG + i * 256, 256)
    cq = _rms(proj(_C_CQ, 256), qg_ref[...])
    q = jnp.dot(cq.astype(BF16), wqb_ref[...], preferred_element_type=F32)
    ckv = _rms(proj(_C_CKV, 128), kvg_ref[...])
    kv = jnp.dot(ckv.astype(BF16), wkv_ref[...], preferred_element_type=F32)
    kr = pltpu.roll(proj(_C_KR, LANE), MLA_NOPE, 1)
    for hh in range(MLA_HEADS):
        sl = slice(hh * LANE, (hh + 1) * LANE)
        mq_ref[:, sl] = _rope_block(q[:, sl], tm_ref, 8).astype(BF16)
        mk_ref[:, sl] = _rope_block(kv[:, sl] + kr, tm_ref, 8).astype(BF16)
    mv_ref[...] = kv[:, MLA_HEADS * LANE:].astype(BF16)


def _inproj(x, sc, sh, g, w_in_p, tab_swa, tab_mla, qg, wqb_p, kvg, wkv_p):
    b, t, d = x.shape
    tt = _tile(t, 512)
    row = lambda n, dt: jax.ShapeDtypeStruct((b, t, n), dt)
    out_shape = (row(256, F32), row(512, BF16), row(256, BF16), row(256, BF16),
                 row(256, F32), row(256, F32), row(256, F32), row(256, F32), row(256, F32),
                 row(512, BF16), row(512, BF16), row(512, BF16))
    xs = lambda n: pl.BlockSpec((None, tt, n), lambda i, bb: (bb, i, 0))
    vec = pl.BlockSpec((None, 1, d), lambda i, bb: (bb, 0, 0))
    tab = pl.BlockSpec((3, tt, LANE), lambda i, bb: (0, i, 0))
    return pl.pallas_call(
        _inproj_kernel,
        out_shape=out_shape,
        grid=(t // tt, b),
        in_specs=[xs(d), vec, vec, _full((1, d)), _full(w_in_p.shape), tab, tab,
                  _full((1, 256)), _full(wqb_p.shape), _full((1, 128)), _full(wkv_p.shape)],
        out_specs=tuple(xs(s.shape[-1]) for s in out_shape),
        compiler_params=_params(("parallel", "parallel")),
        name="inproj",
    )(x, sc, sh, g, w_in_p, tab_swa, tab_mla, qg, wqb_p, kvg, wkv_p)


def _s5_kernel(u_ref, m_ref, f_ref, e_ref, a_ref, s0_ref, y_ref, sfin_ref, zs_ref, ssf_ref, ssb_ref, *, nb, nc):
    ub = u_ref[...].astype(BF16)
    zs_ref[...] = jnp.dot(ub, f_ref[...], preferred_element_type=F32)
    lane = lax.broadcasted_iota(jnp.int32, (1, 256), 1)
    fmask = (lane % LANE) < S5_STATE
    ar = a_ref[0:1, :]
    ai = a_ref[1:2, :]

    def step(i, states):
        new = []
        for bb in range(nb):
            s = states[bb]
            rf = bb * nc + i
            rb = bb * nc + (nc - 1 - i)
            ssf_ref[pl.ds(rf, 1), :] = s
            ssb_ref[pl.ds(rb, 1), :] = s
            z = jnp.where(fmask, zs_ref[pl.ds(rf, 1), :], zs_ref[pl.ds(rb, 1), :])
            re, im = s[:, :LANE], s[:, LANE:]
            new.append(jnp.concatenate([ar * re - ai * im, ar * im + ai * re], axis=1) + z)
        return tuple(new)

    init = tuple(s0_ref[bb:bb + 1, :] for bb in range(nb))
    fin = lax.fori_loop(0, nc, step, init)
    for bb in range(nb):
        sfin_ref[bb:bb + 1, :] = fin[bb]
    ss = jnp.where(fmask, ssf_ref[...], ssb_ref[...])
    y_ref[...] = (jnp.dot(ub, m_ref[...], preferred_element_type=F32)
                  + jnp.dot(ss.astype(BF16), e_ref[...], preferred_element_type=F32))


def _s5_scan(u, mats, s0):
    m, f, e, a16 = mats
    b, t, _ = u.shape
    nc = t // S5_CHUNK
    r = b * nc
    ug = u.reshape(b, nc, S5_CHUNK, S5_GROUPS, S5_GROUP).transpose(3, 0, 1, 2, 4).reshape(S5_GROUPS, r, 256)
    grp = lambda *s: pl.BlockSpec((None,) + s, lambda gidx: (gidx,) + (0,) * len(s))
    y, sfin = pl.pallas_call(
        functools.partial(_s5_kernel, nb=b, nc=nc),
        out_shape=(jax.ShapeDtypeStruct((S5_GROUPS, r, 256), F32),
                   jax.ShapeDtypeStruct((S5_GROUPS, b, 256), F32)),
        grid=(S5_GROUPS,),
        in_specs=[grp(r, 256), grp(256, 256), grp(256, 256), grp(256, 256), grp(2, LANE), grp(b, 256)],
        out_specs=(grp(r, 256), grp(b, 256)),
        scratch_shapes=[pltpu.VMEM((r, 256), F32)] * 3,
        compiler_params=_params(("parallel",)),
        name="s5",
    )(ug, m, f, e, a16, s0)
    y = y.reshape(S5_GROUPS, b, nc, S5_CHUNK, S5_GROUP).transpose(1, 2, 3, 0, 4).reshape(b, t, 256)
    return y, sfin


def _s5_matrices(lam_re, lam_im, log_dt, b_re, b_im, c_re, c_im, d_skip):
    L = S5_CHUNK
    lam = lax.complex(lam_re, lam_im)
    ldt = lam * jnp.exp(log_dt)
    a_bar = jnp.exp(ldt)
    bb = ((a_bar - 1.0) / lam)[..., None] * lax.complex(b_re, b_im)
    cc = lax.complex(c_re, c_im)
    k = jnp.arange(L + 1, dtype=F32)
    apow = jnp.exp(ldt[None] * k[:, None, None, None])
    kern = jnp.real(jnp.einsum('dghp,kdgp,dgpi->dkgih', cc, apow[:L], bb))
    jj = jnp.arange(L)[:, None]
    tt = jnp.arange(L)[None, :]
    lag_f = jnp.clip(tt - jj, 0, L - 1)
    lag_b = jnp.clip(jj - tt, 0, L - 1)
    kf = kern[0][lag_f] * (jj <= tt)[:, :, None, None, None]
    kb = kern[1][lag_b] * (jj >= tt)[:, :, None, None, None]
    dg = d_skip.reshape(S5_GROUPS, S5_GROUP)
    eye = jnp.eye(S5_GROUP, dtype=F32)
    kd = (jj == tt)[:, :, None, None, None] * (dg[:, None, :] * eye[None])[None, None]
    m = (kf + kb + kd).transpose(2, 0, 3, 1, 4).reshape(S5_GROUPS, L * S5_GROUP, L * S5_GROUP)
    pf = apow[L - 1 - jnp.arange(L), 0]
    pb = apow[jnp.arange(L), 1]
    zf = pf[..., None] * bb[0][None]
    zb = pb[..., None] * bb[1][None]
    fmat = jnp.concatenate([jnp.real(zf), jnp.real(zb), jnp.imag(zf), jnp.imag(zb)], axis=2)
    fmat = fmat.transpose(1, 0, 3, 2).reshape(S5_GROUPS, L * S5_GROUP, 4 * S5_STATE)
    wf = cc[0][None] * apow[1 + jnp.arange(L), 0][:, :, None, :]
    wb = cc[1][None] * apow[L - jnp.arange(L), 1][:, :, None, :]
    emat = jnp.concatenate([jnp.real(wf), jnp.real(wb), -jnp.imag(wf), -jnp.imag(wb)], axis=3)
    emat = emat.transpose(1, 3, 0, 2).reshape(S5_GROUPS, 4 * S5_STATE, L * S5_GROUP)
    al = apow[L]
    a16 = jnp.stack([jnp.concatenate([jnp.real(al[0]), jnp.real(al[1])], axis=-1),
                     jnp.concatenate([jnp.imag(al[0]), jnp.imag(al[1])], axis=-1)], axis=1)
    return m.astype(BF16), fmat.astype(BF16), emat.astype(BF16), a16.astype(F32)


def _attn_kernel(*refs, grp, bq, has_sink, has_seq, window, t_seq, bk):
    refs = list(refs)
    sink_ref = refs.pop(0) if has_sink else None
    q_ref, kc_ref, vc_ref = refs[:3]
    ks_ref, vs_ref = (refs[3], refs[4]) if has_seq else (None, None)
    o_ref = refs[-1]
    hk = pl.program_id(1)
    qi = pl.program_id(2)
    rows = grp * bq
    q = jnp.concatenate([q_ref[:, g * LANE:(g + 1) * LANE] for g in range(grp)], axis=0)

    if has_sink:
        m = jnp.concatenate([jnp.full((bq, 1), sink_ref[hk * grp + g], F32) for g in range(grp)], axis=0)
        l = jnp.ones((rows, 1), F32)
    else:
        m = jnp.full((rows, 1), NEG_INF, F32)
        l = jnp.zeros((rows, 1), F32)
    acc = jnp.zeros((rows, LANE), F32)

    def update(carry, k, v, mask):
        m, l, acc = carry
        s = lax.dot_general(q, k, (((1,), (1,)), ((), ())), preferred_element_type=F32)
        if mask is not None:
            s = jnp.where(mask, s, NEG_INF)
        m_new = jnp.maximum(m, jnp.max(s, axis=-1, keepdims=True))
        alpha = jnp.exp(m - m_new)
        p = jnp.exp(s - m_new)
        l = alpha * l + jnp.sum(p, axis=-1, keepdims=True)
        acc = alpha * acc + jnp.dot(p.astype(BF16), v, preferred_element_type=F32)
        return m_new, l, acc

    carry = update((m, l, acc), kc_ref[...], vc_ref[...], None)
    if has_seq and window is not None:
        wlen = bq + 2 * window
        start = jnp.clip(qi * bq - window, 0, t_seq - wlen)
        start = pl.multiple_of(start, LANE)
        qpos = qi * bq + lax.broadcasted_iota(jnp.int32, (rows, wlen), 0) % bq
        kpos = start + lax.broadcasted_iota(jnp.int32, (rows, wlen), 1)
        mask = jnp.abs(qpos - kpos) <= window
        carry = update(carry, ks_ref[pl.ds(start, wlen), :], vs_ref[pl.ds(start, wlen), :], mask)
    elif has_seq:
        def body(j, c):
            st = pl.multiple_of(j * bk, bk)
            return update(c, ks_ref[pl.ds(st, bk), :], vs_ref[pl.ds(st, bk), :], None)
        carry = lax.fori_loop(0, t_seq // bk, body, carry)
    m, l, acc = carry
    o = acc * (1.0 / l)
    for g in range(grp):
        o_ref[:, g * LANE:(g + 1) * LANE] = o[g * bq:(g + 1) * bq].astype(o_ref.dtype)


def _attention(q, kc, vc, ks=None, vs=None, *, grp, sink=None, window=None, bq=256, bk=512):
    b, tq, wq = q.shape
    hkv = wq // (grp * LANE)
    lc = kc.shape[1]
    has_seq = ks is not None
    t_seq = ks.shape[1] if has_seq else 0
    bq = _tile(tq, bq)
    if has_seq and window is None:
        bk = _tile(t_seq, bk)
    kern = functools.partial(_attn_kernel, grp=grp, bq=bq, has_sink=sink is not None, has_seq=has_seq,
                             window=window, t_seq=t_seq, bk=bk)
    in_specs, args = [], []
    if sink is not None:
        in_specs.append(pl.BlockSpec(memory_space=pltpu.SMEM))
        args.append(sink)
    in_specs.append(pl.BlockSpec((None, bq, grp * LANE), lambda bb, h, i: (bb, i, h)))
    args.append(q)
    ctx_spec = pl.BlockSpec((None, lc, LANE), lambda bb, h, i: (bb, 0, h))
    in_specs += [ctx_spec, ctx_spec]
    args += [kc, vc]
    if has_seq:
        seq_spec = pl.BlockSpec((None, t_seq, LANE), lambda bb, h, i: (bb, 0, h))
        in_specs += [seq_spec, seq_spec]
        args += [ks, vs]
    return pl.pallas_call(
        kern,
        out_shape=jax.ShapeDtypeStruct((b, tq, wq), BF16),
        grid=(b, hkv, tq // bq),
        in_specs=in_specs,
        out_specs=pl.BlockSpec((None, bq, grp * LANE), lambda bb, h, i: (bb, i, h)),
        compiler_params=_params(("parallel", "parallel", "parallel")),
        name="attention",
    )(*args)


def _split3(x):
    hi = x.astype(BF16)
    r = x - hi.astype(F32)
    mid = r.astype(BF16)
    lo = (r - mid.astype(F32)).astype(BF16)
    return hi, mid, lo


def _hg_chunk(q, z, v, st, gp, fwd):
    c = q.shape[0]
    log_lb, l1p, om = gp
    ls = jnp.minimum(z, 0.0) - jnp.log1p(jnp.exp(-jnp.abs(z)))
    bterm = l1p + ls
    mx = jnp.maximum(log_lb, bterm)
    lf = mx + jnp.log1p(jnp.exp(-jnp.abs(log_lb - bterm)))
    k = om / (1.0 + jnp.exp(z))
    ri = lax.broadcasted_iota(jnp.int32, (c, c), 0)
    ci = lax.broadcasted_iota(jnp.int32, (c, c), 1)
    tri = jnp.where((ci <= ri) if fwd else (ci >= ri), 1.0, 0.0).astype(BF16)
    gsum = sum(jnp.dot(tri, part, preferred_element_type=F32) for part in _split3(lf))
    tot = gsum[c - 1:c, :] if fwd else gsum[0:1, :]
    gm = gsum[c // 2:c // 2 + 1, :]
    qd = (q * jnp.exp(gsum - gm)).astype(BF16)
    kinv = k * jnp.exp(gm - gsum)
    q_in = (q * jnp.exp(gsum)).astype(BF16)
    k_end = (k * jnp.exp(tot - gsum)).astype(BF16)
    lane = lax.broadcasted_iota(jnp.int32, (1, HG_HEADS * HG_DK), 1)
    hmask = [(lane // HG_DK) == h for h in range(HG_HEADS)]
    kstack = jnp.concatenate([jnp.where(hm, kinv, 0.0) for hm in hmask], axis=0).astype(BF16)
    vstack = jnp.concatenate([jnp.where(hm, v, 0.0) for hm in hmask], axis=0).astype(BF16)
    att = lax.dot_general(qd, kstack, (((1,), (1,)), ((), ())), preferred_element_type=F32)
    ti = lax.broadcasted_iota(jnp.int32, (c, HG_HEADS * c), 0)
    si = lax.broadcasted_iota(jnp.int32, (c, HG_HEADS * c), 1) % c
    att = jnp.where((si <= ti) if fwd else (si >= ti), att, 0.0)
    o = jnp.dot(att.astype(BF16), vstack, preferred_element_type=F32)
    o = o + lax.dot_general(q_in, st.astype(BF16), (((1,), (1,)), ((), ())), preferred_element_type=F32)
    kv_t = lax.dot_general(v.astype(BF16), k_end, (((0,), (0,)), ((), ())), preferred_element_type=F32)
    r2 = lax.broadcasted_iota(jnp.int32, (HG_HEADS * HG_DV, HG_HEADS * HG_DK), 0) // HG_DV
    c2 = lax.broadcasted_iota(jnp.int32, (HG_HEADS * HG_DV, HG_HEADS * HG_DK), 1) // HG_DK
    st = st * jnp.exp(tot) + jnp.where(r2 == c2, kv_t, 0.0)
    return o, st


def _hg_kernel(gp_ref, qf_ref, zf_ref, vf_ref, qb_ref, zb_ref, vb_ref, s0f_ref, s0b_ref,
               of_ref, ob_ref, sf_ref, sb_ref, st_ref, *, nchunk):
    i = pl.program_id(1)
    c = HG_CHUNK

    @pl.when(i == 0)
    def _():
        st_ref[0] = s0f_ref[...]
        st_ref[1] = s0b_ref[...]

    gpf = tuple(gp_ref[r:r + 1, :] for r in range(3))
    gpb = tuple(gp_ref[r:r + 1, :] for r in range(3, 6))
    stf = st_ref[0]
    stb = st_ref[1]
    for n in range(nchunk):
        sl = slice(n * c, (n + 1) * c)
        o, stf = _hg_chunk(qf_ref[sl, :], zf_ref[sl, :], vf_ref[sl, :], stf, gpf, True)
        of_ref[sl, :] = o
        nb = nchunk - 1 - n
        sl = slice(nb * c, (nb + 1) * c)
        o, stb = _hg_chunk(qb_ref[sl, :], zb_ref[sl, :], vb_ref[sl, :], stb, gpb, False)
        ob_ref[sl, :] = o
    st_ref[0] = stf
    st_ref[1] = stb

    @pl.when(i == pl.num_programs(1) - 1)
    def _():
        sf_ref[...] = stf
        sb_ref[...] = stb


def _hgrn(q, zf, zb, v, gp, s0f, s0b):
    b, t, w = q.shape
    tb = _tile(t, 256)
    nblk = t // tb
    fw = pl.BlockSpec((None, tb, w), lambda bb, i: (bb, i, 0))
    bw = pl.BlockSpec((None, tb, w), lambda bb, i: (bb, nblk - 1 - i, 0))
    st = pl.BlockSpec((None, w, w), lambda bb, i: (bb, 0, 0))
    return pl.pallas_call(
        functools.partial(_hg_kernel, nchunk=tb // HG_CHUNK),
        out_shape=(jax.ShapeDtypeStruct((b, t, w), F32), jax.ShapeDtypeStruct((b, t, w), F32),
                   jax.ShapeDtypeStruct((b, w, w), F32), jax.ShapeDtypeStruct((b, w, w), F32)),
        grid=(b, nblk),
        in_specs=[_full((8, w)), fw, fw, fw, bw, bw, bw, st, st],
        out_specs=(fw, bw, st, st),
        scratch_shapes=[pltpu.VMEM((2, w, w), F32)],
        compiler_params=_params(("parallel", "arbitrary")),
        name="hgrn2",
    )(gp, q, zf, v, q, zb, v, s0f, s0b)


def _gelu_tanh(x):
    return 0.5 * x * (1.0 + jnp.tanh(math.sqrt(2.0 / math.pi) * (x + 0.044715 * (x * x * x))))


def _outproj_kernel(x_ref, g1_ref, ya_ref, yb_ref, of_ref, ob_ref, hg_ref, yd_ref,
                    wglu_ref, bglu_ref, hn_ref, pavg_ref, wo_ref, o_ref):
    ya = _gelu_tanh(ya_ref[...])
    gl = jnp.dot(ya.astype(BF16), wglu_ref[...], preferred_element_type=F32) + bglu_ref[...]
    ya = ya * (1.0 / (1.0 + jnp.exp(-gl)))
    o = of_ref[...] + ob_ref[...]
    o2 = o * o
    hi = o2.astype(BF16)
    lo = (o2 - hi.astype(F32)).astype(BF16)
    ms = (jnp.dot(hi, pavg_ref[...], preferred_element_type=F32)
          + jnp.dot(lo, pavg_ref[...], preferred_element_type=F32))
    gate = hg_ref[...]
    yc = o * lax.rsqrt(ms + EPS) * hn_ref[...] * (gate * (1.0 / (1.0 + jnp.exp(-gate))))
    y = (jnp.dot(ya.astype(BF16), wo_ref[0:256, :], preferred_element_type=F32)
         + jnp.dot(yb_ref[...], wo_ref[256:768, :], preferred_element_type=F32)
         + jnp.dot(yc.astype(BF16), wo_ref[768:1024, :], preferred_element_type=F32)
         + jnp.dot(yd_ref[...], wo_ref[1024:1536, :], preferred_element_type=F32))
    o_ref[...] = x_ref[...] + g1_ref[...] * y


def _outproj(x, g1, ya, yb, of, ob, hg, yd, wglu, bglu, hn, pavg, wo_p):
    b, t, d = x.shape
    tt = _tile(t, 512)
    xs = lambda n: pl.BlockSpec((None, tt, n), lambda bb, i: (bb, i, 0))
    vec = pl.BlockSpec((None, 1, d), lambda bb, i: (bb, 0, 0))
    return pl.pallas_call(
        _outproj_kernel,
        out_shape=jax.ShapeDtypeStruct((b, t, d), F32),
        grid=(b, t // tt),
        in_specs=[xs(d), vec, xs(256), xs(512), xs(256), xs(256), xs(256), xs(512),
                  _full(wglu.shape), _full((1, 256)), _full((1, 256)), _full(pavg.shape), _full(wo_p.shape)],
        out_specs=xs(d),
        compiler_params=_params(("parallel", "parallel")),
        name="outproj",
    )(x, g1, ya, yb, of, ob, hg, yd, wglu, bglu, hn, pavg, wo_p)


def _ffn_kernel(x_ref, sc_ref, sh_ref, g2_ref, ng_ref, wup_ref, wdn_ref, fg_ref, o_ref, acc_ref, *, hidden, ck, final):
    x = x_ref[...]
    hb = (_rms(x, ng_ref[...]) * (1.0 + sc_ref[...]) + sh_ref[...]).astype(BF16)
    for j in range(hidden // ck):
        gate = jnp.dot(hb, wup_ref[:, j * ck:(j + 1) * ck], preferred_element_type=F32)
        up = jnp.dot(hb, wup_ref[:, hidden + j * ck:hidden + (j + 1) * ck], preferred_element_type=F32)
        a = (gate * (1.0 / (1.0 + jnp.exp(-gate))) * up).astype(BF16)
        part = jnp.dot(a, wdn_ref[j * ck:(j + 1) * ck, :], preferred_element_type=F32)
        if j == 0:
            acc_ref[...] = part
        else:
            acc_ref[...] += part
    y = x + g2_ref[...] * acc_ref[...]
    if final:
        y = _rms(y, fg_ref[...])
    o_ref[...] = y


def _ffn(x, sc, sh, g2, ng, wup, wdn, fg, final):
    b, t, d = x.shape
    hidden = wdn.shape[0]
    tt = _tile(t, 256)
    xs = pl.BlockSpec((None, tt, d), lambda bb, i: (bb, i, 0))
    vec = pl.BlockSpec((None, 1, d), lambda bb, i: (bb, 0, 0))
    return pl.pallas_call(
        functools.partial(_ffn_kernel, hidden=hidden, ck=_tile(hidden, 256), final=final),
        out_shape=jax.ShapeDtypeStruct((b, t, d), F32),
        grid=(b, t // tt),
        in_specs=[xs, vec, vec, vec, _full((1, d)), _full(wup.shape), _full(wdn.shape), _full((1, d))],
        out_specs=xs,
        scratch_shapes=[pltpu.VMEM((tt, d), F32)],
        compiler_params=_params(("parallel", "parallel")),
        name="ffn",
    )(x, sc, sh, g2, ng, wup, wdn, fg)


def _pad_heads(w, heads, dim):
    w = w.reshape(w.shape[:-1] + (heads, dim))
    w = jnp.pad(w, [(0, 0)] * (w.ndim - 1) + [(0, LANE - dim)])
    return w.reshape(w.shape[:-2] + (heads * LANE,))


def _layer_weights(w_in, w_out, mla_w_qb, mla_w_kvb):
    d = w_in.shape[0]
    sizes = (256, 256, 128, 128, 256, 256, 256, 256, 256, 256, 128, 32)
    offs = [0]
    for n in sizes:
        offs.append(offs[-1] + n)
    col = lambda i: w_in[:, offs[i]:offs[i + 1]]
    w_in_p = jnp.concatenate([
        col(0),
        _pad_heads(col(1) * SWA_HEAD_DIM ** -0.5, SWA_HEADS, SWA_HEAD_DIM),
        _pad_heads(col(2), SWA_KV_HEADS, SWA_HEAD_DIM),
        _pad_heads(col(3), SWA_KV_HEADS, SWA_HEAD_DIM),
        col(4), col(5), col(6), col(7), col(8), col(9), col(10),
        jnp.pad(col(11), ((0, 0), (0, LANE - MLA_ROPE))),
    ], axis=1).astype(BF16)
    assert w_in_p.shape == (d, _N_INP)
    wqb_p = _pad_heads(mla_w_qb * MLA_SCALE, MLA_HEADS, MLA_NOPE + MLA_ROPE).astype(BF16)
    kvb = mla_w_kvb.reshape(MLA_KV_RANK, MLA_HEADS, MLA_NOPE + MLA_V)
    wk = _pad_heads(kvb[..., :MLA_NOPE].reshape(MLA_KV_RANK, -1), MLA_HEADS, MLA_NOPE)
    wv = _pad_heads(kvb[..., MLA_NOPE:].reshape(MLA_KV_RANK, -1), MLA_HEADS, MLA_V)
    wkv_p = jnp.concatenate([wk, wv], axis=1).astype(BF16)
    dmix = w_out.shape[0]
    rows = lambda lo, n: w_out[lo:lo + n]
    pad_rows = lambda w, heads, dim: _pad_heads(w.T, heads, dim).T
    wo_p = jnp.concatenate([rows(0, 256), pad_rows(rows(256, 256), SWA_HEADS, SWA_HEAD_DIM),
                            rows(512, 256), pad_rows(rows(768, 256), MLA_HEADS, MLA_V)], axis=0).astype(BF16)
    assert dmix == 1024
    return w_in_p, wqb_p, wkv_p, wo_p


def _rope_tables(length, dim, lo, ident):
    n_freq = dim // 4
    rows = length // GRID_W
    row = jnp.repeat(jnp.arange(rows, dtype=F32), GRID_W)
    col = jnp.tile(jnp.arange(GRID_W, dtype=F32), rows)
    inv = ROPE_BASE ** (-jnp.arange(n_freq, dtype=F32) / n_freq)
    ang = jnp.stack([row[:, None] * inv, col[:, None] * inv], axis=1)
    cos, sin = jnp.cos(ang), jnp.sin(ang)
    z = jnp.zeros_like(sin)
    cos_l = jnp.stack([cos, cos], axis=2).reshape(length, dim)
    sina = jnp.stack([-sin, z], axis=2).reshape(length, dim)
    sinb = jnp.stack([z, sin], axis=2).reshape(length, dim)
    if ident:
        cos_l, sina, sinb = jnp.ones_like(cos_l), jnp.zeros_like(sina), jnp.zeros_like(sinb)
    pad = lambda a, fill: jnp.pad(a, ((0, 0), (lo, LANE - lo - dim)), constant_values=fill)
    return jnp.stack([pad(cos_l, 1.0), pad(sina, 0.0), pad(sinb, 0.0)], axis=0)


def kernel(x, c, ctx, c_ctx, w_mod, b_mod, norm1_g, norm2_g, w_in, w_out, s5_lam_re, s5_lam_im, s5_log_dt,
           s5_b_re, s5_b_im, s5_c_re, s5_c_im, s5_d, s5_w_glu, s5_b_glu, swa_sink, hg_lb, hg_norm_g,
           mla_q_norm_g, mla_w_qb, mla_kv_norm_g, mla_w_kvb, ffn_w_up, ffn_w_down, final_norm_g):
    b, t, d = x.shape
    lc = ctx.shape[1]
    depth = w_mod.shape[0]

    cc = jnp.zeros((8, d), F32).at[:b].set(c).at[b].set(c_ctx)
    mods = _modulation(cc, w_mod, b_mod)

    lb_cum = jnp.cumsum(jax.nn.softmax(hg_lb.astype(F32), axis=1), axis=1)
    lb = lb_cum - lb_cum[:, :1]

    tab_swa = _rope_tables(t, SWA_HEAD_DIM, 0, False)
    tab_mla = _rope_tables(t, MLA_ROPE, MLA_NOPE, False)
    tab_swa_c = _rope_tables(lc, SWA_HEAD_DIM, 0, True)
    tab_mla_c = _rope_tables(lc, MLA_ROPE, MLA_NOPE, True)
    pavg = jnp.kron(jnp.eye(HG_HEADS, dtype=F32), jnp.full((HG_DV, HG_DV), 1.0 / HG_DV, F32)).astype(BF16)
    zero_st = jnp.zeros((b, HG_HEADS * HG_DV, HG_HEADS * HG_DK), F32)
    zero_s5 = jnp.zeros((S5_GROUPS, b, 4 * S5_STATE), F32)

    for i in range(depth):
        need_ctx = i < depth - 1
        mod = mods[i, :b].reshape(b, 6, 1, d)
        mod_c = jnp.broadcast_to(mods[i, b].reshape(1, 6, 1, d), (b, 6, 1, d))
        sh1, sc1, g1, sh2, sc2, g2 = (mod[:, j] for j in range(6))
        csh1, csc1, cg1, csh2, csc2, cg2 = (mod_c[:, j] for j in range(6))
        w_in_p, wqb_p, wkv_p, wo_p = _layer_weights(w_in[i], w_out[i], mla_w_qb[i], mla_w_kvb[i])
        n1 = norm1_g[i].reshape(1, d)
        qg = mla_q_norm_g[i].reshape(1, -1)
        kvg = mla_kv_norm_g[i].reshape(1, -1)
        px = _inproj(x, sc1, sh1, n1, w_in_p, tab_swa, tab_mla, qg, wqb_p, kvg, wkv_p)
        pc = _inproj(ctx, csc1, csh1, n1, w_in_p, tab_swa_c, tab_mla_c, qg, wqb_p, kvg, wkv_p)
        (xu, xsq, xsk, xsv, xhq, xhzf, xhzb, xhi, xhg, xmq, xmk, xmv) = px
        (cu, csq, csk, csv, chq, chzf, chzb, chi, chg, cmq, cmk, cmv) = pc

        mats = _s5_matrices(s5_lam_re[i], s5_lam_im[i], s5_log_dt[i], s5_b_re[i], s5_b_im[i],
                            s5_c_re[i], s5_c_im[i], s5_d[i])
        ya_c, s5_fin = _s5_scan(cu, mats, zero_s5)
        ya, _ = _s5_scan(xu, mats, s5_fin)

        grp = SWA_HEADS // SWA_KV_HEADS
        sink = swa_sink[i].astype(F32)
        yb = _attention(xsq, csk, csv, xsk, xsv, grp=grp, sink=sink, window=SWA_WINDOW)
        lbf, lbb = lb[0, i], lb[1, i]
        gp = jnp.stack([jnp.log(lbf), jnp.log1p(-lbf), 1.0 - lbf,
                        jnp.log(lbb), jnp.log1p(-lbb), 1.0 - lbb, lbf, lbb], axis=0)
        of_c, ob_c, stf, stb = _hgrn(chq, chzf, chzb, chi, gp, zero_st, zero_st)
        of, ob, _, _ = _hgrn(xhq, xhzf, xhzb, xhi, gp, stf, stb)
        yd = _attention(xmq, cmk, cmv, xmk, xmv, grp=1)

        wglu = s5_w_glu[i].astype(BF16)
        bglu = s5_b_glu[i].reshape(1, -1)
        hn = jnp.tile(hg_norm_g[i], HG_HEADS).reshape(1, -1)
        x = _outproj(x, g1, ya, yb, of, ob, xhg, yd, wglu, bglu, hn, pavg, wo_p)
        wup = ffn_w_up[i].astype(BF16)
        wdn = ffn_w_down[i].astype(BF16)
        n2 = norm2_g[i].reshape(1, d)
        fg = final_norm_g.reshape(1, d)
        x = _ffn(x, sc2, sh2, g2, n2, wup, wdn, fg, final=not need_ctx)
        if need_ctx:
            yb_c = _attention(csq, csk, csv, grp=grp, sink=sink)
            yd_c = _attention(cmq, cmk, cmv, grp=1)
            ctx = _outproj(ctx, cg1, ya_c, yb_c, of_c, ob_c, chg, yd_c, wglu, bglu, hn, pavg, wo_p)
            ctx = _ffn(ctx, csc2, csh2, cg2, n2, wup, wdn, fg, final=False)
    return x
```

```python
import functools
import math

import jax
import jax.numpy as jnp
from jax import lax
from jax.experimental import pallas as pl
from jax.experimental.pallas import tpu as pltpu

F32 = jnp.float32
BF16 = jnp.bfloat16

EPS = 1e-6
NEG_INF = -1e30
ROPE_BASE = 10000.0
GRID_W = 64
LANE = 128
VMEM_LIMIT = 56 * 1024 * 1024

S5_CH, S5_GROUP, S5_STATE = 256, 16, 64
S5_GROUPS = S5_CH // S5_GROUP
S5_CHUNK = 16
SWA_HEADS, SWA_KV_HEADS, SWA_HEAD_DIM, SWA_WINDOW = 4, 2, 64, 128
HG_HEADS, HG_DK, HG_DV = 4, 64, 64
HG_CHUNK = 64
MLA_HEADS, MLA_Q_RANK, MLA_KV_RANK = 4, 256, 128
MLA_NOPE, MLA_ROPE, MLA_V = 64, 32, 64
MLA_SCALE = (MLA_NOPE + MLA_ROPE) ** -0.5
LOG2E = 1.4426950408889634
V_ONE = 64


def _tile(n, pref):
    t = min(n, pref)
    assert n % t == 0, (n, pref)
    return t


def _params(sem):
    return pltpu.CompilerParams(dimension_semantics=sem, vmem_limit_bytes=VMEM_LIMIT)


def _full(shape):
    nd = len(shape)
    return pl.BlockSpec(shape, lambda *_: (0,) * nd, pipeline_mode=pl.Buffered(1))


def _mod_kernel(c_ref, w_ref, b_ref, o_ref):
    c = c_ref[...]
    s = c * (1.0 / (1.0 + jnp.exp(-c)))
    o_ref[...] = jnp.dot(s.astype(BF16), w_ref[...].astype(BF16), preferred_element_type=F32) + b_ref[...]


def _modulation(cc, w_mod, b_mod):
    depth, d, n = w_mod.shape
    tn = _tile(n, 1536)
    return pl.pallas_call(
        _mod_kernel,
        out_shape=jax.ShapeDtypeStruct((depth, 8, n), F32),
        grid=(depth, n // tn),
        in_specs=[pl.BlockSpec((8, d), lambda l, j: (0, 0)),
                  pl.BlockSpec((None, d, tn), lambda l, j: (l, 0, j)),
                  pl.BlockSpec((None, 1, tn), lambda l, j: (l, 0, j))],
        out_specs=pl.BlockSpec((None, 8, tn), lambda l, j: (l, 0, j)),
        compiler_params=_params(("parallel", "parallel")),
        name="modulation",
    )(cc, w_mod, b_mod.reshape(depth, 1, n))


def _rope_block(x, t_ref, half):
    return (x * t_ref[0] + pltpu.roll(x, LANE - half, 1) * t_ref[1]
            + pltpu.roll(x, half, 1) * t_ref[2])


def _rms(x, g):
    return x * lax.rsqrt(jnp.mean(x * x, axis=-1, keepdims=True) + EPS) * g


_C_U = 0
_C_SQ = 256
_C_SK = _C_SQ + 512
_C_SV = _C_SK + 256
_C_HG = _C_SV + 256
_C_CQ = _C_HG + 5 * 256
_C_CKV = _C_CQ + 256
_C_KR = _C_CKV + 128
_N_INP = _C_KR + 128


def _inproj_kernel(x_ref, sc_ref, sh_ref, g_ref, w_ref, ts_ref, tm_ref, qg_ref, wqb_ref, kvg_ref, wkv_ref,
                   u_ref, sq_ref, sk_ref, sv_ref, hq_ref, hzf_ref, hzb_ref, hi_ref, hg_ref,
                   mq_ref, mk_ref, mv_ref):
    h = _rms(x_ref[...], g_ref[...]) * (1.0 + sc_ref[...]) + sh_ref[...]
    hb = h.astype(BF16)

    def proj(lo, n):
        return jnp.dot(hb, w_ref[:, lo:lo + n], preferred_element_type=F32)

    u_ref[...] = proj(_C_U, 256)
    for hh in range(SWA_HEADS):
        sq_ref[:, hh * LANE:(hh + 1) * LANE] = _rope_block(proj(_C_SQ + hh * LANE, LANE), ts_ref, 16).astype(BF16)
    for hh in range(SWA_KV_HEADS):
        sk_ref[:, hh * LANE:(hh + 1) * LANE] = _rope_block(proj(_C_SK + hh * LANE, LANE), ts_ref, 16).astype(BF16)
    def with_ones(v):
        lane = lax.broadcasted_iota(jnp.int32, (1, v.shape[-1]), 1)
        return jnp.where(lane % LANE == V_ONE, 1.0, v).astype(BF16)

    sv_ref[...] = with_ones(proj(_C_SV, 256))
    for i, r in enumerate((hq_ref, hzf_ref, hzb_ref, hi_ref, hg_ref)):
        r[...] = proj(_C_HG + i * 256, 256)
    cq = _rms(proj(_C_CQ, 256), qg_ref[...])
    q = jnp.dot(cq.astype(BF16), wqb_ref[...], preferred_element_type=F32)
    ckv = _rms(proj(_C_CKV, 128), kvg_ref[...])
    kv = jnp.dot(ckv.astype(BF16), wkv_ref[...], preferred_element_type=F32)
    kr = pltpu.roll(proj(_C_KR, LANE), MLA_NOPE, 1)
    for hh in range(MLA_HEADS):
        sl = slice(hh * LANE, (hh + 1) * LANE)
        mq_ref[:, sl] = _rope_block(q[:, sl], tm_ref, 8).astype(BF16)
        mk_ref[:, sl] = _rope_block(kv[:, sl] + kr, tm_ref, 8).astype(BF16)
    mv_ref[...] = with_ones(kv[:, MLA_HEADS * LANE:])


def _inproj(x, sc, sh, g, w_in_p, tab_swa, tab_mla, qg, wqb_p, kvg, wkv_p):
    b, t, d = x.shape
    tt = _tile(t, 512)
    row = lambda n, dt: jax.ShapeDtypeStruct((b, t, n), dt)
    out_shape = (row(256, F32), row(512, BF16), row(256, BF16), row(256, BF16),
                 row(256, F32), row(256, F32), row(256, F32), row(256, F32), row(256, F32),
                 row(512, BF16), row(512, BF16), row(512, BF16))
    xs = lambda n: pl.BlockSpec((None, tt, n), lambda i, bb: (bb, i, 0))
    vec = pl.BlockSpec((None, 1, d), lambda i, bb: (bb, 0, 0))
    tab = pl.BlockSpec((3, tt, LANE), lambda i, bb: (0, i, 0))
    return pl.pallas_call(
        _inproj_kernel,
        out_shape=out_shape,
        grid=(t // tt, b),
        in_specs=[xs(d), vec, vec, _full((1, d)), _full(w_in_p.shape), tab, tab,
                  _full((1, 256)), _full(wqb_p.shape), _full((1, 128)), _full(wkv_p.shape)],
        out_specs=tuple(xs(s.shape[-1]) for s in out_shape),
        compiler_params=_params(("parallel", "parallel")),
        name="inproj",
    )(x, sc, sh, g, w_in_p, tab_swa, tab_mla, qg, wqb_p, kvg, wkv_p)


def _s5_kernel(u_ref, m_ref, f_ref, e_ref, a_ref, s0_ref, y_ref, sfin_ref, zs_ref, ssf_ref, ssb_ref, *, nb, nc):
    ub = u_ref[...].astype(BF16)
    zs_ref[...] = jnp.dot(ub, f_ref[...], preferred_element_type=F32)
    lane = lax.broadcasted_iota(jnp.int32, (1, 256), 1)
    fmask = (lane % LANE) < S5_STATE
    ar = a_ref[0:1, :]
    ai = a_ref[1:2, :]

    def step(i, states):
        new = []
        for bb in range(nb):
            s = states[bb]
            rf = bb * nc + i
            rb = bb * nc + (nc - 1 - i)
            ssf_ref[pl.ds(rf, 1), :] = s
            ssb_ref[pl.ds(rb, 1), :] = s
            z = jnp.where(fmask, zs_ref[pl.ds(rf, 1), :], zs_ref[pl.ds(rb, 1), :])
            re, im = s[:, :LANE], s[:, LANE:]
            new.append(jnp.concatenate([ar * re - ai * im, ar * im + ai * re], axis=1) + z)
        return tuple(new)

    init = tuple(s0_ref[bb:bb + 1, :] for bb in range(nb))
    fin = lax.fori_loop(0, nc, step, init)
    for bb in range(nb):
        sfin_ref[bb:bb + 1, :] = fin[bb]
    ss = jnp.where(fmask, ssf_ref[...], ssb_ref[...])
    y_ref[...] = (jnp.dot(ub, m_ref[...], preferred_element_type=F32)
                  + jnp.dot(ss.astype(BF16), e_ref[...], preferred_element_type=F32))


def _s5_scan(u, mats, s0):
    m, f, e, a16 = mats
    b, t, _ = u.shape
    nc = t // S5_CHUNK
    r = b * nc
    ug = u.reshape(b, nc, S5_CHUNK, S5_GROUPS, S5_GROUP).transpose(3, 0, 1, 2, 4).reshape(S5_GROUPS, r, 256)
    grp = lambda *s: pl.BlockSpec((None,) + s, lambda gidx: (gidx,) + (0,) * len(s))
    y, sfin = pl.pallas_call(
        functools.partial(_s5_kernel, nb=b, nc=nc),
        out_shape=(jax.ShapeDtypeStruct((S5_GROUPS, r, 256), F32),
                   jax.ShapeDtypeStruct((S5_GROUPS, b, 256), F32)),
        grid=(S5_GROUPS,),
        in_specs=[grp(r, 256), grp(256, 256), grp(256, 256), grp(256, 256), grp(2, LANE), grp(b, 256)],
        out_specs=(grp(r, 256), grp(b, 256)),
        scratch_shapes=[pltpu.VMEM((r, 256), F32)] * 3,
        compiler_params=_params(("parallel",)),
        name="s5",
    )(ug, m, f, e, a16, s0)
    y = y.reshape(S5_GROUPS, b, nc, S5_CHUNK, S5_GROUP).transpose(1, 2, 3, 0, 4).reshape(b, t, 256)
    return y, sfin


def _s5_matrices(lam_re, lam_im, log_dt, b_re, b_im, c_re, c_im, d_skip):
    L = S5_CHUNK
    lam = lax.complex(lam_re, lam_im)
    ldt = lam * jnp.exp(log_dt)
    a_bar = jnp.exp(ldt)
    bb = ((a_bar - 1.0) / lam)[..., None] * lax.complex(b_re, b_im)
    cc = lax.complex(c_re, c_im)
    k = jnp.arange(L + 1, dtype=F32)
    apow = jnp.exp(ldt[None] * k[:, None, None, None])
    kern = jnp.real(jnp.einsum('dghp,kdgp,dgpi->dkgih', cc, apow[:L], bb))
    jj = jnp.arange(L)[:, None]
    tt = jnp.arange(L)[None, :]
    lag_f = jnp.clip(tt - jj, 0, L - 1)
    lag_b = jnp.clip(jj - tt, 0, L - 1)
    kf = kern[0][lag_f] * (jj <= tt)[:, :, None, None, None]
    kb = kern[1][lag_b] * (jj >= tt)[:, :, None, None, None]
    dg = d_skip.reshape(S5_GROUPS, S5_GROUP)
    eye = jnp.eye(S5_GROUP, dtype=F32)
    kd = (jj == tt)[:, :, None, None, None] * (dg[:, None, :] * eye[None])[None, None]
    m = (kf + kb + kd).transpose(2, 0, 3, 1, 4).reshape(S5_GROUPS, L * S5_GROUP, L * S5_GROUP)
    pf = apow[L - 1 - jnp.arange(L), 0]
    pb = apow[jnp.arange(L), 1]
    zf = pf[..., None] * bb[0][None]
    zb = pb[..., None] * bb[1][None]
    fmat = jnp.concatenate([jnp.real(zf), jnp.real(zb), jnp.imag(zf), jnp.imag(zb)], axis=2)
    fmat = fmat.transpose(1, 0, 3, 2).reshape(S5_GROUPS, L * S5_GROUP, 4 * S5_STATE)
    wf = cc[0][None] * apow[1 + jnp.arange(L), 0][:, :, None, :]
    wb = cc[1][None] * apow[L - jnp.arange(L), 1][:, :, None, :]
    emat = jnp.concatenate([jnp.real(wf), jnp.real(wb), -jnp.imag(wf), -jnp.imag(wb)], axis=3)
    emat = emat.transpose(1, 3, 0, 2).reshape(S5_GROUPS, 4 * S5_STATE, L * S5_GROUP)
    al = apow[L]
    a16 = jnp.stack([jnp.concatenate([jnp.real(al[0]), jnp.real(al[1])], axis=-1),
                     jnp.concatenate([jnp.imag(al[0]), jnp.imag(al[1])], axis=-1)], axis=1)
    return m.astype(BF16), fmat.astype(BF16), emat.astype(BF16), a16.astype(F32)


def _attn_kernel(*refs, grp, bq, has_sink, has_seq, window, t_seq, bk):
    refs = list(refs)
    sink_ref = refs.pop(0) if has_sink else None
    q_ref, kc_ref, vc_ref = refs[:3]
    ks_ref, vs_ref = (refs[3], refs[4]) if has_seq else (None, None)
    o_ref = refs[-1]
    hk = pl.program_id(1)
    qi = pl.program_id(2)
    rows = grp * bq
    q = jnp.concatenate([q_ref[:, g * LANE:(g + 1) * LANE] for g in range(grp)], axis=0)

    def scores(k, mask=None):
        s = lax.dot_general(q, k, (((1,), (1,)), ((), ())), preferred_element_type=F32)
        return s if mask is None else jnp.where(mask, s, NEG_INF)

    s_parts, v_parts = [scores(kc_ref[...])], [vc_ref[...]]
    if has_seq and window is not None:
        wlen = bq + 2 * window
        start = jnp.clip(qi * bq - window, 0, t_seq - wlen)
        start = pl.multiple_of(start, LANE)
        qpos = qi * bq + lax.broadcasted_iota(jnp.int32, (rows, wlen), 0) % bq
        kpos = start + lax.broadcasted_iota(jnp.int32, (rows, wlen), 1)
        s_parts.append(scores(ks_ref[pl.ds(start, wlen), :], jnp.abs(qpos - kpos) <= window))
        v_parts.append(vs_ref[pl.ds(start, wlen), :])
    elif has_seq:
        s_parts.append(scores(ks_ref[0:bk, :]))
        v_parts.append(vs_ref[0:bk, :])
    m = functools.reduce(jnp.maximum, [jnp.max(s, axis=-1, keepdims=True) for s in s_parts])
    if has_sink:
        sink = jnp.concatenate([jnp.full((bq, 1), sink_ref[hk * grp + g], F32) for g in range(grp)], axis=0)
        m = jnp.maximum(m, sink)
    acc = sum(jnp.dot(jnp.exp2(s - m).astype(BF16), v, preferred_element_type=F32)
              for s, v in zip(s_parts, v_parts))
    if has_sink:
        one_lane = lax.broadcasted_iota(jnp.int32, (1, LANE), 1) == V_ONE
        acc = acc + jnp.where(one_lane, jnp.exp2(sink - m), 0.0)
    if has_seq and window is None:
        def body(j, carry):
            m, acc = carry
            st = pl.multiple_of(j * bk, bk)
            s = scores(ks_ref[pl.ds(st, bk), :])
            m_new = jnp.maximum(m, jnp.max(s, axis=-1, keepdims=True))
            p = jnp.exp2(s - m_new).astype(BF16)
            acc = jnp.exp2(m - m_new) * acc + jnp.dot(p, vs_ref[pl.ds(st, bk), :], preferred_element_type=F32)
            return m_new, acc
        m, acc = lax.fori_loop(1, t_seq // bk, body, (m, acc), unroll=True)
    o = acc * (1.0 / acc[:, V_ONE:V_ONE + 1])
    for g in range(grp):
        o_ref[:, g * LANE:(g + 1) * LANE] = o[g * bq:(g + 1) * bq].astype(o_ref.dtype)


def _attention(q, kc, vc, ks=None, vs=None, *, grp, sink=None, window=None, bq=512, bk=1024):
    b, tq, wq = q.shape
    hkv = wq // (grp * LANE)
    lc = kc.shape[1]
    has_seq = ks is not None
    t_seq = ks.shape[1] if has_seq else 0
    if has_seq and window is not None:
        bq = min(bq, (t_seq - 2 * window) // LANE * LANE)
    bq = _tile(tq, bq)
    if has_seq and window is None:
        bk = _tile(t_seq, bk)
    kern = functools.partial(_attn_kernel, grp=grp, bq=bq, has_sink=sink is not None, has_seq=has_seq,
                             window=window, t_seq=t_seq, bk=bk)
    in_specs, args = [], []
    if sink is not None:
        in_specs.append(pl.BlockSpec(memory_space=pltpu.SMEM))
        args.append(sink)
    in_specs.append(pl.BlockSpec((None, bq, grp * LANE), lambda bb, h, i: (bb, i, h)))
    args.append(q)
    ctx_spec = pl.BlockSpec((None, lc, LANE), lambda bb, h, i: (bb, 0, h))
    in_specs += [ctx_spec, ctx_spec]
    args += [kc, vc]
    if has_seq:
        seq_spec = pl.BlockSpec((None, t_seq, LANE), lambda bb, h, i: (bb, 0, h))
        in_specs += [seq_spec, seq_spec]
        args += [ks, vs]
    return pl.pallas_call(
        kern,
        out_shape=jax.ShapeDtypeStruct((b, tq, wq), BF16),
        grid=(b, hkv, tq // bq),
        in_specs=in_specs,
        out_specs=pl.BlockSpec((None, bq, grp * LANE), lambda bb, h, i: (bb, i, h)),
        compiler_params=_params(("parallel", "parallel", "parallel")),
        name="attention",
    )(*args)


def _split3(x):
    hi = x.astype(BF16)
    r = x - hi.astype(F32)
    mid = r.astype(BF16)
    lo = (r - mid.astype(F32)).astype(BF16)
    return hi, mid, lo


def _hg_chunk(q, z, v, st, gp, fwd):
    c = q.shape[0]
    log_lb, l1p, om = gp
    ls = jnp.minimum(z, 0.0) - jnp.log1p(jnp.exp(-jnp.abs(z)))
    bterm = l1p + ls
    mx = jnp.maximum(log_lb, bterm)
    lf = mx + jnp.log1p(jnp.exp(-jnp.abs(log_lb - bterm)))
    k = om / (1.0 + jnp.exp(z))
    ri = lax.broadcasted_iota(jnp.int32, (c, c), 0)
    ci = lax.broadcasted_iota(jnp.int32, (c, c), 1)
    tri = jnp.where((ci <= ri) if fwd else (ci >= ri), 1.0, 0.0).astype(BF16)
    gsum = sum(jnp.dot(tri, part, preferred_element_type=F32) for part in _split3(lf))
    tot = gsum[c - 1:c, :] if fwd else gsum[0:1, :]
    gm = gsum[c // 2:c // 2 + 1, :]
    qd = (q * jnp.exp(gsum - gm)).astype(BF16)
    kinv = k * jnp.exp(gm - gsum)
    q_in = (q * jnp.exp(gsum)).astype(BF16)
    k_end = (k * jnp.exp(tot - gsum)).astype(BF16)
    lane = lax.broadcasted_iota(jnp.int32, (1, HG_HEADS * HG_DK), 1)
    hmask = [(lane // HG_DK) == h for h in range(HG_HEADS)]
    kstack = jnp.concatenate([jnp.where(hm, kinv, 0.0) for hm in hmask], axis=0).astype(BF16)
    vstack = jnp.concatenate([jnp.where(hm, v, 0.0) for hm in hmask], axis=0).astype(BF16)
    att = lax.dot_general(qd, kstack, (((1,), (1,)), ((), ())), preferred_element_type=F32)
    ti = lax.broadcasted_iota(jnp.int32, (c, HG_HEADS * c), 0)
    si = lax.broadcasted_iota(jnp.int32, (c, HG_HEADS * c), 1) % c
    att = jnp.where((si <= ti) if fwd else (si >= ti), att, 0.0)
    o = jnp.dot(att.astype(BF16), vstack, preferred_element_type=F32)
    o = o + lax.dot_general(q_in, st.astype(BF16), (((1,), (1,)), ((), ())), preferred_element_type=F32)
    kv_t = lax.dot_general(v.astype(BF16), k_end, (((0,), (0,)), ((), ())), preferred_element_type=F32)
    r2 = lax.broadcasted_iota(jnp.int32, (HG_HEADS * HG_DV, HG_HEADS * HG_DK), 0) // HG_DV
    c2 = lax.broadcasted_iota(jnp.int32, (HG_HEADS * HG_DV, HG_HEADS * HG_DK), 1) // HG_DK
    st = st * jnp.exp(tot) + jnp.where(r2 == c2, kv_t, 0.0)
    return o, st


def _hg_kernel(gp_ref, qf_ref, zf_ref, vf_ref, qb_ref, zb_ref, vb_ref, s0f_ref, s0b_ref,
               of_ref, ob_ref, sf_ref, sb_ref, st_ref, *, nchunk):
    i = pl.program_id(1)
    c = HG_CHUNK

    @pl.when(i == 0)
    def _():
        st_ref[0] = s0f_ref[...]
        st_ref[1] = s0b_ref[...]

    gpf = tuple(gp_ref[r:r + 1, :] for r in range(3))
    gpb = tuple(gp_ref[r:r + 1, :] for r in range(3, 6))
    stf = st_ref[0]
    stb = st_ref[1]
    for n in range(nchunk):
        sl = slice(n * c, (n + 1) * c)
        o, stf = _hg_chunk(qf_ref[sl, :], zf_ref[sl, :], vf_ref[sl, :], stf, gpf, True)
        of_ref[sl, :] = o
        nb = nchunk - 1 - n
        sl = slice(nb * c, (nb + 1) * c)
        o, stb = _hg_chunk(qb_ref[sl, :], zb_ref[sl, :], vb_ref[sl, :], stb, gpb, False)
        ob_ref[sl, :] = o
    st_ref[0] = stf
    st_ref[1] = stb

    @pl.when(i == pl.num_programs(1) - 1)
    def _():
        sf_ref[...] = stf
        sb_ref[...] = stb


def _hgrn(q, zf, zb, v, gp, s0f, s0b):
    b, t, w = q.shape
    tb = _tile(t, 256)
    nblk = t // tb
    fw = pl.BlockSpec((None, tb, w), lambda bb, i: (bb, i, 0))
    bw = pl.BlockSpec((None, tb, w), lambda bb, i: (bb, nblk - 1 - i, 0))
    st = pl.BlockSpec((None, w, w), lambda bb, i: (bb, 0, 0))
    return pl.pallas_call(
        functools.partial(_hg_kernel, nchunk=tb // HG_CHUNK),
        out_shape=(jax.ShapeDtypeStruct((b, t, w), F32), jax.ShapeDtypeStruct((b, t, w), F32),
                   jax.ShapeDtypeStruct((b, w, w), F32), jax.ShapeDtypeStruct((b, w, w), F32)),
        grid=(b, nblk),
        in_specs=[_full((8, w)), fw, fw, fw, bw, bw, bw, st, st],
        out_specs=(fw, bw, st, st),
        scratch_shapes=[pltpu.VMEM((2, w, w), F32)],
        compiler_params=_params(("parallel", "arbitrary")),
        name="hgrn2",
    )(gp, q, zf, v, q, zb, v, s0f, s0b)


def _gelu_tanh(x):
    return 0.5 * x * (1.0 + jnp.tanh(math.sqrt(2.0 / math.pi) * (x + 0.044715 * (x * x * x))))


def _outproj_kernel(x_ref, g1_ref, ya_ref, yb_ref, of_ref, ob_ref, hg_ref, yd_ref,
                    wglu_ref, bglu_ref, hn_ref, pavg_ref, wo_ref, o_ref):
    ya = _gelu_tanh(ya_ref[...])
    gl = jnp.dot(ya.astype(BF16), wglu_ref[...], preferred_element_type=F32) + bglu_ref[...]
    ya = ya * (1.0 / (1.0 + jnp.exp(-gl)))
    o = of_ref[...] + ob_ref[...]
    o2 = o * o
    hi = o2.astype(BF16)
    lo = (o2 - hi.astype(F32)).astype(BF16)
    ms = (jnp.dot(hi, pavg_ref[...], preferred_element_type=F32)
          + jnp.dot(lo, pavg_ref[...], preferred_element_type=F32))
    gate = hg_ref[...]
    yc = o * lax.rsqrt(ms + EPS) * hn_ref[...] * (gate * (1.0 / (1.0 + jnp.exp(-gate))))
    y = (jnp.dot(ya.astype(BF16), wo_ref[0:256, :], preferred_element_type=F32)
         + jnp.dot(yb_ref[...], wo_ref[256:768, :], preferred_element_type=F32)
         + jnp.dot(yc.astype(BF16), wo_ref[768:1024, :], preferred_element_type=F32)
         + jnp.dot(yd_ref[...], wo_ref[1024:1536, :], preferred_element_type=F32))
    o_ref[...] = x_ref[...] + g1_ref[...] * y


def _outproj(x, g1, ya, yb, of, ob, hg, yd, wglu, bglu, hn, pavg, wo_p):
    b, t, d = x.shape
    tt = _tile(t, 512)
    xs = lambda n: pl.BlockSpec((None, tt, n), lambda bb, i: (bb, i, 0))
    vec = pl.BlockSpec((None, 1, d), lambda bb, i: (bb, 0, 0))
    return pl.pallas_call(
        _outproj_kernel,
        out_shape=jax.ShapeDtypeStruct((b, t, d), F32),
        grid=(b, t // tt),
        in_specs=[xs(d), vec, xs(256), xs(512), xs(256), xs(256), xs(256), xs(512),
                  _full(wglu.shape), _full((1, 256)), _full((1, 256)), _full(pavg.shape), _full(wo_p.shape)],
        out_specs=xs(d),
        compiler_params=_params(("parallel", "parallel")),
        name="outproj",
    )(x, g1, ya, yb, of, ob, hg, yd, wglu, bglu, hn, pavg, wo_p)


def _ffn_kernel(x_ref, sc_ref, sh_ref, g2_ref, ng_ref, wup_ref, wdn_ref, fg_ref, o_ref, acc_ref, *, hidden, ck, final):
    x = x_ref[...]
    hb = (_rms(x, ng_ref[...]) * (1.0 + sc_ref[...]) + sh_ref[...]).astype(BF16)
    for j in range(hidden // ck):
        gate = jnp.dot(hb, wup_ref[:, j * ck:(j + 1) * ck], preferred_element_type=F32)
        up = jnp.dot(hb, wup_ref[:, hidden + j * ck:hidden + (j + 1) * ck], preferred_element_type=F32)
        a = (gate * (1.0 / (1.0 + jnp.exp(-gate))) * up).astype(BF16)
        part = jnp.dot(a, wdn_ref[j * ck:(j + 1) * ck, :], preferred_element_type=F32)
        if j == 0:
            acc_ref[...] = part
        else:
            acc_ref[...] += part
    y = x + g2_ref[...] * acc_ref[...]
    if final:
        y = _rms(y, fg_ref[...])
    o_ref[...] = y


def _ffn(x, sc, sh, g2, ng, wup, wdn, fg, final):
    b, t, d = x.shape
    hidden = wdn.shape[0]
    tt = _tile(t, 512)
    xs = pl.BlockSpec((None, tt, d), lambda bb, i: (bb, i, 0))
    vec = pl.BlockSpec((None, 1, d), lambda bb, i: (bb, 0, 0))
    return pl.pallas_call(
        functools.partial(_ffn_kernel, hidden=hidden, ck=_tile(hidden, 256), final=final),
        out_shape=jax.ShapeDtypeStruct((b, t, d), F32),
        grid=(b, t // tt),
        in_specs=[xs, vec, vec, vec, _full((1, d)), _full(wup.shape), _full(wdn.shape), _full((1, d))],
        out_specs=xs,
        scratch_shapes=[pltpu.VMEM((tt, d), F32)],
        compiler_params=_params(("parallel", "parallel")),
        name="ffn",
    )(x, sc, sh, g2, ng, wup, wdn, fg)


def _pad_heads(w, heads, dim):
    w = w.reshape(w.shape[:-1] + (heads, dim))
    w = jnp.pad(w, [(0, 0)] * (w.ndim - 1) + [(0, LANE - dim)])
    return w.reshape(w.shape[:-2] + (heads * LANE,))


def _layer_weights(w_in, w_out, mla_w_qb, mla_w_kvb):
    d = w_in.shape[0]
    sizes = (256, 256, 128, 128, 256, 256, 256, 256, 256, 256, 128, 32)
    offs = [0]
    for n in sizes:
        offs.append(offs[-1] + n)
    col = lambda i: w_in[:, offs[i]:offs[i + 1]]
    w_in_p = jnp.concatenate([
        col(0),
        _pad_heads(col(1) * (SWA_HEAD_DIM ** -0.5 * LOG2E), SWA_HEADS, SWA_HEAD_DIM),
        _pad_heads(col(2), SWA_KV_HEADS, SWA_HEAD_DIM),
        _pad_heads(col(3), SWA_KV_HEADS, SWA_HEAD_DIM),
        col(4), col(5), col(6), col(7), col(8), col(9), col(10),
        jnp.pad(col(11), ((0, 0), (0, LANE - MLA_ROPE))),
    ], axis=1).astype(BF16)
    assert w_in_p.shape == (d, _N_INP)
    wqb_p = _pad_heads(mla_w_qb * (MLA_SCALE * LOG2E), MLA_HEADS, MLA_NOPE + MLA_ROPE).astype(BF16)
    kvb = mla_w_kvb.reshape(MLA_KV_RANK, MLA_HEADS, MLA_NOPE + MLA_V)
    wk = _pad_heads(kvb[..., :MLA_NOPE].reshape(MLA_KV_RANK, -1), MLA_HEADS, MLA_NOPE)
    wv = _pad_heads(kvb[..., MLA_NOPE:].reshape(MLA_KV_RANK, -1), MLA_HEADS, MLA_V)
    wkv_p = jnp.concatenate([wk, wv], axis=1).astype(BF16)
    dmix = w_out.shape[0]
    rows = lambda lo, n: w_out[lo:lo + n]
    pad_rows = lambda w, heads, dim: _pad_heads(w.T, heads, dim).T
    wo_p = jnp.concatenate([rows(0, 256), pad_rows(rows(256, 256), SWA_HEADS, SWA_HEAD_DIM),
                            rows(512, 256), pad_rows(rows(768, 256), MLA_HEADS, MLA_V)], axis=0).astype(BF16)
    assert dmix == 1024
    return w_in_p, wqb_p, wkv_p, wo_p


def _rope_tables(length, dim, lo, ident):
    n_freq = dim // 4
    rows = length // GRID_W
    row = jnp.repeat(jnp.arange(rows, dtype=F32), GRID_W)
    col = jnp.tile(jnp.arange(GRID_W, dtype=F32), rows)
    inv = ROPE_BASE ** (-jnp.arange(n_freq, dtype=F32) / n_freq)
    ang = jnp.stack([row[:, None] * inv, col[:, None] * inv], axis=1)
    cos, sin = jnp.cos(ang), jnp.sin(ang)
    z = jnp.zeros_like(sin)
    cos_l = jnp.stack([cos, cos], axis=2).reshape(length, dim)
    sina = jnp.stack([-sin, z], axis=2).reshape(length, dim)
    sinb = jnp.stack([z, sin], axis=2).reshape(length, dim)
    if ident:
        cos_l, sina, sinb = jnp.ones_like(cos_l), jnp.zeros_like(sina), jnp.zeros_like(sinb)
    pad = lambda a, fill: jnp.pad(a, ((0, 0), (lo, LANE - lo - dim)), constant_values=fill)
    return jnp.stack([pad(cos_l, 1.0), pad(sina, 0.0), pad(sinb, 0.0)], axis=0)


def kernel(x, c, ctx, c_ctx, w_mod, b_mod, norm1_g, norm2_g, w_in, w_out, s5_lam_re, s5_lam_im, s5_log_dt,
           s5_b_re, s5_b_im, s5_c_re, s5_c_im, s5_d, s5_w_glu, s5_b_glu, swa_sink, hg_lb, hg_norm_g,
           mla_q_norm_g, mla_w_qb, mla_kv_norm_g, mla_w_kvb, ffn_w_up, ffn_w_down, final_norm_g):
    b, t, d = x.shape
    lc = ctx.shape[1]
    depth = w_mod.shape[0]

    cc = jnp.zeros((8, d), F32).at[:b].set(c).at[b].set(c_ctx)
    mods = _modulation(cc, w_mod, b_mod)

    lb_cum = jnp.cumsum(jax.nn.softmax(hg_lb.astype(F32), axis=1), axis=1)
    lb = lb_cum - lb_cum[:, :1]

    tab_swa = _rope_tables(t, SWA_HEAD_DIM, 0, False)
    tab_mla = _rope_tables(t, MLA_ROPE, MLA_NOPE, False)
    tab_swa_c = _rope_tables(lc, SWA_HEAD_DIM, 0, True)
    tab_mla_c = _rope_tables(lc, MLA_ROPE, MLA_NOPE, True)
    pavg = jnp.kron(jnp.eye(HG_HEADS, dtype=F32), jnp.full((HG_DV, HG_DV), 1.0 / HG_DV, F32)).astype(BF16)
    zero_st = jnp.zeros((b, HG_HEADS * HG_DV, HG_HEADS * HG_DK), F32)
    zero_s5 = jnp.zeros((S5_GROUPS, b, 4 * S5_STATE), F32)

    for i in range(depth):
        need_ctx = i < depth - 1
        mod = mods[i, :b].reshape(b, 6, 1, d)
        mod_c = jnp.broadcast_to(mods[i, b].reshape(1, 6, 1, d), (b, 6, 1, d))
        sh1, sc1, g1, sh2, sc2, g2 = (mod[:, j] for j in range(6))
        csh1, csc1, cg1, csh2, csc2, cg2 = (mod_c[:, j] for j in range(6))
        w_in_p, wqb_p, wkv_p, wo_p = _layer_weights(w_in[i], w_out[i], mla_w_qb[i], mla_w_kvb[i])
        n1 = norm1_g[i].reshape(1, d)
        qg = mla_q_norm_g[i].reshape(1, -1)
        kvg = mla_kv_norm_g[i].reshape(1, -1)
        px = _inproj(x, sc1, sh1, n1, w_in_p, tab_swa, tab_mla, qg, wqb_p, kvg, wkv_p)
        pc = _inproj(ctx, csc1, csh1, n1, w_in_p, tab_swa_c, tab_mla_c, qg, wqb_p, kvg, wkv_p)
        (xu, xsq, xsk, xsv, xhq, xhzf, xhzb, xhi, xhg, xmq, xmk, xmv) = px
        (cu, csq, csk, csv, chq, chzf, chzb, chi, chg, cmq, cmk, cmv) = pc

        mats = _s5_matrices(s5_lam_re[i], s5_lam_im[i], s5_log_dt[i], s5_b_re[i], s5_b_im[i],
                            s5_c_re[i], s5_c_im[i], s5_d[i])
        ya_c, s5_fin = _s5_scan(cu, mats, zero_s5)
        ya, _ = _s5_scan(xu, mats, s5_fin)

        grp = SWA_HEADS // SWA_KV_HEADS
        sink = swa_sink[i].astype(F32) * LOG2E
        yb = _attention(xsq, csk, csv, xsk, xsv, grp=grp, sink=sink, window=SWA_WINDOW)
        lbf, lbb = lb[0, i], lb[1, i]
        gp = jnp.stack([jnp.log(lbf), jnp.log1p(-lbf), 1.0 - lbf,
                        jnp.log(lbb), jnp.log1p(-lbb), 1.0 - lbb, lbf, lbb], axis=0)
        of_c, ob_c, stf, stb = _hgrn(chq, chzf, chzb, chi, gp, zero_st, zero_st)
        of, ob, _, _ = _hgrn(xhq, xhzf, xhzb, xhi, gp, stf, stb)
        yd = _attention(xmq, cmk, cmv, xmk, xmv, grp=1)

        wglu = s5_w_glu[i].astype(BF16)
        bglu = s5_b_glu[i].reshape(1, -1)
        hn = jnp.tile(hg_norm_g[i], HG_HEADS).reshape(1, -1)
        x = _outproj(x, g1, ya, yb, of, ob, xhg, yd, wglu, bglu, hn, pavg, wo_p)
        wup = ffn_w_up[i].astype(BF16)
        wdn = ffn_w_down[i].astype(BF16)
        n2 = norm2_g[i].reshape(1, d)
        fg = final_norm_g.reshape(1, d)
        x = _ffn(x, sc2, sh2, g2, n2, wup, wdn, fg, final=not need_ctx)
        if need_ctx:
            yb_c = _attention(csq, csk, csv, grp=grp, sink=sink)
            yd_c = _attention(cmq, cmk, cmv, grp=1)
            ctx = _outproj(ctx, cg1, ya_c, yb_c, of_c, ob_c, chg, yd_c, wglu, bglu, hn, pavg, wo_p)
            ctx = _ffn(ctx, csc2, csh2, cg2, n2, wup, wdn, fg, final=False)
    return x
```

```python
import functools
import math

import jax
import jax.numpy as jnp
from jax import lax
from jax.experimental import pallas as pl
from jax.experimental.pallas import tpu as pltpu

F32 = jnp.float32
BF16 = jnp.bfloat16

EPS = 1e-6
NEG_INF = -1e30
ROPE_BASE = 10000.0
GRID_W = 64
LANE = 128
VMEM_LIMIT = 56 * 1024 * 1024

S5_CH, S5_GROUP, S5_STATE = 256, 16, 64
S5_GROUPS = S5_CH // S5_GROUP
S5_CHUNK = 16
SWA_HEADS, SWA_KV_HEADS, SWA_HEAD_DIM, SWA_WINDOW = 4, 2, 64, 128
HG_HEADS, HG_DK, HG_DV = 4, 64, 64
HG_CHUNK = 64
MLA_HEADS, MLA_Q_RANK, MLA_KV_RANK = 4, 256, 128
MLA_NOPE, MLA_ROPE, MLA_V = 64, 32, 64
MLA_SCALE = (MLA_NOPE + MLA_ROPE) ** -0.5
LOG2E = 1.4426950408889634
V_ONE = 64


def _tile(n, pref):
    t = min(n, pref)
    assert n % t == 0, (n, pref)
    return t


def _params(sem):
    return pltpu.CompilerParams(dimension_semantics=sem, vmem_limit_bytes=VMEM_LIMIT)


def _full(shape):
    nd = len(shape)
    return pl.BlockSpec(shape, lambda *_: (0,) * nd, pipeline_mode=pl.Buffered(1))


def _mod_kernel(c_ref, w_ref, b_ref, o_ref):
    c = c_ref[...]
    s = c * (1.0 / (1.0 + jnp.exp(-c)))
    o_ref[...] = jnp.dot(s.astype(BF16), w_ref[...].astype(BF16), preferred_element_type=F32) + b_ref[...]


def _modulation(cc, w_mod, b_mod):
    depth, d, n = w_mod.shape
    tn = _tile(n, 1536)
    return pl.pallas_call(
        _mod_kernel,
        out_shape=jax.ShapeDtypeStruct((depth, 8, n), F32),
        grid=(depth, n // tn),
        in_specs=[pl.BlockSpec((8, d), lambda l, j: (0, 0)),
                  pl.BlockSpec((None, d, tn), lambda l, j: (l, 0, j)),
                  pl.BlockSpec((None, 1, tn), lambda l, j: (l, 0, j))],
        out_specs=pl.BlockSpec((None, 8, tn), lambda l, j: (l, 0, j)),
        compiler_params=_params(("parallel", "parallel")),
        name="modulation",
    )(cc, w_mod, b_mod.reshape(depth, 1, n))


def _rope_block(x, t_ref, half):
    return (x * t_ref[0] + pltpu.roll(x, LANE - half, 1) * t_ref[1]
            + pltpu.roll(x, half, 1) * t_ref[2])


def _rms(x, g):
    return x * lax.rsqrt(jnp.mean(x * x, axis=-1, keepdims=True) + EPS) * g


_C_U = 0
_C_SQ = 256
_C_SK = _C_SQ + 512
_C_SV = _C_SK + 256
_C_HG = _C_SV + 256
_C_CQ = _C_HG + 5 * 256
_C_CKV = _C_CQ + 256
_C_KR = _C_CKV + 128
_N_INP = _C_KR + 128


def _inproj_kernel(x_ref, sc_ref, sh_ref, g_ref, w_ref, ts_ref, tm_ref, qg_ref, wqb_ref, kvg_ref, wkv_ref,
                   u_ref, sq_ref, sk_ref, sv_ref, hq_ref, hzf_ref, hzb_ref, hi_ref, hg_ref,
                   mq_ref, mk_ref, mv_ref, us_ref):
    h = _rms(x_ref[...], g_ref[...]) * (1.0 + sc_ref[...]) + sh_ref[...]
    hb = h.astype(BF16)

    def proj(lo, n):
        return jnp.dot(hb, w_ref[:, lo:lo + n], preferred_element_type=F32)

    u = proj(_C_U, 256)
    for v in range(2):
        us_ref[v] = u[:, v * LANE:(v + 1) * LANE]
    for j in range(S5_CHUNK):
        for v in range(2):
            u_ref[j, :, v * LANE:(v + 1) * LANE] = us_ref[v, pl.ds(j, us_ref.shape[1] // S5_CHUNK, stride=S5_CHUNK), :]
    for hh in range(SWA_HEADS):
        sq_ref[:, hh * LANE:(hh + 1) * LANE] = _rope_block(proj(_C_SQ + hh * LANE, LANE), ts_ref, 16).astype(BF16)
    for hh in range(SWA_KV_HEADS):
        sk_ref[:, hh * LANE:(hh + 1) * LANE] = _rope_block(proj(_C_SK + hh * LANE, LANE), ts_ref, 16).astype(BF16)
    def with_ones(v):
        lane = lax.broadcasted_iota(jnp.int32, (1, v.shape[-1]), 1)
        return jnp.where(lane % LANE == V_ONE, 1.0, v).astype(BF16)

    sv_ref[...] = with_ones(proj(_C_SV, 256))
    for i, r in enumerate((hq_ref, hzf_ref, hzb_ref, hi_ref, hg_ref)):
        r[...] = proj(_C_HG + i * 256, 256)
    cq = _rms(proj(_C_CQ, 256), qg_ref[...])
    q = jnp.dot(cq.astype(BF16), wqb_ref[...], preferred_element_type=F32)
    ckv = _rms(proj(_C_CKV, 128), kvg_ref[...])
    kv = jnp.dot(ckv.astype(BF16), wkv_ref[...], preferred_element_type=F32)
    kr = pltpu.roll(proj(_C_KR, LANE), MLA_NOPE, 1)
    for hh in range(MLA_HEADS):
        sl = slice(hh * LANE, (hh + 1) * LANE)
        mq_ref[:, sl] = _rope_block(q[:, sl], tm_ref, 8).astype(BF16)
        mk_ref[:, sl] = _rope_block(kv[:, sl] + kr, tm_ref, 8).astype(BF16)
    mv_ref[...] = with_ones(kv[:, MLA_HEADS * LANE:])


def _inproj(x, sc, sh, g, w_in_p, tab_swa, tab_mla, qg, wqb_p, kvg, wkv_p):
    b, t, d = x.shape
    tt = _tile(t, 512)
    row = lambda n, dt: jax.ShapeDtypeStruct((b, t, n), dt)
    out_shape = (jax.ShapeDtypeStruct((b, S5_CHUNK, t // S5_CHUNK, 256), F32),
                 row(512, BF16), row(256, BF16), row(256, BF16),
                 row(256, F32), row(256, F32), row(256, F32), row(256, F32), row(256, F32),
                 row(512, BF16), row(512, BF16), row(512, BF16))
    xs = lambda n: pl.BlockSpec((None, tt, n), lambda i, bb: (bb, i, 0))
    vec = pl.BlockSpec((None, 1, d), lambda i, bb: (bb, 0, 0))
    tab = pl.BlockSpec((3, tt, LANE), lambda i, bb: (0, i, 0))
    return pl.pallas_call(
        _inproj_kernel,
        out_shape=out_shape,
        grid=(t // tt, b),
        in_specs=[xs(d), vec, vec, _full((1, d)), _full(w_in_p.shape), tab, tab,
                  _full((1, 256)), _full(wqb_p.shape), _full((1, 128)), _full(wkv_p.shape)],
        out_specs=(pl.BlockSpec((None, S5_CHUNK, tt // S5_CHUNK, 256), lambda i, bb: (bb, 0, i, 0)),)
        + tuple(xs(s.shape[-1]) for s in out_shape[1:]),
        scratch_shapes=[pltpu.VMEM((2, tt, LANE), F32)],
        compiler_params=_params(("parallel", "parallel")),
        name="inproj",
    )(x, sc, sh, g, w_in_p, tab_swa, tab_mla, qg, wqb_p, kvg, wkv_p)


def _gather_lane_blocks(pieces):
    blk = lax.broadcasted_iota(jnp.int32, (1, LANE), 1) // S5_GROUP
    out = None
    for k, (arr, src) in enumerate(pieces):
        shift = (S5_GROUP * (k - src)) % LANE
        moved = pltpu.roll(arr, shift, 1) if shift else arr
        out = moved if out is None else jnp.where(blk == k, moved, out)
    return out


def _s5_kernel(u_ref, m_ref, f_ref, e_ref, a_ref, s0_ref, y_ref, sfin_ref, zs_ref, ss_ref, ug_ref, *, nc):
    ng, half = S5_GROUPS, LANE // S5_GROUP
    for g in range(ng):
        v, src = g // half, g % half
        ug = jnp.concatenate(
            [_gather_lane_blocks([(u_ref[half * hv + k, :, v * LANE:(v + 1) * LANE], src) for k in range(half)])
             for hv in range(2)], axis=1).astype(BF16)
        ug_ref[g] = ug
        z = jnp.dot(ug, f_ref[g], preferred_element_type=F32)
        for c in range(2):
            zs_ref[c, pl.ds(g, nc, stride=ng), :] = z[:, c * LANE:(c + 1) * LANE]

    fmask = lax.broadcasted_iota(jnp.int32, (1, LANE), 1) < S5_STATE
    ar, ai = a_ref[0], a_ref[1]

    def step(i, s):
        rf = pl.multiple_of(i * ng, ng)
        rb = pl.multiple_of((nc - 1 - i) * ng, ng)
        for c in range(2):
            ss_ref[c, pl.ds(rf, ng), 0:S5_STATE] = s[c][:, 0:S5_STATE]
            ss_ref[c, pl.ds(rb, ng), S5_STATE:LANE] = s[c][:, S5_STATE:LANE]
        zre, zim = (jnp.where(fmask, zs_ref[c, pl.ds(rf, ng), :], zs_ref[c, pl.ds(rb, ng), :]) for c in range(2))
        re, im = s
        return ar * re - ai * im + zre, ar * im + ai * re + zim

    re, im = lax.fori_loop(0, nc, step, (s0_ref[:, :LANE], s0_ref[:, LANE:]), unroll=4)
    sfin_ref[:, :LANE] = re
    sfin_ref[:, LANE:] = im

    for g in range(ng):
        ss = jnp.concatenate([ss_ref[c, pl.ds(g, nc, stride=ng), :] for c in range(2)], axis=1).astype(BF16)
        y = (jnp.dot(ug_ref[g], m_ref[g], preferred_element_type=F32)
             + jnp.dot(ss, e_ref[g], preferred_element_type=F32))
        for c in range(2):
            zs_ref[c, g * nc:(g + 1) * nc, :] = y[:, c * LANE:(c + 1) * LANE]
    for t in range(S5_CHUNK):
        v, src = t // half, t % half
        y_ref[t] = jnp.concatenate(
            [_gather_lane_blocks([(zs_ref[v, (half * hv + k) * nc:(half * hv + k + 1) * nc, :], src)
                                  for k in range(half)]) for hv in range(2)], axis=1)


def _s5_scan(uj, mats, s0):
    m, f, e, a16 = mats
    b, _, nc, _ = uj.shape
    slab = pl.BlockSpec((None, S5_CHUNK, nc, 256), lambda bb: (bb, 0, 0, 0), pipeline_mode=pl.Buffered(1))
    st = pl.BlockSpec((None, S5_GROUPS, 256), lambda bb: (bb, 0, 0))
    return pl.pallas_call(
        functools.partial(_s5_kernel, nc=nc),
        out_shape=(jax.ShapeDtypeStruct(uj.shape, F32), jax.ShapeDtypeStruct((b, S5_GROUPS, 256), F32)),
        grid=(b,),
        in_specs=[slab, _full(m.shape), _full(f.shape), _full(e.shape), _full(a16.shape), st],
        out_specs=(slab, st),
        scratch_shapes=[pltpu.VMEM((2, S5_GROUPS * nc, LANE), F32), pltpu.VMEM((2, S5_GROUPS * nc, LANE), F32),
                        pltpu.VMEM((S5_GROUPS, nc, 256), BF16)],
        compiler_params=_params(("parallel",)),
        name="s5",
    )(uj, m, f, e, a16, s0)


def _s5_matrices(lam_re, lam_im, log_dt, b_re, b_im, c_re, c_im, d_skip):
    L = S5_CHUNK
    lam = lax.complex(lam_re, lam_im)
    ldt = lam * jnp.exp(log_dt)
    a_bar = jnp.exp(ldt)
    bb = ((a_bar - 1.0) / lam)[..., None] * lax.complex(b_re, b_im)
    cc = lax.complex(c_re, c_im)
    k = jnp.arange(L + 1, dtype=F32)
    apow = jnp.exp(ldt[None] * k[:, None, None, None])
    kern = jnp.real(jnp.einsum('dghp,kdgp,dgpi->dkgih', cc, apow[:L], bb))
    jj = jnp.arange(L)[:, None]
    tt = jnp.arange(L)[None, :]
    lag_f = jnp.clip(tt - jj, 0, L - 1)
    lag_b = jnp.clip(jj - tt, 0, L - 1)
    kf = kern[0][lag_f] * (jj <= tt)[:, :, None, None, None]
    kb = kern[1][lag_b] * (jj >= tt)[:, :, None, None, None]
    dg = d_skip.reshape(S5_GROUPS, S5_GROUP)
    eye = jnp.eye(S5_GROUP, dtype=F32)
    kd = (jj == tt)[:, :, None, None, None] * (dg[:, None, :] * eye[None])[None, None]
    m = (kf + kb + kd).transpose(2, 0, 3, 1, 4).reshape(S5_GROUPS, L * S5_GROUP, L * S5_GROUP)
    pf = apow[L - 1 - jnp.arange(L), 0]
    pb = apow[jnp.arange(L), 1]
    zf = pf[..., None] * bb[0][None]
    zb = pb[..., None] * bb[1][None]
    fmat = jnp.concatenate([jnp.real(zf), jnp.real(zb), jnp.imag(zf), jnp.imag(zb)], axis=2)
    fmat = fmat.transpose(1, 0, 3, 2).reshape(S5_GROUPS, L * S5_GROUP, 4 * S5_STATE)
    wf = cc[0][None] * apow[1 + jnp.arange(L), 0][:, :, None, :]
    wb = cc[1][None] * apow[L - jnp.arange(L), 1][:, :, None, :]
    emat = jnp.concatenate([jnp.real(wf), jnp.real(wb), -jnp.imag(wf), -jnp.imag(wb)], axis=3)
    emat = emat.transpose(1, 3, 0, 2).reshape(S5_GROUPS, 4 * S5_STATE, L * S5_GROUP)
    al = apow[L]
    a16 = jnp.stack([jnp.concatenate([jnp.real(al[0]), jnp.real(al[1])], axis=-1),
                     jnp.concatenate([jnp.imag(al[0]), jnp.imag(al[1])], axis=-1)], axis=0)
    return m.astype(BF16), fmat.astype(BF16), emat.astype(BF16), a16.astype(F32)


def _attn_kernel(*refs, grp, bq, has_sink, has_seq, window, t_seq, bk):
    refs = list(refs)
    sink_ref = refs.pop(0) if has_sink else None
    q_ref, kc_ref, vc_ref = refs[:3]
    ks_ref, vs_ref = (refs[3], refs[4]) if has_seq else (None, None)
    o_ref = refs[-1]
    hk = pl.program_id(1)
    qi = pl.program_id(2)
    rows = grp * bq
    q = jnp.concatenate([q_ref[:, g * LANE:(g + 1) * LANE] for g in range(grp)], axis=0)

    def scores(k, mask=None):
        s = lax.dot_general(q, k, (((1,), (1,)), ((), ())), preferred_element_type=F32)
        return s if mask is None else jnp.where(mask, s, NEG_INF)

    s_parts, v_parts = [scores(kc_ref[...])], [vc_ref[...]]
    if has_seq and window is not None:
        wlen = bq + 2 * window
        start = jnp.clip(qi * bq - window, 0, t_seq - wlen)
        start = pl.multiple_of(start, LANE)
        qpos = qi * bq + lax.broadcasted_iota(jnp.int32, (rows, wlen), 0) % bq
        kpos = start + lax.broadcasted_iota(jnp.int32, (rows, wlen), 1)
        s_parts.append(scores(ks_ref[pl.ds(start, wlen), :], jnp.abs(qpos - kpos) <= window))
        v_parts.append(vs_ref[pl.ds(start, wlen), :])
    elif has_seq:
        s_parts.append(scores(ks_ref[0:bk, :]))
        v_parts.append(vs_ref[0:bk, :])
    m = functools.reduce(jnp.maximum, [jnp.max(s, axis=-1, keepdims=True) for s in s_parts])
    if has_sink:
        sink = jnp.concatenate([jnp.full((bq, 1), sink_ref[hk * grp + g], F32) for g in range(grp)], axis=0)
        m = jnp.maximum(m, sink)
    acc = sum(jnp.dot(jnp.exp2(s - m).astype(BF16), v, preferred_element_type=F32)
              for s, v in zip(s_parts, v_parts))
    if has_sink:
        one_lane = lax.broadcasted_iota(jnp.int32, (1, LANE), 1) == V_ONE
        acc = acc + jnp.where(one_lane, jnp.exp2(sink - m), 0.0)
    if has_seq and window is None:
        def body(j, carry):
            m, acc = carry
            st = pl.multiple_of(j * bk, bk)
            s = scores(ks_ref[pl.ds(st, bk), :])
            m_new = jnp.maximum(m, jnp.max(s, axis=-1, keepdims=True))
            p = jnp.exp2(s - m_new).astype(BF16)
            acc = jnp.exp2(m - m_new) * acc + jnp.dot(p, vs_ref[pl.ds(st, bk), :], preferred_element_type=F32)
            return m_new, acc
        m, acc = lax.fori_loop(1, t_seq // bk, body, (m, acc), unroll=True)
    o = acc * (1.0 / acc[:, V_ONE:V_ONE + 1])
    for g in range(grp):
        o_ref[:, g * LANE:(g + 1) * LANE] = o[g * bq:(g + 1) * bq].astype(o_ref.dtype)


def _attention(q, kc, vc, ks=None, vs=None, *, grp, sink=None, window=None, bq=512, bk=1024):
    b, tq, wq = q.shape
    hkv = wq // (grp * LANE)
    lc = kc.shape[1]
    has_seq = ks is not None
    t_seq = ks.shape[1] if has_seq else 0
    if has_seq and window is not None:
        bq = min(bq, (t_seq - 2 * window) // LANE * LANE)
    bq = _tile(tq, bq)
    if has_seq and window is None:
        bk = _tile(t_seq, bk)
    kern = functools.partial(_attn_kernel, grp=grp, bq=bq, has_sink=sink is not None, has_seq=has_seq,
                             window=window, t_seq=t_seq, bk=bk)
    in_specs, args = [], []
    if sink is not None:
        in_specs.append(pl.BlockSpec(memory_space=pltpu.SMEM))
        args.append(sink)
    in_specs.append(pl.BlockSpec((None, bq, grp * LANE), lambda bb, h, i: (bb, i, h)))
    args.append(q)
    ctx_spec = pl.BlockSpec((None, lc, LANE), lambda bb, h, i: (bb, 0, h))
    in_specs += [ctx_spec, ctx_spec]
    args += [kc, vc]
    if has_seq:
        seq_spec = pl.BlockSpec((None, t_seq, LANE), lambda bb, h, i: (bb, 0, h))
        in_specs += [seq_spec, seq_spec]
        args += [ks, vs]
    return pl.pallas_call(
        kern,
        out_shape=jax.ShapeDtypeStruct((b, tq, wq), BF16),
        grid=(b, hkv, tq // bq),
        in_specs=in_specs,
        out_specs=pl.BlockSpec((None, bq, grp * LANE), lambda bb, h, i: (bb, i, h)),
        compiler_params=_params(("parallel", "parallel", "parallel")),
        name="attention",
    )(*args)


def _split3(x):
    hi = x.astype(BF16)
    r = x - hi.astype(F32)
    mid = r.astype(BF16)
    lo = (r - mid.astype(F32)).astype(BF16)
    return hi, mid, lo


def _hg_chunk(q, z, v, st, gp, fwd):
    c = q.shape[0]
    log_lb, l1p, om = gp
    ls = jnp.minimum(z, 0.0) - jnp.log(1.0 + jnp.exp(-jnp.abs(z)))
    bterm = l1p + ls
    mx = jnp.maximum(log_lb, bterm)
    lf = mx + jnp.log(1.0 + jnp.exp(-jnp.abs(log_lb - bterm)))
    k = om / (1.0 + jnp.exp(z))
    ri = lax.broadcasted_iota(jnp.int32, (c, c), 0)
    ci = lax.broadcasted_iota(jnp.int32, (c, c), 1)
    tri = jnp.where((ci <= ri) if fwd else (ci >= ri), 1.0, 0.0).astype(BF16)
    gsum = sum(jnp.dot(tri, part, preferred_element_type=F32) for part in _split3(lf))
    tot = gsum[c - 1:c, :] if fwd else gsum[0:1, :]
    gm = gsum[c // 2:c // 2 + 1, :]
    qd = (q * jnp.exp(gsum - gm)).astype(BF16)
    kinv = k * jnp.exp(gm - gsum)
    q_in = (q * jnp.exp(gsum)).astype(BF16)
    k_end = (k * jnp.exp(tot - gsum)).astype(BF16)
    lane = lax.broadcasted_iota(jnp.int32, (1, HG_HEADS * HG_DK), 1)
    hmask = [(lane // HG_DK) == h for h in range(HG_HEADS)]
    kstack = jnp.concatenate([jnp.where(hm, kinv, 0.0) for hm in hmask], axis=0).astype(BF16)
    vstack = jnp.concatenate([jnp.where(hm, v, 0.0) for hm in hmask], axis=0).astype(BF16)
    att = lax.dot_general(qd, kstack, (((1,), (1,)), ((), ())), preferred_element_type=F32)
    ti = lax.broadcasted_iota(jnp.int32, (c, HG_HEADS * c), 0)
    si = lax.broadcasted_iota(jnp.int32, (c, HG_HEADS * c), 1) % c
    att = jnp.where((si <= ti) if fwd else (si >= ti), att, 0.0)
    o = jnp.dot(att.astype(BF16), vstack, preferred_element_type=F32)
    o = o + lax.dot_general(q_in, st.astype(BF16), (((1,), (1,)), ((), ())), preferred_element_type=F32)
    kv_t = lax.dot_general(v.astype(BF16), k_end, (((0,), (0,)), ((), ())), preferred_element_type=F32)
    r2 = lax.broadcasted_iota(jnp.int32, (HG_HEADS * HG_DV, HG_HEADS * HG_DK), 0) // HG_DV
    c2 = lax.broadcasted_iota(jnp.int32, (HG_HEADS * HG_DV, HG_HEADS * HG_DK), 1) // HG_DK
    st = st * jnp.exp(tot) + jnp.where(r2 == c2, kv_t, 0.0)
    return o, st


def _hg_kernel(gp_ref, qf_ref, zf_ref, vf_ref, qb_ref, zb_ref, vb_ref, s0f_ref, s0b_ref,
               of_ref, ob_ref, sf_ref, sb_ref, st_ref, *, nchunk):
    i = pl.program_id(1)
    c = HG_CHUNK

    @pl.when(i == 0)
    def _():
        st_ref[0] = s0f_ref[...]
        st_ref[1] = s0b_ref[...]

    gpf = tuple(gp_ref[r:r + 1, :] for r in range(3))
    gpb = tuple(gp_ref[r:r + 1, :] for r in range(3, 6))
    stf = st_ref[0]
    stb = st_ref[1]
    for n in range(nchunk):
        sl = slice(n * c, (n + 1) * c)
        o, stf = _hg_chunk(qf_ref[sl, :], zf_ref[sl, :], vf_ref[sl, :], stf, gpf, True)
        of_ref[sl, :] = o
        nb = nchunk - 1 - n
        sl = slice(nb * c, (nb + 1) * c)
        o, stb = _hg_chunk(qb_ref[sl, :], zb_ref[sl, :], vb_ref[sl, :], stb, gpb, False)
        ob_ref[sl, :] = o
    st_ref[0] = stf
    st_ref[1] = stb

    @pl.when(i == pl.num_programs(1) - 1)
    def _():
        sf_ref[...] = stf
        sb_ref[...] = stb


def _hgrn(q, zf, zb, v, gp, s0f, s0b):
    b, t, w = q.shape
    tb = _tile(t, 256)
    nblk = t // tb
    fw = pl.BlockSpec((None, tb, w), lambda bb, i: (bb, i, 0))
    bw = pl.BlockSpec((None, tb, w), lambda bb, i: (bb, nblk - 1 - i, 0))
    st = pl.BlockSpec((None, w, w), lambda bb, i: (bb, 0, 0))
    return pl.pallas_call(
        functools.partial(_hg_kernel, nchunk=tb // HG_CHUNK),
        out_shape=(jax.ShapeDtypeStruct((b, t, w), F32), jax.ShapeDtypeStruct((b, t, w), F32),
                   jax.ShapeDtypeStruct((b, w, w), F32), jax.ShapeDtypeStruct((b, w, w), F32)),
        grid=(b, nblk),
        in_specs=[_full((8, w)), fw, fw, fw, bw, bw, bw, st, st],
        out_specs=(fw, bw, st, st),
        scratch_shapes=[pltpu.VMEM((2, w, w), F32)],
        compiler_params=_params(("parallel", "arbitrary")),
        name="hgrn2",
    )(gp, q, zf, v, q, zb, v, s0f, s0b)


def _gelu_tanh(x):
    return 0.5 * x * (1.0 + jnp.tanh(math.sqrt(2.0 / math.pi) * (x + 0.044715 * (x * x * x))))


def _outproj_kernel(x_ref, g1_ref, ya_ref, yb_ref, of_ref, ob_ref, hg_ref, yd_ref,
                    wglu_ref, bglu_ref, hn_ref, pavg_ref, wo_ref, o_ref, ys_ref):
    for j in range(S5_CHUNK):
        for v in range(2):
            ys_ref[v, pl.ds(j, ys_ref.shape[1] // S5_CHUNK, stride=S5_CHUNK), :] = ya_ref[j, :, v * LANE:(v + 1) * LANE]
    ya = _gelu_tanh(jnp.concatenate([ys_ref[0], ys_ref[1]], axis=1))
    gl = jnp.dot(ya.astype(BF16), wglu_ref[...], preferred_element_type=F32) + bglu_ref[...]
    ya = ya * (1.0 / (1.0 + jnp.exp(-gl)))
    o = of_ref[...] + ob_ref[...]
    o2 = o * o
    hi = o2.astype(BF16)
    lo = (o2 - hi.astype(F32)).astype(BF16)
    ms = (jnp.dot(hi, pavg_ref[...], preferred_element_type=F32)
          + jnp.dot(lo, pavg_ref[...], preferred_element_type=F32))
    gate = hg_ref[...]
    yc = o * lax.rsqrt(ms + EPS) * hn_ref[...] * (gate * (1.0 / (1.0 + jnp.exp(-gate))))
    y = (jnp.dot(ya.astype(BF16), wo_ref[0:256, :], preferred_element_type=F32)
         + jnp.dot(yb_ref[...], wo_ref[256:768, :], preferred_element_type=F32)
         + jnp.dot(yc.astype(BF16), wo_ref[768:1024, :], preferred_element_type=F32)
         + jnp.dot(yd_ref[...], wo_ref[1024:1536, :], preferred_element_type=F32))
    o_ref[...] = x_ref[...] + g1_ref[...] * y


def _outproj(x, g1, ya, yb, of, ob, hg, yd, wglu, bglu, hn, pavg, wo_p):
    b, t, d = x.shape
    tt = _tile(t, 512)
    xs = lambda n: pl.BlockSpec((None, tt, n), lambda bb, i: (bb, i, 0))
    vec = pl.BlockSpec((None, 1, d), lambda bb, i: (bb, 0, 0))
    return pl.pallas_call(
        _outproj_kernel,
        out_shape=jax.ShapeDtypeStruct((b, t, d), F32),
        grid=(b, t // tt),
        in_specs=[xs(d), vec, pl.BlockSpec((None, S5_CHUNK, tt // S5_CHUNK, 256), lambda bb, i: (bb, 0, i, 0)),
                  xs(512), xs(256), xs(256), xs(256), xs(512),
                  _full(wglu.shape), _full((1, 256)), _full((1, 256)), _full(pavg.shape), _full(wo_p.shape)],
        out_specs=xs(d),
        scratch_shapes=[pltpu.VMEM((2, tt, LANE), F32)],
        compiler_params=_params(("parallel", "parallel")),
        name="outproj",
    )(x, g1, ya, yb, of, ob, hg, yd, wglu, bglu, hn, pavg, wo_p)


def _ffn_kernel(x_ref, sc_ref, sh_ref, g2_ref, ng_ref, wup_ref, wdn_ref, fg_ref, o_ref, acc_ref, *, hidden, ck, final):
    x = x_ref[...]
    hb = (_rms(x, ng_ref[...]) * (1.0 + sc_ref[...]) + sh_ref[...]).astype(BF16)
    for j in range(hidden // ck):
        gate = jnp.dot(hb, wup_ref[:, j * ck:(j + 1) * ck], preferred_element_type=F32)
        up = jnp.dot(hb, wup_ref[:, hidden + j * ck:hidden + (j + 1) * ck], preferred_element_type=F32)
        a = (gate * (1.0 / (1.0 + jnp.exp(-gate))) * up).astype(BF16)
        part = jnp.dot(a, wdn_ref[j * ck:(j + 1) * ck, :], preferred_element_type=F32)
        if j == 0:
            acc_ref[...] = part
        else:
            acc_ref[...] += part
    y = x + g2_ref[...] * acc_ref[...]
    if final:
        y = _rms(y, fg_ref[...])
    o_ref[...] = y


def _ffn(x, sc, sh, g2, ng, wup, wdn, fg, final):
    b, t, d = x.shape
    hidden = wdn.shape[0]
    tt = _tile(t, 512)
    xs = pl.BlockSpec((None, tt, d), lambda bb, i: (bb, i, 0))
    vec = pl.BlockSpec((None, 1, d), lambda bb, i: (bb, 0, 0))
    return pl.pallas_call(
        functools.partial(_ffn_kernel, hidden=hidden, ck=_tile(hidden, 256), final=final),
        out_shape=jax.ShapeDtypeStruct((b, t, d), F32),
        grid=(b, t // tt),
        in_specs=[xs, vec, vec, vec, _full((1, d)), _full(wup.shape), _full(wdn.shape), _full((1, d))],
        out_specs=xs,
        scratch_shapes=[pltpu.VMEM((tt, d), F32)],
        compiler_params=_params(("parallel", "parallel")),
        name="ffn",
    )(x, sc, sh, g2, ng, wup, wdn, fg)


def _pad_heads(w, heads, dim):
    w = w.reshape(w.shape[:-1] + (heads, dim))
    w = jnp.pad(w, [(0, 0)] * (w.ndim - 1) + [(0, LANE - dim)])
    return w.reshape(w.shape[:-2] + (heads * LANE,))


def _layer_weights(w_in, w_out, mla_w_qb, mla_w_kvb):
    d = w_in.shape[0]
    sizes = (256, 256, 128, 128, 256, 256, 256, 256, 256, 256, 128, 32)
    offs = [0]
    for n in sizes:
        offs.append(offs[-1] + n)
    col = lambda i: w_in[:, offs[i]:offs[i + 1]]
    w_in_p = jnp.concatenate([
        col(0),
        _pad_heads(col(1) * (SWA_HEAD_DIM ** -0.5 * LOG2E), SWA_HEADS, SWA_HEAD_DIM),
        _pad_heads(col(2), SWA_KV_HEADS, SWA_HEAD_DIM),
        _pad_heads(col(3), SWA_KV_HEADS, SWA_HEAD_DIM),
        col(4), col(5), col(6), col(7), col(8), col(9), col(10),
        jnp.pad(col(11), ((0, 0), (0, LANE - MLA_ROPE))),
    ], axis=1).astype(BF16)
    assert w_in_p.shape == (d, _N_INP)
    wqb_p = _pad_heads(mla_w_qb * (MLA_SCALE * LOG2E), MLA_HEADS, MLA_NOPE + MLA_ROPE).astype(BF16)
    kvb = mla_w_kvb.reshape(MLA_KV_RANK, MLA_HEADS, MLA_NOPE + MLA_V)
    wk = _pad_heads(kvb[..., :MLA_NOPE].reshape(MLA_KV_RANK, -1), MLA_HEADS, MLA_NOPE)
    wv = _pad_heads(kvb[..., MLA_NOPE:].reshape(MLA_KV_RANK, -1), MLA_HEADS, MLA_V)
    wkv_p = jnp.concatenate([wk, wv], axis=1).astype(BF16)
    dmix = w_out.shape[0]
    rows = lambda lo, n: w_out[lo:lo + n]
    pad_rows = lambda w, heads, dim: _pad_heads(w.T, heads, dim).T
    wo_p = jnp.concatenate([rows(0, 256), pad_rows(rows(256, 256), SWA_HEADS, SWA_HEAD_DIM),
                            rows(512, 256), pad_rows(rows(768, 256), MLA_HEADS, MLA_V)], axis=0).astype(BF16)
    assert dmix == 1024
    return w_in_p, wqb_p, wkv_p, wo_p


def _rope_tables(length, dim, lo, ident):
    n_freq = dim // 4
    rows = length // GRID_W
    row = jnp.repeat(jnp.arange(rows, dtype=F32), GRID_W)
    col = jnp.tile(jnp.arange(GRID_W, dtype=F32), rows)
    inv = ROPE_BASE ** (-jnp.arange(n_freq, dtype=F32) / n_freq)
    ang = jnp.stack([row[:, None] * inv, col[:, None] * inv], axis=1)
    cos, sin = jnp.cos(ang), jnp.sin(ang)
    z = jnp.zeros_like(sin)
    cos_l = jnp.stack([cos, cos], axis=2).reshape(length, dim)
    sina = jnp.stack([-sin, z], axis=2).reshape(length, dim)
    sinb = jnp.stack([z, sin], axis=2).reshape(length, dim)
    if ident:
        cos_l, sina, sinb = jnp.ones_like(cos_l), jnp.zeros_like(sina), jnp.zeros_like(sinb)
    pad = lambda a, fill: jnp.pad(a, ((0, 0), (lo, LANE - lo - dim)), constant_values=fill)
    return jnp.stack([pad(cos_l, 1.0), pad(sina, 0.0), pad(sinb, 0.0)], axis=0)


def kernel(x, c, ctx, c_ctx, w_mod, b_mod, norm1_g, norm2_g, w_in, w_out, s5_lam_re, s5_lam_im, s5_log_dt,
           s5_b_re, s5_b_im, s5_c_re, s5_c_im, s5_d, s5_w_glu, s5_b_glu, swa_sink, hg_lb, hg_norm_g,
           mla_q_norm_g, mla_w_qb, mla_kv_norm_g, mla_w_kvb, ffn_w_up, ffn_w_down, final_norm_g):
    b, t, d = x.shape
    lc = ctx.shape[1]
    depth = w_mod.shape[0]

    cc = jnp.zeros((8, d), F32).at[:b].set(c).at[b].set(c_ctx)
    mods = _modulation(cc, w_mod, b_mod)

    lb_cum = jnp.cumsum(jax.nn.softmax(hg_lb.astype(F32), axis=1), axis=1)
    lb = lb_cum - lb_cum[:, :1]

    tab_swa = _rope_tables(t, SWA_HEAD_DIM, 0, False)
    tab_mla = _rope_tables(t, MLA_ROPE, MLA_NOPE, False)
    tab_swa_c = _rope_tables(lc, SWA_HEAD_DIM, 0, True)
    tab_mla_c = _rope_tables(lc, MLA_ROPE, MLA_NOPE, True)
    pavg = jnp.kron(jnp.eye(HG_HEADS, dtype=F32), jnp.full((HG_DV, HG_DV), 1.0 / HG_DV, F32)).astype(BF16)
    zero_st = jnp.zeros((b, HG_HEADS * HG_DV, HG_HEADS * HG_DK), F32)
    zero_s5 = jnp.zeros((b, S5_GROUPS, 4 * S5_STATE), F32)

    for i in range(depth):
        need_ctx = i < depth - 1
        mod = mods[i, :b].reshape(b, 6, 1, d)
        mod_c = jnp.broadcast_to(mods[i, b].reshape(1, 6, 1, d), (b, 6, 1, d))
        sh1, sc1, g1, sh2, sc2, g2 = (mod[:, j] for j in range(6))
        csh1, csc1, cg1, csh2, csc2, cg2 = (mod_c[:, j] for j in range(6))
        w_in_p, wqb_p, wkv_p, wo_p = _layer_weights(w_in[i], w_out[i], mla_w_qb[i], mla_w_kvb[i])
        n1 = norm1_g[i].reshape(1, d)
        qg = mla_q_norm_g[i].reshape(1, -1)
        kvg = mla_kv_norm_g[i].reshape(1, -1)
        px = _inproj(x, sc1, sh1, n1, w_in_p, tab_swa, tab_mla, qg, wqb_p, kvg, wkv_p)
        pc = _inproj(ctx, csc1, csh1, n1, w_in_p, tab_swa_c, tab_mla_c, qg, wqb_p, kvg, wkv_p)
        (xu, xsq, xsk, xsv, xhq, xhzf, xhzb, xhi, xhg, xmq, xmk, xmv) = px
        (cu, csq, csk, csv, chq, chzf, chzb, chi, chg, cmq, cmk, cmv) = pc

        mats = _s5_matrices(s5_lam_re[i], s5_lam_im[i], s5_log_dt[i], s5_b_re[i], s5_b_im[i],
                            s5_c_re[i], s5_c_im[i], s5_d[i])
        ya_c, s5_fin = _s5_scan(cu, mats, zero_s5)
        ya, _ = _s5_scan(xu, mats, s5_fin)

        grp = SWA_HEADS // SWA_KV_HEADS
        sink = swa_sink[i].astype(F32) * LOG2E
        yb = _attention(xsq, csk, csv, xsk, xsv, grp=grp, sink=sink, window=SWA_WINDOW)
        lbf, lbb = lb[0, i], lb[1, i]
        gp = jnp.stack([jnp.log(lbf), jnp.log1p(-lbf), 1.0 - lbf,
                        jnp.log(lbb), jnp.log1p(-lbb), 1.0 - lbb, lbf, lbb], axis=0)
        of_c, ob_c, stf, stb = _hgrn(chq, chzf, chzb, chi, gp, zero_st, zero_st)
        of, ob, _, _ = _hgrn(xhq, xhzf, xhzb, xhi, gp, stf, stb)
        yd = _attention(xmq, cmk, cmv, xmk, xmv, grp=1)

        wglu = s5_w_glu[i].astype(BF16)
        bglu = s5_b_glu[i].reshape(1, -1)
        hn = jnp.tile(hg_norm_g[i], HG_HEADS).reshape(1, -1)
        x = _outproj(x, g1, ya, yb, of, ob, xhg, yd, wglu, bglu, hn, pavg, wo_p)
        wup = ffn_w_up[i].astype(BF16)
        wdn = ffn_w_down[i].astype(BF16)
        n2 = norm2_g[i].reshape(1, d)
        fg = final_norm_g.reshape(1, d)
        x = _ffn(x, sc2, sh2, g2, n2, wup, wdn, fg, final=not need_ctx)
        if need_ctx:
            yb_c = _attention(csq, csk, csv, grp=grp, sink=sink)
            yd_c = _attention(cmq, cmk, cmv, grp=1)
            ctx = _outproj(ctx, cg1, ya_c, yb_c, of_c, ob_c, chg, yd_c, wglu, bglu, hn, pavg, wo_p)
            ctx = _ffn(ctx, csc2, csh2, cg2, n2, wup, wdn, fg, final=False)
    return x
```

```python
import functools
import math

import jax
import jax.numpy as jnp
from jax import lax
from jax.experimental import pallas as pl
from jax.experimental.pallas import tpu as pltpu

F32 = jnp.float32
BF16 = jnp.bfloat16

EPS = 1e-6
NEG_INF = -1e30
ROPE_BASE = 10000.0
GRID_W = 64
LANE = 128
VMEM_LIMIT = 56 * 1024 * 1024

S5_CH, S5_GROUP, S5_STATE = 256, 16, 64
S5_GROUPS = S5_CH // S5_GROUP
S5_CHUNK = 16
SWA_HEADS, SWA_KV_HEADS, SWA_HEAD_DIM, SWA_WINDOW = 4, 2, 64, 128
HG_HEADS, HG_DK, HG_DV = 4, 64, 64
HG_CHUNK = 64
MLA_HEADS, MLA_Q_RANK, MLA_KV_RANK = 4, 256, 128
MLA_NOPE, MLA_ROPE, MLA_V = 64, 32, 64
MLA_SCALE = (MLA_NOPE + MLA_ROPE) ** -0.5
LOG2E = 1.4426950408889634
V_ONE = 64


def _tile(n, pref):
    t = min(n, pref)
    assert n % t == 0, (n, pref)
    return t


def _params(sem):
    return pltpu.CompilerParams(dimension_semantics=sem, vmem_limit_bytes=VMEM_LIMIT)


def _full(shape):
    nd = len(shape)
    return pl.BlockSpec(shape, lambda *_: (0,) * nd, pipeline_mode=pl.Buffered(1))


def _mod_kernel(c_ref, w_ref, b_ref, o_ref):
    c = c_ref[...]
    s = c * (1.0 / (1.0 + jnp.exp(-c)))
    o_ref[...] = jnp.dot(s.astype(BF16), w_ref[...].astype(BF16), preferred_element_type=F32) + b_ref[...]


def _modulation(cc, w_mod, b_mod):
    depth, d, n = w_mod.shape
    tn = _tile(n, 1536)
    return pl.pallas_call(
        _mod_kernel,
        out_shape=jax.ShapeDtypeStruct((depth, 8, n), F32),
        grid=(depth, n // tn),
        in_specs=[pl.BlockSpec((8, d), lambda l, j: (0, 0)),
                  pl.BlockSpec((None, d, tn), lambda l, j: (l, 0, j)),
                  pl.BlockSpec((None, 1, tn), lambda l, j: (l, 0, j))],
        out_specs=pl.BlockSpec((None, 8, tn), lambda l, j: (l, 0, j)),
        compiler_params=_params(("parallel", "parallel")),
        name="modulation",
    )(cc, w_mod, b_mod.reshape(depth, 1, n))


def _rope_block(x, t_ref, half):
    return (x * t_ref[0] + pltpu.roll(x, LANE - half, 1) * t_ref[1]
            + pltpu.roll(x, half, 1) * t_ref[2])


def _rms(x, g):
    return x * lax.rsqrt(jnp.mean(x * x, axis=-1, keepdims=True) + EPS) * g


_C_U = 0
_C_SQ = 256
_C_SKV = 512
_C_HG = 768
_C_CQ = _C_HG + 5 * 256
_C_CKV = _C_CQ + 256
_N_INP = _C_CKV + 256


def _inproj_kernel(x_ref, sc_ref, sh_ref, g_ref, w_ref, ts_ref, tm_ref, qg_ref, wqb_ref, kvg_ref, wkv_ref,
                   u_ref, sq_ref, sk_ref, sv_ref, hq_ref, hzf_ref, hzb_ref, hi_ref, hg_ref,
                   mq_ref, mk_ref, mv_ref, us_ref):
    h = _rms(x_ref[...], g_ref[...]) * (1.0 + sc_ref[...]) + sh_ref[...]
    hb = h.astype(BF16)

    def proj(lo, n):
        return jnp.dot(hb, w_ref[:, lo:lo + n], preferred_element_type=F32)

    cq = _rms(proj(_C_CQ, 256), qg_ref[...]).astype(BF16)
    ckv_kr = proj(_C_CKV, 256)
    ckv = _rms(ckv_kr[:, :LANE], kvg_ref[...]).astype(BF16)
    kr = pltpu.roll(ckv_kr[:, LANE:], MLA_NOPE, 1)

    u = proj(_C_U, 256)
    for v in range(2):
        us_ref[v] = u[:, v * LANE:(v + 1) * LANE]
    for j in range(S5_CHUNK):
        for v in range(2):
            u_ref[j, :, v * LANE:(v + 1) * LANE] = us_ref[v, pl.ds(j, us_ref.shape[1] // S5_CHUNK, stride=S5_CHUNK), :]
    lane = lax.broadcasted_iota(jnp.int32, (1, LANE), 1)
    low = lane < SWA_HEAD_DIM
    one = jnp.where(lane == V_ONE, 1.0, 0.0)

    def spread(pair, fill):
        return [jnp.where(low, blk, fill).astype(BF16) for blk in (pair, pltpu.roll(pair, SWA_HEAD_DIM, 1))]

    sq = proj(_C_SQ, 256)
    for pp in range(SWA_HEADS // 2):
        heads = spread(_rope_block(sq[:, pp * LANE:(pp + 1) * LANE], ts_ref, 16), 0.0)
        for e in range(2):
            sq_ref[:, (2 * pp + e) * LANE:(2 * pp + e + 1) * LANE] = heads[e]
    skv = proj(_C_SKV, 256)
    for e, blk in enumerate(spread(_rope_block(skv[:, :LANE], ts_ref, 16), 0.0)):
        sk_ref[:, e * LANE:(e + 1) * LANE] = blk
    for e, blk in enumerate(spread(skv[:, LANE:], one)):
        sv_ref[:, e * LANE:(e + 1) * LANE] = blk

    q = jnp.dot(cq, wqb_ref[...], preferred_element_type=F32)
    for hh in range(MLA_HEADS):
        sl = slice(hh * LANE, (hh + 1) * LANE)
        mq_ref[:, sl] = _rope_block(q[:, sl], tm_ref, 8).astype(BF16)
    for i, r in enumerate((hq_ref, hzf_ref, hzb_ref)):
        r[...] = proj(_C_HG + i * 256, 256)
    kv = jnp.dot(ckv, wkv_ref[...], preferred_element_type=F32)
    for hh in range(MLA_HEADS):
        sl = slice(hh * LANE, (hh + 1) * LANE)
        mk_ref[:, sl] = _rope_block(kv[:, sl] + kr, tm_ref, 8).astype(BF16)
        mv_ref[:, sl] = jnp.where(lane == V_ONE, 1.0, kv[:, MLA_HEADS * LANE + hh * LANE:][:, :LANE]).astype(BF16)
    for i, r in enumerate((hi_ref, hg_ref)):
        r[...] = proj(_C_HG + (3 + i) * 256, 256)


def _inproj(x, sc, sh, g, w_in_p, tab_swa, tab_mla, qg, wqb_p, kvg, wkv_p):
    b, t, d = x.shape
    tt = _tile(t, 512)
    row = lambda n, dt: jax.ShapeDtypeStruct((b, t, n), dt)
    out_shape = (jax.ShapeDtypeStruct((b, S5_CHUNK, t // S5_CHUNK, 256), F32),
                 row(512, BF16), row(256, BF16), row(256, BF16),
                 row(256, F32), row(256, F32), row(256, F32), row(256, F32), row(256, F32),
                 row(512, BF16), row(512, BF16), row(512, BF16))
    xs = lambda n: pl.BlockSpec((None, tt, n), lambda i, bb: (bb, i, 0))
    vec = pl.BlockSpec((None, 1, d), lambda i, bb: (bb, 0, 0))
    tab = pl.BlockSpec((3, tt, LANE), lambda i, bb: (0, i, 0))
    return pl.pallas_call(
        _inproj_kernel,
        out_shape=out_shape,
        grid=(t // tt, b),
        in_specs=[xs(d), vec, vec, _full((1, d)), _full(w_in_p.shape), tab, tab,
                  _full((1, 256)), _full(wqb_p.shape), _full((1, 128)), _full(wkv_p.shape)],
        out_specs=(pl.BlockSpec((None, S5_CHUNK, tt // S5_CHUNK, 256), lambda i, bb: (bb, 0, i, 0)),)
        + tuple(xs(s.shape[-1]) for s in out_shape[1:]),
        scratch_shapes=[pltpu.VMEM((2, tt, LANE), F32)],
        compiler_params=_params(("parallel", "parallel")),
        name="inproj",
    )(x, sc, sh, g, w_in_p, tab_swa, tab_mla, qg, wqb_p, kvg, wkv_p)


def _gather_lane_blocks(pieces):
    blk = lax.broadcasted_iota(jnp.int32, (1, LANE), 1) // S5_GROUP
    out = None
    for k, (arr, src) in enumerate(pieces):
        shift = (S5_GROUP * (k - src)) % LANE
        moved = pltpu.roll(arr, shift, 1) if shift else arr
        out = moved if out is None else jnp.where(blk == k, moved, out)
    return out


def _s5_kernel(u_ref, m_ref, f_ref, e_ref, a_ref, s0_ref, y_ref, sfin_ref, zs_ref, ss_ref, ug_ref, *, nc):
    ng, half = S5_GROUPS, LANE // S5_GROUP
    for g in range(ng):
        v, src = g // half, g % half
        ug = jnp.concatenate(
            [_gather_lane_blocks([(u_ref[half * hv + k, :, v * LANE:(v + 1) * LANE], src) for k in range(half)])
             for hv in range(2)], axis=1).astype(BF16)
        ug_ref[g] = ug
        z = jnp.dot(ug, f_ref[g], preferred_element_type=F32)
        for c in range(2):
            zs_ref[c, pl.ds(g, nc, stride=ng), :] = z[:, c * LANE:(c + 1) * LANE]

    fmask = lax.broadcasted_iota(jnp.int32, (1, LANE), 1) < S5_STATE
    ar, ai = a_ref[0], a_ref[1]

    def step(i, s):
        rf = pl.multiple_of(i * ng, ng)
        rb = pl.multiple_of((nc - 1 - i) * ng, ng)
        for c in range(2):
            ss_ref[c, pl.ds(rf, ng), 0:S5_STATE] = s[c][:, 0:S5_STATE]
            ss_ref[c, pl.ds(rb, ng), S5_STATE:LANE] = s[c][:, S5_STATE:LANE]
        zre, zim = (jnp.where(fmask, zs_ref[c, pl.ds(rf, ng), :], zs_ref[c, pl.ds(rb, ng), :]) for c in range(2))
        re, im = s
        return ar * re - ai * im + zre, ar * im + ai * re + zim

    re, im = lax.fori_loop(0, nc, step, (s0_ref[:, :LANE], s0_ref[:, LANE:]), unroll=4)
    sfin_ref[:, :LANE] = re
    sfin_ref[:, LANE:] = im

    for g in range(ng):
        ss = jnp.concatenate([ss_ref[c, pl.ds(g, nc, stride=ng), :] for c in range(2)], axis=1).astype(BF16)
        y = (jnp.dot(ug_ref[g], m_ref[g], preferred_element_type=F32)
             + jnp.dot(ss, e_ref[g], preferred_element_type=F32))
        for c in range(2):
            zs_ref[c, g * nc:(g + 1) * nc, :] = y[:, c * LANE:(c + 1) * LANE]
    for t in range(S5_CHUNK):
        v, src = t // half, t % half
        y_ref[t] = jnp.concatenate(
            [_gather_lane_blocks([(zs_ref[v, (half * hv + k) * nc:(half * hv + k + 1) * nc, :], src)
                                  for k in range(half)]) for hv in range(2)], axis=1)


def _s5_scan(uj, mats, s0):
    m, f, e, a16 = mats
    b, _, nc, _ = uj.shape
    slab = pl.BlockSpec((None, S5_CHUNK, nc, 256), lambda bb: (bb, 0, 0, 0), pipeline_mode=pl.Buffered(1))
    st = pl.BlockSpec((None, S5_GROUPS, 256), lambda bb: (bb, 0, 0))
    return pl.pallas_call(
        functools.partial(_s5_kernel, nc=nc),
        out_shape=(jax.ShapeDtypeStruct(uj.shape, F32), jax.ShapeDtypeStruct((b, S5_GROUPS, 256), F32)),
        grid=(b,),
        in_specs=[slab, _full(m.shape), _full(f.shape), _full(e.shape), _full(a16.shape), st],
        out_specs=(slab, st),
        scratch_shapes=[pltpu.VMEM((2, S5_GROUPS * nc, LANE), F32), pltpu.VMEM((2, S5_GROUPS * nc, LANE), F32),
                        pltpu.VMEM((S5_GROUPS, nc, 256), BF16)],
        compiler_params=_params(("parallel",)),
        name="s5",
    )(uj, m, f, e, a16, s0)


def _s5_matrices(lam_re, lam_im, log_dt, b_re, b_im, c_re, c_im, d_skip):
    L = S5_CHUNK
    lam = lax.complex(lam_re, lam_im)
    ldt = lam * jnp.exp(log_dt)
    a_bar = jnp.exp(ldt)
    bb = ((a_bar - 1.0) / lam)[..., None] * lax.complex(b_re, b_im)
    cc = lax.complex(c_re, c_im)
    k = jnp.arange(L + 1, dtype=F32)
    apow = jnp.exp(ldt[None] * k[:, None, None, None])
    kern = jnp.real(jnp.einsum('dghp,kdgp,dgpi->dkgih', cc, apow[:L], bb))
    jj = jnp.arange(L)[:, None]
    tt = jnp.arange(L)[None, :]
    lag_f = jnp.clip(tt - jj, 0, L - 1)
    lag_b = jnp.clip(jj - tt, 0, L - 1)
    kf = kern[0][lag_f] * (jj <= tt)[:, :, None, None, None]
    kb = kern[1][lag_b] * (jj >= tt)[:, :, None, None, None]
    dg = d_skip.reshape(S5_GROUPS, S5_GROUP)
    eye = jnp.eye(S5_GROUP, dtype=F32)
    kd = (jj == tt)[:, :, None, None, None] * (dg[:, None, :] * eye[None])[None, None]
    m = (kf + kb + kd).transpose(2, 0, 3, 1, 4).reshape(S5_GROUPS, L * S5_GROUP, L * S5_GROUP)
    pf = apow[L - 1 - jnp.arange(L), 0]
    pb = apow[jnp.arange(L), 1]
    zf = pf[..., None] * bb[0][None]
    zb = pb[..., None] * bb[1][None]
    fmat = jnp.concatenate([jnp.real(zf), jnp.real(zb), jnp.imag(zf), jnp.imag(zb)], axis=2)
    fmat = fmat.transpose(1, 0, 3, 2).reshape(S5_GROUPS, L * S5_GROUP, 4 * S5_STATE)
    wf = cc[0][None] * apow[1 + jnp.arange(L), 0][:, :, None, :]
    wb = cc[1][None] * apow[L - jnp.arange(L), 1][:, :, None, :]
    emat = jnp.concatenate([jnp.real(wf), jnp.real(wb), -jnp.imag(wf), -jnp.imag(wb)], axis=3)
    emat = emat.transpose(1, 3, 0, 2).reshape(S5_GROUPS, 4 * S5_STATE, L * S5_GROUP)
    al = apow[L]
    a16 = jnp.stack([jnp.concatenate([jnp.real(al[0]), jnp.real(al[1])], axis=-1),
                     jnp.concatenate([jnp.imag(al[0]), jnp.imag(al[1])], axis=-1)], axis=0)
    return m.astype(BF16), fmat.astype(BF16), emat.astype(BF16), a16.astype(F32)


def _attn_kernel(*refs, grp, bq, sub, has_sink, has_seq, window, t_seq, bk):
    refs = list(refs)
    sink_ref = refs.pop(0) if has_sink else None
    q_ref, kc_ref, vc_ref = refs[:3]
    ks_ref, vs_ref = (refs[3], refs[4]) if has_seq else (None, None)
    o_ref = refs[-1]
    hk = pl.program_id(1)
    qi = pl.program_id(2)

    def scores(q, k, mask=None):
        s = lax.dot_general(q, k, (((1,), (1,)), ((), ())), preferred_element_type=F32)
        return s if mask is None else jnp.where(mask, s, NEG_INF)

    def first_pass(r0, nq):
        rows = grp * nq
        q = jnp.concatenate([q_ref[r0:r0 + nq, g * LANE:(g + 1) * LANE] for g in range(grp)], axis=0)
        s_parts, v_parts = [scores(q, kc_ref[...])], [vc_ref[...]]
        if has_seq and window is not None:
            wlen = nq + 2 * window
            q0 = qi * bq + r0
            start = pl.multiple_of(jnp.clip(q0 - window, 0, t_seq - wlen), LANE)
            qpos = q0 + lax.broadcasted_iota(jnp.int32, (rows, wlen), 0) % nq
            kpos = start + lax.broadcasted_iota(jnp.int32, (rows, wlen), 1)
            s_parts.append(scores(q, ks_ref[pl.ds(start, wlen), :], jnp.abs(qpos - kpos) <= window))
            v_parts.append(vs_ref[pl.ds(start, wlen), :])
        elif has_seq:
            s_parts.append(scores(q, ks_ref[0:bk, :]))
            v_parts.append(vs_ref[0:bk, :])
        m = functools.reduce(jnp.maximum, [jnp.max(s, axis=-1, keepdims=True) for s in s_parts])
        if has_sink:
            sink = jnp.concatenate([jnp.full((nq, 1), sink_ref[hk * grp + g], F32) for g in range(grp)], axis=0)
            m = jnp.maximum(m, sink)
        acc = sum(jnp.dot(jnp.exp2(s - m).astype(BF16), v, preferred_element_type=F32)
                  for s, v in zip(s_parts, v_parts))
        if has_sink:
            one_lane = lax.broadcasted_iota(jnp.int32, (1, LANE), 1) == V_ONE
            acc = acc + jnp.where(one_lane, jnp.exp2(sink - m), 0.0)
        return q, m, acc

    def finish(r0, nq, acc):
        o = acc * (1.0 / acc[:, V_ONE:V_ONE + 1])
        for g in range(grp):
            o_ref[r0:r0 + nq, g * LANE:(g + 1) * LANE] = o[g * nq:(g + 1) * nq].astype(o_ref.dtype)

    if has_seq and window is not None:
        for r0 in range(0, bq, sub):
            _, _, acc = first_pass(r0, sub)
            finish(r0, sub, acc)
        return
    q, m, acc = first_pass(0, bq)
    if has_seq:
        def body(j, carry):
            m, acc = carry
            st = pl.multiple_of(j * bk, bk)
            s = scores(q, ks_ref[pl.ds(st, bk), :])
            m_new = jnp.maximum(m, jnp.max(s, axis=-1, keepdims=True))
            p = jnp.exp2(s - m_new).astype(BF16)
            acc = jnp.exp2(m - m_new) * acc + jnp.dot(p, vs_ref[pl.ds(st, bk), :], preferred_element_type=F32)
            return m_new, acc
        m, acc = lax.fori_loop(1, t_seq // bk, body, (m, acc), unroll=True)
    finish(0, bq, acc)


def _attention(q, kc, vc, ks=None, vs=None, *, grp, sink=None, window=None, bq=512, bk=1024):
    b, tq, wq = q.shape
    hkv = wq // (grp * LANE)
    lc = kc.shape[1]
    has_seq = ks is not None
    t_seq = ks.shape[1] if has_seq else 0
    bq = _tile(tq, bq)
    sub = min(bq, LANE)
    if has_seq and window is None:
        bk = _tile(t_seq, bk)
    kern = functools.partial(_attn_kernel, grp=grp, bq=bq, sub=sub, has_sink=sink is not None, has_seq=has_seq,
                             window=window, t_seq=t_seq, bk=bk)
    in_specs, args = [], []
    if sink is not None:
        in_specs.append(pl.BlockSpec(memory_space=pltpu.SMEM))
        args.append(sink)
    in_specs.append(pl.BlockSpec((None, bq, grp * LANE), lambda bb, h, i: (bb, i, h)))
    args.append(q)
    ctx_spec = pl.BlockSpec((None, lc, LANE), lambda bb, h, i: (bb, 0, h))
    in_specs += [ctx_spec, ctx_spec]
    args += [kc, vc]
    if has_seq:
        seq_spec = pl.BlockSpec((None, t_seq, LANE), lambda bb, h, i: (bb, 0, h))
        in_specs += [seq_spec, seq_spec]
        args += [ks, vs]
    return pl.pallas_call(
        kern,
        out_shape=jax.ShapeDtypeStruct((b, tq, wq), BF16),
        grid=(b, hkv, tq // bq),
        in_specs=in_specs,
        out_specs=pl.BlockSpec((None, bq, grp * LANE), lambda bb, h, i: (bb, i, h)),
        compiler_params=_params(("parallel", "parallel", "parallel")),
        name="attention",
    )(*args)


def _split3(x):
    hi = x.astype(BF16)
    r = x - hi.astype(F32)
    mid = r.astype(BF16)
    lo = (r - mid.astype(F32)).astype(BF16)
    return hi, mid, lo


def _hg_chunk(q, z, v, st, gp, fwd):
    c = q.shape[0]
    log_lb, l1p, om = gp
    ls = jnp.minimum(z, 0.0) - jnp.log(1.0 + jnp.exp(-jnp.abs(z)))
    bterm = l1p + ls
    mx = jnp.maximum(log_lb, bterm)
    lf = mx + jnp.log(1.0 + jnp.exp(-jnp.abs(log_lb - bterm)))
    k = om / (1.0 + jnp.exp(z))
    ri = lax.broadcasted_iota(jnp.int32, (c, c), 0)
    ci = lax.broadcasted_iota(jnp.int32, (c, c), 1)
    tri = jnp.where((ci <= ri) if fwd else (ci >= ri), 1.0, 0.0).astype(BF16)
    gsum = sum(jnp.dot(tri, part, preferred_element_type=F32) for part in _split3(lf))
    tot = gsum[c - 1:c, :] if fwd else gsum[0:1, :]
    gm = gsum[c // 2:c // 2 + 1, :]
    qd = (q * jnp.exp(gsum - gm)).astype(BF16)
    kinv = k * jnp.exp(gm - gsum)
    q_in = (q * jnp.exp(gsum)).astype(BF16)
    k_end = (k * jnp.exp(tot - gsum)).astype(BF16)
    lane = lax.broadcasted_iota(jnp.int32, (1, HG_HEADS * HG_DK), 1)
    hmask = [(lane // HG_DK) == h for h in range(HG_HEADS)]
    kstack = jnp.concatenate([jnp.where(hm, kinv, 0.0) for hm in hmask], axis=0).astype(BF16)
    vstack = jnp.concatenate([jnp.where(hm, v, 0.0) for hm in hmask], axis=0).astype(BF16)
    att = lax.dot_general(qd, kstack, (((1,), (1,)), ((), ())), preferred_element_type=F32)
    ti = lax.broadcasted_iota(jnp.int32, (c, HG_HEADS * c), 0)
    si = lax.broadcasted_iota(jnp.int32, (c, HG_HEADS * c), 1) % c
    att = jnp.where((si <= ti) if fwd else (si >= ti), att, 0.0)
    o = jnp.dot(att.astype(BF16), vstack, preferred_element_type=F32)
    o = o + lax.dot_general(q_in, st.astype(BF16), (((1,), (1,)), ((), ())), preferred_element_type=F32)
    kv_t = lax.dot_general(v.astype(BF16), k_end, (((0,), (0,)), ((), ())), preferred_element_type=F32)
    r2 = lax.broadcasted_iota(jnp.int32, (HG_HEADS * HG_DV, HG_HEADS * HG_DK), 0) // HG_DV
    c2 = lax.broadcasted_iota(jnp.int32, (HG_HEADS * HG_DV, HG_HEADS * HG_DK), 1) // HG_DK
    st = st * jnp.exp(tot) + jnp.where(r2 == c2, kv_t, 0.0)
    return o, st


def _hg_kernel(gp_ref, qf_ref, zf_ref, vf_ref, qb_ref, zb_ref, vb_ref, s0f_ref, s0b_ref,
               of_ref, ob_ref, sf_ref, sb_ref, st_ref, *, nchunk):
    i = pl.program_id(1)
    c = HG_CHUNK

    @pl.when(i == 0)
    def _():
        st_ref[0] = s0f_ref[...]
        st_ref[1] = s0b_ref[...]

    gpf = tuple(gp_ref[r:r + 1, :] for r in range(3))
    gpb = tuple(gp_ref[r:r + 1, :] for r in range(3, 6))
    stf = st_ref[0]
    stb = st_ref[1]
    for n in range(nchunk):
        sl = slice(n * c, (n + 1) * c)
        o, stf = _hg_chunk(qf_ref[sl, :], zf_ref[sl, :], vf_ref[sl, :], stf, gpf, True)
        of_ref[sl, :] = o
        nb = nchunk - 1 - n
        sl = slice(nb * c, (nb + 1) * c)
        o, stb = _hg_chunk(qb_ref[sl, :], zb_ref[sl, :], vb_ref[sl, :], stb, gpb, False)
        ob_ref[sl, :] = o
    st_ref[0] = stf
    st_ref[1] = stb

    @pl.when(i == pl.num_programs(1) - 1)
    def _():
        sf_ref[...] = stf
        sb_ref[...] = stb


def _hgrn(q, zf, zb, v, gp, s0f, s0b):
    b, t, w = q.shape
    tb = _tile(t, 256)
    nblk = t // tb
    fw = pl.BlockSpec((None, tb, w), lambda bb, i: (bb, i, 0))
    bw = pl.BlockSpec((None, tb, w), lambda bb, i: (bb, nblk - 1 - i, 0))
    st = pl.BlockSpec((None, w, w), lambda bb, i: (bb, 0, 0))
    return pl.pallas_call(
        functools.partial(_hg_kernel, nchunk=tb // HG_CHUNK),
        out_shape=(jax.ShapeDtypeStruct((b, t, w), F32), jax.ShapeDtypeStruct((b, t, w), F32),
                   jax.ShapeDtypeStruct((b, w, w), F32), jax.ShapeDtypeStruct((b, w, w), F32)),
        grid=(b, nblk),
        in_specs=[_full((8, w)), fw, fw, fw, bw, bw, bw, st, st],
        out_specs=(fw, bw, st, st),
        scratch_shapes=[pltpu.VMEM((2, w, w), F32)],
        compiler_params=_params(("parallel", "arbitrary")),
        name="hgrn2",
    )(gp, q, zf, v, q, zb, v, s0f, s0b)


def _gelu_tanh(x):
    return 0.5 * x * (1.0 + jnp.tanh(math.sqrt(2.0 / math.pi) * (x + 0.044715 * (x * x * x))))


def _outproj_kernel(x_ref, g1_ref, ya_ref, yb_ref, of_ref, ob_ref, hg_ref, yd_ref,
                    wglu_ref, bglu_ref, hn_ref, pavg_ref, wo_ref, o_ref, ys_ref):
    for j in range(S5_CHUNK):
        for v in range(2):
            ys_ref[v, pl.ds(j, ys_ref.shape[1] // S5_CHUNK, stride=S5_CHUNK), :] = ya_ref[j, :, v * LANE:(v + 1) * LANE]
    ya = _gelu_tanh(jnp.concatenate([ys_ref[0], ys_ref[1]], axis=1))
    gl = jnp.dot(ya.astype(BF16), wglu_ref[...], preferred_element_type=F32) + bglu_ref[...]
    ya = ya * (1.0 / (1.0 + jnp.exp(-gl)))
    o = of_ref[...] + ob_ref[...]
    o2 = o * o
    hi = o2.astype(BF16)
    lo = (o2 - hi.astype(F32)).astype(BF16)
    ms = (jnp.dot(hi, pavg_ref[...], preferred_element_type=F32)
          + jnp.dot(lo, pavg_ref[...], preferred_element_type=F32))
    gate = hg_ref[...]
    yc = o * lax.rsqrt(ms + EPS) * hn_ref[...] * (gate * (1.0 / (1.0 + jnp.exp(-gate))))
    y = (jnp.dot(ya.astype(BF16), wo_ref[0:256, :], preferred_element_type=F32)
         + jnp.dot(yb_ref[...], wo_ref[256:768, :], preferred_element_type=F32)
         + jnp.dot(yc.astype(BF16), wo_ref[768:1024, :], preferred_element_type=F32)
         + jnp.dot(yd_ref[...], wo_ref[1024:1536, :], preferred_element_type=F32))
    o_ref[...] = x_ref[...] + g1_ref[...] * y


def _outproj(x, g1, ya, yb, of, ob, hg, yd, wglu, bglu, hn, pavg, wo_p):
    b, t, d = x.shape
    tt = _tile(t, 512)
    xs = lambda n: pl.BlockSpec((None, tt, n), lambda bb, i: (bb, i, 0))
    vec = pl.BlockSpec((None, 1, d), lambda bb, i: (bb, 0, 0))
    return pl.pallas_call(
        _outproj_kernel,
        out_shape=jax.ShapeDtypeStruct((b, t, d), F32),
        grid=(b, t // tt),
        in_specs=[xs(d), vec, pl.BlockSpec((None, S5_CHUNK, tt // S5_CHUNK, 256), lambda bb, i: (bb, 0, i, 0)),
                  xs(512), xs(256), xs(256), xs(256), xs(512),
                  _full(wglu.shape), _full((1, 256)), _full((1, 256)), _full(pavg.shape), _full(wo_p.shape)],
        out_specs=xs(d),
        scratch_shapes=[pltpu.VMEM((2, tt, LANE), F32)],
        compiler_params=_params(("parallel", "parallel")),
        name="outproj",
    )(x, g1, ya, yb, of, ob, hg, yd, wglu, bglu, hn, pavg, wo_p)


def _ffn_kernel(x_ref, sc_ref, sh_ref, g2_ref, ng_ref, wup_ref, wdn_ref, fg_ref, o_ref, acc_ref, *, hidden, ck, final):
    x = x_ref[...]
    hb = (_rms(x, ng_ref[...]) * (1.0 + sc_ref[...]) + sh_ref[...]).astype(BF16)
    for j in range(hidden // ck):
        gate = jnp.dot(hb, wup_ref[:, j * ck:(j + 1) * ck], preferred_element_type=F32)
        up = jnp.dot(hb, wup_ref[:, hidden + j * ck:hidden + (j + 1) * ck], preferred_element_type=F32)
        a = (gate * (1.0 / (1.0 + jnp.exp(-gate))) * up).astype(BF16)
        part = jnp.dot(a, wdn_ref[j * ck:(j + 1) * ck, :], preferred_element_type=F32)
        if j == 0:
            acc_ref[...] = part
        else:
            acc_ref[...] += part
    y = x + g2_ref[...] * acc_ref[...]
    if final:
        y = _rms(y, fg_ref[...])
    o_ref[...] = y


def _ffn(x, sc, sh, g2, ng, wup, wdn, fg, final):
    b, t, d = x.shape
    hidden = wdn.shape[0]
    tt = _tile(t, 512)
    xs = pl.BlockSpec((None, tt, d), lambda bb, i: (bb, i, 0))
    vec = pl.BlockSpec((None, 1, d), lambda bb, i: (bb, 0, 0))
    return pl.pallas_call(
        functools.partial(_ffn_kernel, hidden=hidden, ck=_tile(hidden, 256), final=final),
        out_shape=jax.ShapeDtypeStruct((b, t, d), F32),
        grid=(b, t // tt),
        in_specs=[xs, vec, vec, vec, _full((1, d)), _full(wup.shape), _full(wdn.shape), _full((1, d))],
        out_specs=xs,
        scratch_shapes=[pltpu.VMEM((tt, d), F32)],
        compiler_params=_params(("parallel", "parallel")),
        name="ffn",
    )(x, sc, sh, g2, ng, wup, wdn, fg)


def _pad_heads(w, heads, dim):
    w = w.reshape(w.shape[:-1] + (heads, dim))
    w = jnp.pad(w, [(0, 0)] * (w.ndim - 1) + [(0, LANE - dim)])
    return w.reshape(w.shape[:-2] + (heads * LANE,))


def _layer_weights(w_in, w_out, mla_w_qb, mla_w_kvb):
    d = w_in.shape[0]
    n_in = w_in.shape[1]
    qscale = jnp.ones((n_in,), F32).at[_C_SQ:_C_SQ + SWA_HEADS * SWA_HEAD_DIM].set(SWA_HEAD_DIM ** -0.5 * LOG2E)
    w_in_p = jnp.pad(w_in * qscale, ((0, 0), (0, _N_INP - n_in))).astype(BF16)
    assert w_in_p.shape == (d, _N_INP)
    wqb_p = _pad_heads(mla_w_qb * (MLA_SCALE * LOG2E), MLA_HEADS, MLA_NOPE + MLA_ROPE).astype(BF16)
    kvb = mla_w_kvb.reshape(MLA_KV_RANK, MLA_HEADS, MLA_NOPE + MLA_V)
    wk = _pad_heads(kvb[..., :MLA_NOPE].reshape(MLA_KV_RANK, -1), MLA_HEADS, MLA_NOPE)
    wv = _pad_heads(kvb[..., MLA_NOPE:].reshape(MLA_KV_RANK, -1), MLA_HEADS, MLA_V)
    wkv_p = jnp.concatenate([wk, wv], axis=1).astype(BF16)
    dmix = w_out.shape[0]
    rows = lambda lo, n: w_out[lo:lo + n]
    pad_rows = lambda w, heads, dim: _pad_heads(w.T, heads, dim).T
    wo_p = jnp.concatenate([rows(0, 256), pad_rows(rows(256, 256), SWA_HEADS, SWA_HEAD_DIM),
                            rows(512, 256), pad_rows(rows(768, 256), MLA_HEADS, MLA_V)], axis=0).astype(BF16)
    assert dmix == 1024
    return w_in_p, wqb_p, wkv_p, wo_p


def _rope_tables(length, dim, lo, ident, repeat=1):
    n_freq = dim // 4
    rows = length // GRID_W
    row = jnp.repeat(jnp.arange(rows, dtype=F32), GRID_W)
    col = jnp.tile(jnp.arange(GRID_W, dtype=F32), rows)
    inv = ROPE_BASE ** (-jnp.arange(n_freq, dtype=F32) / n_freq)
    ang = jnp.stack([row[:, None] * inv, col[:, None] * inv], axis=1)
    cos, sin = jnp.cos(ang), jnp.sin(ang)
    z = jnp.zeros_like(sin)
    cos_l = jnp.stack([cos, cos], axis=2).reshape(length, dim)
    sina = jnp.stack([-sin, z], axis=2).reshape(length, dim)
    sinb = jnp.stack([z, sin], axis=2).reshape(length, dim)
    if ident:
        cos_l, sina, sinb = jnp.ones_like(cos_l), jnp.zeros_like(sina), jnp.zeros_like(sinb)
    cos_l, sina, sinb = (jnp.tile(a, (1, repeat)) for a in (cos_l, sina, sinb))
    pad = lambda a, fill: jnp.pad(a, ((0, 0), (lo, LANE - lo - dim * repeat)), constant_values=fill)
    return jnp.stack([pad(cos_l, 1.0), pad(sina, 0.0), pad(sinb, 0.0)], axis=0)


def kernel(x, c, ctx, c_ctx, w_mod, b_mod, norm1_g, norm2_g, w_in, w_out, s5_lam_re, s5_lam_im, s5_log_dt,
           s5_b_re, s5_b_im, s5_c_re, s5_c_im, s5_d, s5_w_glu, s5_b_glu, swa_sink, hg_lb, hg_norm_g,
           mla_q_norm_g, mla_w_qb, mla_kv_norm_g, mla_w_kvb, ffn_w_up, ffn_w_down, final_norm_g):
    b, t, d = x.shape
    lc = ctx.shape[1]
    depth = w_mod.shape[0]

    cc = jnp.zeros((8, d), F32).at[:b].set(c).at[b].set(c_ctx)
    mods = _modulation(cc, w_mod, b_mod)

    lb_cum = jnp.cumsum(jax.nn.softmax(hg_lb.astype(F32), axis=1), axis=1)
    lb = lb_cum - lb_cum[:, :1]

    tab_swa = _rope_tables(t, SWA_HEAD_DIM, 0, False, repeat=2)
    tab_mla = _rope_tables(t, MLA_ROPE, MLA_NOPE, False)
    tab_swa_c = _rope_tables(lc, SWA_HEAD_DIM, 0, True, repeat=2)
    tab_mla_c = _rope_tables(lc, MLA_ROPE, MLA_NOPE, True)
    pavg = jnp.kron(jnp.eye(HG_HEADS, dtype=F32), jnp.full((HG_DV, HG_DV), 1.0 / HG_DV, F32)).astype(BF16)
    zero_st = jnp.zeros((b, HG_HEADS * HG_DV, HG_HEADS * HG_DK), F32)
    zero_s5 = jnp.zeros((b, S5_GROUPS, 4 * S5_STATE), F32)

    for i in range(depth):
        need_ctx = i < depth - 1
        mod = mods[i, :b].reshape(b, 6, 1, d)
        mod_c = jnp.broadcast_to(mods[i, b].reshape(1, 6, 1, d), (b, 6, 1, d))
        sh1, sc1, g1, sh2, sc2, g2 = (mod[:, j] for j in range(6))
        csh1, csc1, cg1, csh2, csc2, cg2 = (mod_c[:, j] for j in range(6))
        w_in_p, wqb_p, wkv_p, wo_p = _layer_weights(w_in[i], w_out[i], mla_w_qb[i], mla_w_kvb[i])
        n1 = norm1_g[i].reshape(1, d)
        qg = mla_q_norm_g[i].reshape(1, -1)
        kvg = mla_kv_norm_g[i].reshape(1, -1)
        px = _inproj(x, sc1, sh1, n1, w_in_p, tab_swa, tab_mla, qg, wqb_p, kvg, wkv_p)
        pc = _inproj(ctx, csc1, csh1, n1, w_in_p, tab_swa_c, tab_mla_c, qg, wqb_p, kvg, wkv_p)
        (xu, xsq, xsk, xsv, xhq, xhzf, xhzb, xhi, xhg, xmq, xmk, xmv) = px
        (cu, csq, csk, csv, chq, chzf, chzb, chi, chg, cmq, cmk, cmv) = pc

        mats = _s5_matrices(s5_lam_re[i], s5_lam_im[i], s5_log_dt[i], s5_b_re[i], s5_b_im[i],
                            s5_c_re[i], s5_c_im[i], s5_d[i])
        ya_c, s5_fin = _s5_scan(cu, mats, zero_s5)
        ya, _ = _s5_scan(xu, mats, s5_fin)

        grp = SWA_HEADS // SWA_KV_HEADS
        sink = swa_sink[i].astype(F32) * LOG2E
        yb = _attention(xsq, csk, csv, xsk, xsv, grp=grp, sink=sink, window=SWA_WINDOW)
        lbf, lbb = lb[0, i], lb[1, i]
        gp = jnp.stack([jnp.log(lbf), jnp.log1p(-lbf), 1.0 - lbf,
                        jnp.log(lbb), jnp.log1p(-lbb), 1.0 - lbb, lbf, lbb], axis=0)
        of_c, ob_c, stf, stb = _hgrn(chq, chzf, chzb, chi, gp, zero_st, zero_st)
        of, ob, _, _ = _hgrn(xhq, xhzf, xhzb, xhi, gp, stf, stb)
        yd = _attention(xmq, cmk, cmv, xmk, xmv, grp=1, bq=1024)

        wglu = s5_w_glu[i].astype(BF16)
        bglu = s5_b_glu[i].reshape(1, -1)
        hn = jnp.tile(hg_norm_g[i], HG_HEADS).reshape(1, -1)
        x = _outproj(x, g1, ya, yb, of, ob, xhg, yd, wglu, bglu, hn, pavg, wo_p)
        wup = ffn_w_up[i].astype(BF16)
        wdn = ffn_w_down[i].astype(BF16)
        n2 = norm2_g[i].reshape(1, d)
        fg = final_norm_g.reshape(1, d)
        x = _ffn(x, sc2, sh2, g2, n2, wup, wdn, fg, final=not need_ctx)
        if need_ctx:
            yb_c = _attention(csq, csk, csv, grp=grp, sink=sink)
            yd_c = _attention(cmq, cmk, cmv, grp=1)
            ctx = _outproj(ctx, cg1, ya_c, yb_c, of_c, ob_c, chg, yd_c, wglu, bglu, hn, pavg, wo_p)
            ctx = _ffn(ctx, csc2, csh2, cg2, n2, wup, wdn, fg, final=False)
    return x
```

```python
import functools
import math

import jax
import jax.numpy as jnp
from jax import lax
from jax.experimental import pallas as pl
from jax.experimental.pallas import tpu as pltpu

F32 = jnp.float32
BF16 = jnp.bfloat16

EPS = 1e-6
NEG_INF = -1e30
ROPE_BASE = 10000.0
GRID_W = 64
LANE = 128
VMEM_LIMIT = 56 * 1024 * 1024

S5_CH, S5_GROUP, S5_STATE = 256, 16, 64
S5_GROUPS = S5_CH // S5_GROUP
S5_CHUNK = 16
SWA_HEADS, SWA_KV_HEADS, SWA_HEAD_DIM, SWA_WINDOW = 4, 2, 64, 128
HG_HEADS, HG_DK, HG_DV = 4, 64, 64
HG_CHUNK = 64
MLA_HEADS, MLA_Q_RANK, MLA_KV_RANK = 4, 256, 128
MLA_NOPE, MLA_ROPE, MLA_V = 64, 32, 64
MLA_SCALE = (MLA_NOPE + MLA_ROPE) ** -0.5
LOG2E = 1.4426950408889634
V_ONE = 64


def _tile(n, pref):
    t = min(n, pref)
    assert n % t == 0, (n, pref)
    return t


def _params(sem):
    return pltpu.CompilerParams(dimension_semantics=sem, vmem_limit_bytes=VMEM_LIMIT)


def _full(shape):
    nd = len(shape)
    return pl.BlockSpec(shape, lambda *_: (0,) * nd, pipeline_mode=pl.Buffered(1))


def _layer(w, li):
    nd = w.ndim - 1
    return pl.BlockSpec((None,) + w.shape[1:], lambda *_: (li,) + (0,) * nd, pipeline_mode=pl.Buffered(1))


def _mod_kernel(c_ref, w_ref, b_ref, o_ref):
    c = c_ref[...]
    s = c * (1.0 / (1.0 + jnp.exp(-c)))
    o_ref[...] = jnp.dot(s.astype(BF16), w_ref[...].astype(BF16), preferred_element_type=F32) + b_ref[...]


def _modulation(cc, w_mod, b_mod):
    depth, d, n = w_mod.shape
    tn = _tile(n, 1536)
    return pl.pallas_call(
        _mod_kernel,
        out_shape=jax.ShapeDtypeStruct((depth, 8, n), F32),
        grid=(depth, n // tn),
        in_specs=[pl.BlockSpec((8, d), lambda l, j: (0, 0)),
                  pl.BlockSpec((None, d, tn), lambda l, j: (l, 0, j)),
                  pl.BlockSpec((None, 1, tn), lambda l, j: (l, 0, j))],
        out_specs=pl.BlockSpec((None, 8, tn), lambda l, j: (l, 0, j)),
        compiler_params=_params(("parallel", "parallel")),
        name="modulation",
    )(cc, w_mod, b_mod.reshape(depth, 1, n))


def _rope_block(x, t_ref, half):
    return (x * t_ref[0] + pltpu.roll(x, LANE - half, 1) * t_ref[1]
            + pltpu.roll(x, half, 1) * t_ref[2])


def _rms(x, g):
    return x * lax.rsqrt(jnp.mean(x * x, axis=-1, keepdims=True) + EPS) * g


_C_U = 0
_C_SQ = 256
_C_SKV = 512
_C_HG = 768
_C_CQ = _C_HG + 5 * 256
_C_CKV = _C_CQ + 256
_N_INP = _C_CKV + 256


def _inproj_kernel(x_ref, sc_ref, sh_ref, g_ref, w_ref, ts_ref, tm_ref, qg_ref, wqb_ref, kvg_ref, wkv_ref,
                   u_ref, sq_ref, sk_ref, sv_ref, hq_ref, hzf_ref, hzb_ref, hi_ref, hg_ref,
                   mq_ref, mk_ref, mv_ref, us_ref):
    h = _rms(x_ref[...], g_ref[...]) * (1.0 + sc_ref[...]) + sh_ref[...]
    hb = h.astype(BF16)

    def proj(lo, n):
        return jnp.dot(hb, w_ref[:, lo:lo + n], preferred_element_type=F32)

    cq = _rms(proj(_C_CQ, 256), qg_ref[...]).astype(BF16)
    ckv_kr = proj(_C_CKV, 256)
    ckv = _rms(ckv_kr[:, :LANE], kvg_ref[...]).astype(BF16)
    kr = pltpu.roll(ckv_kr[:, LANE:], MLA_NOPE, 1)

    u = proj(_C_U, 256)
    for v in range(2):
        us_ref[v] = u[:, v * LANE:(v + 1) * LANE]
    for j in range(S5_CHUNK):
        for v in range(2):
            u_ref[j, :, v * LANE:(v + 1) * LANE] = us_ref[v, pl.ds(j, us_ref.shape[1] // S5_CHUNK, stride=S5_CHUNK), :]
    lane = lax.broadcasted_iota(jnp.int32, (1, LANE), 1)
    low = lane < SWA_HEAD_DIM
    one = jnp.where(lane == V_ONE, 1.0, 0.0)

    def spread(pair, fill):
        return [jnp.where(low, blk, fill).astype(BF16) for blk in (pair, pltpu.roll(pair, SWA_HEAD_DIM, 1))]

    sq = proj(_C_SQ, 256)
    for pp in range(SWA_HEADS // 2):
        heads = spread(_rope_block(sq[:, pp * LANE:(pp + 1) * LANE], ts_ref, 16), 0.0)
        for e in range(2):
            sq_ref[:, (2 * pp + e) * LANE:(2 * pp + e + 1) * LANE] = heads[e]
    skv = proj(_C_SKV, 256)
    for e, blk in enumerate(spread(_rope_block(skv[:, :LANE], ts_ref, 16), 0.0)):
        sk_ref[:, e * LANE:(e + 1) * LANE] = blk
    for e, blk in enumerate(spread(skv[:, LANE:], one)):
        sv_ref[:, e * LANE:(e + 1) * LANE] = blk

    q = jnp.dot(cq, wqb_ref[...], preferred_element_type=F32)
    for hh in range(MLA_HEADS):
        sl = slice(hh * LANE, (hh + 1) * LANE)
        mq_ref[:, sl] = _rope_block(q[:, sl], tm_ref, 8).astype(BF16)
    for i, r in enumerate((hq_ref, hzf_ref, hzb_ref)):
        r[...] = proj(_C_HG + i * 256, 256)
    kv = jnp.dot(ckv, wkv_ref[...], preferred_element_type=F32)
    for hh in range(MLA_HEADS):
        sl = slice(hh * LANE, (hh + 1) * LANE)
        mk_ref[:, sl] = _rope_block(kv[:, sl] + kr, tm_ref, 8).astype(BF16)
        mv_ref[:, sl] = jnp.where(lane == V_ONE, 1.0, kv[:, MLA_HEADS * LANE + hh * LANE:][:, :LANE]).astype(BF16)
    for i, r in enumerate((hi_ref, hg_ref)):
        r[...] = proj(_C_HG + (3 + i) * 256, 256)


def _inproj(x, sc, sh, g, w_in_p, tab_swa, tab_mla, qg, wqb_p, kvg, wkv_p, li):
    b, t, d = x.shape
    tt = _tile(t, 512)
    row = lambda n, dt: jax.ShapeDtypeStruct((b, t, n), dt)
    out_shape = (jax.ShapeDtypeStruct((b, S5_CHUNK, t // S5_CHUNK, 256), F32),
                 row(512, BF16), row(256, BF16), row(256, BF16),
                 row(256, F32), row(256, F32), row(256, F32), row(256, F32), row(256, F32),
                 row(512, BF16), row(512, BF16), row(512, BF16))
    xs = lambda n: pl.BlockSpec((None, tt, n), lambda i, bb: (bb, i, 0))
    vec = pl.BlockSpec((None, 1, d), lambda i, bb: (bb, 0, 0))
    tab = pl.BlockSpec((3, tt, LANE), lambda i, bb: (0, i, 0))
    return pl.pallas_call(
        _inproj_kernel,
        out_shape=out_shape,
        grid=(t // tt, b),
        in_specs=[xs(d), vec, vec, _full((1, d)), _layer(w_in_p, li), tab, tab,
                  _full((1, 256)), _layer(wqb_p, li), _full((1, 128)), _layer(wkv_p, li)],
        out_specs=(pl.BlockSpec((None, S5_CHUNK, tt // S5_CHUNK, 256), lambda i, bb: (bb, 0, i, 0)),)
        + tuple(xs(s.shape[-1]) for s in out_shape[1:]),
        scratch_shapes=[pltpu.VMEM((2, tt, LANE), F32)],
        compiler_params=_params(("parallel", "parallel")),
        name="inproj",
    )(x, sc, sh, g, w_in_p, tab_swa, tab_mla, qg, wqb_p, kvg, wkv_p)


def _gather_lane_blocks(pieces):
    blk = lax.broadcasted_iota(jnp.int32, (1, LANE), 1) // S5_GROUP
    out = None
    for k, (arr, src) in enumerate(pieces):
        shift = (S5_GROUP * (k - src)) % LANE
        moved = pltpu.roll(arr, shift, 1) if shift else arr
        out = moved if out is None else jnp.where(blk == k, moved, out)
    return out


def _s5_kernel(u_ref, m_ref, f_ref, e_ref, a_ref, s0_ref, y_ref, sfin_ref, zs_ref, ss_ref, ug_ref, *, nc):
    ng, half = S5_GROUPS, LANE // S5_GROUP
    for g in range(ng):
        v, src = g // half, g % half
        ug = jnp.concatenate(
            [_gather_lane_blocks([(u_ref[half * hv + k, :, v * LANE:(v + 1) * LANE], src) for k in range(half)])
             for hv in range(2)], axis=1).astype(BF16)
        ug_ref[g] = ug
        z = jnp.dot(ug, f_ref[g], preferred_element_type=F32)
        for c in range(2):
            zs_ref[c, pl.ds(g, nc, stride=ng), :] = z[:, c * LANE:(c + 1) * LANE]

    fmask = lax.broadcasted_iota(jnp.int32, (1, LANE), 1) < S5_STATE
    ar, ai = a_ref[0], a_ref[1]

    def step(i, s):
        rf = pl.multiple_of(i * ng, ng)
        rb = pl.multiple_of((nc - 1 - i) * ng, ng)
        for c in range(2):
            ss_ref[c, pl.ds(rf, ng), 0:S5_STATE] = s[c][:, 0:S5_STATE]
            ss_ref[c, pl.ds(rb, ng), S5_STATE:LANE] = s[c][:, S5_STATE:LANE]
        zre, zim = (jnp.where(fmask, zs_ref[c, pl.ds(rf, ng), :], zs_ref[c, pl.ds(rb, ng), :]) for c in range(2))
        re, im = s
        return ar * re - ai * im + zre, ar * im + ai * re + zim

    re, im = lax.fori_loop(0, nc, step, (s0_ref[:, :LANE], s0_ref[:, LANE:]), unroll=4)
    sfin_ref[:, :LANE] = re
    sfin_ref[:, LANE:] = im

    for g in range(ng):
        ss = jnp.concatenate([ss_ref[c, pl.ds(g, nc, stride=ng), :] for c in range(2)], axis=1).astype(BF16)
        y = (jnp.dot(ug_ref[g], m_ref[g], preferred_element_type=F32)
             + jnp.dot(ss, e_ref[g], preferred_element_type=F32))
        for c in range(2):
            zs_ref[c, g * nc:(g + 1) * nc, :] = y[:, c * LANE:(c + 1) * LANE]
    for t in range(S5_CHUNK):
        v, src = t // half, t % half
        y_ref[t] = jnp.concatenate(
            [_gather_lane_blocks([(zs_ref[v, (half * hv + k) * nc:(half * hv + k + 1) * nc, :], src)
                                  for k in range(half)]) for hv in range(2)], axis=1)


def _s5_scan(uj, mats, s0, li):
    m, f, e, a16 = mats
    b, _, nc, _ = uj.shape
    slab = pl.BlockSpec((None, S5_CHUNK, nc, 256), lambda bb: (bb, 0, 0, 0), pipeline_mode=pl.Buffered(1))
    st = pl.BlockSpec((None, S5_GROUPS, 256), lambda bb: (bb, 0, 0))
    return pl.pallas_call(
        functools.partial(_s5_kernel, nc=nc),
        out_shape=(jax.ShapeDtypeStruct(uj.shape, F32), jax.ShapeDtypeStruct((b, S5_GROUPS, 256), F32)),
        grid=(b,),
        in_specs=[slab, _layer(m, li), _layer(f, li), _layer(e, li), _layer(a16, li), st],
        out_specs=(slab, st),
        scratch_shapes=[pltpu.VMEM((2, S5_GROUPS * nc, LANE), F32), pltpu.VMEM((2, S5_GROUPS * nc, LANE), F32),
                        pltpu.VMEM((S5_GROUPS, nc, 256), BF16)],
        compiler_params=_params(("parallel",)),
        name="s5",
    )(uj, m, f, e, a16, s0)


def _s5_matrices(lam_re, lam_im, log_dt, b_re, b_im, c_re, c_im, d_skip):
    L = S5_CHUNK
    depth = lam_re.shape[0]
    hp = lax.Precision.HIGHEST
    lam = lax.complex(lam_re, lam_im)
    ldt = lam * jnp.exp(log_dt)
    bb = ((jnp.exp(ldt) - 1.0) / lam)[..., None] * lax.complex(b_re, b_im)
    cc = lax.complex(c_re, c_im)
    k = jnp.arange(L + 1, dtype=F32)
    apow = jnp.exp(ldt[:, None] * k[None, :, None, None, None])
    kern = jnp.real(jnp.einsum('ndghp,nkdgp,ndgpi->ndkgih', cc, apow[:, :L], bb, precision=hp))
    idx = jnp.arange(L)
    lag = idx[None, :] - idx[:, None]
    oh_f = (lag[:, :, None] == idx[None, None, :]).astype(F32)
    oh_b = (-lag[:, :, None] == idx[None, None, :]).astype(F32)
    eye = jnp.eye(L, dtype=F32)
    dg = d_skip.reshape(depth, S5_GROUPS, S5_GROUP)
    m = (jnp.einsum('jtk,nkgih->ngjith', oh_f, kern[:, 0], precision=hp)
         + jnp.einsum('jtk,nkgih->ngjith', oh_b, kern[:, 1], precision=hp)
         + jnp.einsum('jt,ih,ngh->ngjith', eye, jnp.eye(S5_GROUP, dtype=F32), dg, precision=hp))
    m = m.reshape(depth, S5_GROUPS, L * S5_GROUP, L * S5_GROUP)
    zf = apow[:, L - 1 - idx, 0][..., None] * bb[:, 0][:, None]
    zb = apow[:, idx, 1][..., None] * bb[:, 1][:, None]
    fmat = jnp.concatenate([jnp.real(zf), jnp.real(zb), jnp.imag(zf), jnp.imag(zb)], axis=3)
    fmat = fmat.transpose(0, 2, 1, 4, 3).reshape(depth, S5_GROUPS, L * S5_GROUP, 4 * S5_STATE)
    wf = cc[:, 0][:, None] * apow[:, 1 + idx, 0][:, :, :, None, :]
    wb = cc[:, 1][:, None] * apow[:, L - idx, 1][:, :, :, None, :]
    emat = jnp.concatenate([jnp.real(wf), jnp.real(wb), -jnp.imag(wf), -jnp.imag(wb)], axis=4)
    emat = emat.transpose(0, 2, 4, 1, 3).reshape(depth, S5_GROUPS, 4 * S5_STATE, L * S5_GROUP)
    al = apow[:, L]
    a16 = jnp.stack([jnp.concatenate([jnp.real(al[:, 0]), jnp.real(al[:, 1])], axis=-1),
                     jnp.concatenate([jnp.imag(al[:, 0]), jnp.imag(al[:, 1])], axis=-1)], axis=1)
    return m.astype(BF16), fmat.astype(BF16), emat.astype(BF16), a16.astype(F32)


def _attn_kernel(*refs, grp, bq, sub, has_sink, has_seq, window, t_seq, bk):
    refs = list(refs)
    sink_ref = refs.pop(0) if has_sink else None
    q_ref, kc_ref, vc_ref = refs[:3]
    ks_ref, vs_ref = (refs[3], refs[4]) if has_seq else (None, None)
    o_ref = refs[-1]
    hk = pl.program_id(1)
    qi = pl.program_id(2)

    def scores(q, k, mask=None):
        s = lax.dot_general(q, k, (((1,), (1,)), ((), ())), preferred_element_type=F32)
        return s if mask is None else jnp.where(mask, s, NEG_INF)

    def first_pass(r0, nq):
        rows = grp * nq
        q = jnp.concatenate([q_ref[r0:r0 + nq, g * LANE:(g + 1) * LANE] for g in range(grp)], axis=0)
        s_parts, v_parts = [scores(q, kc_ref[...])], [vc_ref[...]]
        if has_seq and window is not None:
            wlen = nq + 2 * window
            q0 = qi * bq + r0
            start = pl.multiple_of(jnp.clip(q0 - window, 0, t_seq - wlen), LANE)
            qpos = q0 + lax.broadcasted_iota(jnp.int32, (rows, wlen), 0) % nq
            kpos = start + lax.broadcasted_iota(jnp.int32, (rows, wlen), 1)
            s_parts.append(scores(q, ks_ref[pl.ds(start, wlen), :], jnp.abs(qpos - kpos) <= window))
            v_parts.append(vs_ref[pl.ds(start, wlen), :])
        elif has_seq:
            s_parts.append(scores(q, ks_ref[0:bk, :]))
            v_parts.append(vs_ref[0:bk, :])
        m = functools.reduce(jnp.maximum, [jnp.max(s, axis=-1, keepdims=True) for s in s_parts])
        if has_sink:
            sink = jnp.concatenate([jnp.full((nq, 1), sink_ref[hk * grp + g], F32) for g in range(grp)], axis=0)
            m = jnp.maximum(m, sink)
        acc = sum(jnp.dot(jnp.exp2(s - m).astype(BF16), v, preferred_element_type=F32)
                  for s, v in zip(s_parts, v_parts))
        if has_sink:
            one_lane = lax.broadcasted_iota(jnp.int32, (1, LANE), 1) == V_ONE
            acc = acc + jnp.where(one_lane, jnp.exp2(sink - m), 0.0)
        return q, m, acc

    def finish(r0, nq, acc):
        o = acc * (1.0 / acc[:, V_ONE:V_ONE + 1])
        for g in range(grp):
            o_ref[r0:r0 + nq, g * LANE:(g + 1) * LANE] = o[g * nq:(g + 1) * nq].astype(o_ref.dtype)

    if has_seq and window is not None:
        for r0 in range(0, bq, sub):
            _, _, acc = first_pass(r0, sub)
            finish(r0, sub, acc)
        return
    q, m, acc = first_pass(0, bq)
    if has_seq:
        def body(j, carry):
            m, acc = carry
            st = pl.multiple_of(j * bk, bk)
            s = scores(q, ks_ref[pl.ds(st, bk), :])
            m_new = jnp.maximum(m, jnp.max(s, axis=-1, keepdims=True))
            p = jnp.exp2(s - m_new).astype(BF16)
            acc = jnp.exp2(m - m_new) * acc + jnp.dot(p, vs_ref[pl.ds(st, bk), :], preferred_element_type=F32)
            return m_new, acc
        m, acc = lax.fori_loop(1, t_seq // bk, body, (m, acc), unroll=True)
    finish(0, bq, acc)


def _attention(q, kc, vc, ks=None, vs=None, *, grp, sink=None, window=None, bq=512, bk=1024):
    b, tq, wq = q.shape
    hkv = wq // (grp * LANE)
    lc = kc.shape[1]
    has_seq = ks is not None
    t_seq = ks.shape[1] if has_seq else 0
    bq = _tile(tq, bq)
    sub = min(bq, LANE)
    if has_seq and window is None:
        bk = _tile(t_seq, bk)
    kern = functools.partial(_attn_kernel, grp=grp, bq=bq, sub=sub, has_sink=sink is not None, has_seq=has_seq,
                             window=window, t_seq=t_seq, bk=bk)
    in_specs, args = [], []
    if sink is not None:
        in_specs.append(pl.BlockSpec(memory_space=pltpu.SMEM))
        args.append(sink)
    in_specs.append(pl.BlockSpec((None, bq, grp * LANE), lambda bb, h, i: (bb, i, h)))
    args.append(q)
    ctx_spec = pl.BlockSpec((None, lc, LANE), lambda bb, h, i: (bb, 0, h))
    in_specs += [ctx_spec, ctx_spec]
    args += [kc, vc]
    if has_seq:
        seq_spec = pl.BlockSpec((None, t_seq, LANE), lambda bb, h, i: (bb, 0, h))
        in_specs += [seq_spec, seq_spec]
        args += [ks, vs]
    return pl.pallas_call(
        kern,
        out_shape=jax.ShapeDtypeStruct((b, tq, wq), BF16),
        grid=(b, hkv, tq // bq),
        in_specs=in_specs,
        out_specs=pl.BlockSpec((None, bq, grp * LANE), lambda bb, h, i: (bb, i, h)),
        compiler_params=_params(("parallel", "parallel", "parallel")),
        name="attention",
    )(*args)


def _split3(x):
    hi = x.astype(BF16)
    r = x - hi.astype(F32)
    mid = r.astype(BF16)
    lo = (r - mid.astype(F32)).astype(BF16)
    return hi, mid, lo


def _hg_kernel(gp_ref, qf_ref, zf_ref, vf_ref, qb_ref, zb_ref, vb_ref, s0f_ref, s0b_ref,
               of_ref, ob_ref, sf_ref, sb_ref, st_ref, *, nchunk):
    i = pl.program_id(1)
    c = HG_CHUNK
    w = HG_HEADS * HG_DK

    @pl.when(i == 0)
    def _():
        st_ref[0] = s0f_ref[...]
        st_ref[1] = s0b_ref[...]

    ri = lax.broadcasted_iota(jnp.int32, (c, c), 0)
    ci = lax.broadcasted_iota(jnp.int32, (c, c), 1)
    ti = lax.broadcasted_iota(jnp.int32, (c, HG_HEADS * c), 0)
    si = lax.broadcasted_iota(jnp.int32, (c, HG_HEADS * c), 1) % c
    lane = lax.broadcasted_iota(jnp.int32, (1, w), 1)
    hmask = [(lane // HG_DK) == h for h in range(HG_HEADS)]
    r2 = lax.broadcasted_iota(jnp.int32, (w, w), 0) // HG_DV
    c2 = lax.broadcasted_iota(jnp.int32, (w, w), 1) // HG_DK
    diag = r2 == c2
    dirs = {
        True: dict(q=qf_ref, z=zf_ref, v=vf_ref, o=of_ref, gp=tuple(gp_ref[r:r + 1, :] for r in range(3)),
                   tri=jnp.where(ci <= ri, 1.0, 0.0).astype(BF16), causal=si <= ti),
        False: dict(q=qb_ref, z=zb_ref, v=vb_ref, o=ob_ref, gp=tuple(gp_ref[r:r + 1, :] for r in range(3, 6)),
                    tri=jnp.where(ci >= ri, 1.0, 0.0).astype(BF16), causal=si >= ti),
    }
    units = []
    for n in range(nchunk):
        units += [(True, n), (False, nchunk - 1 - n)]

    ph1 = []
    for fwd, n in units:
        dd = dirs[fwd]
        sl = slice(n * c, (n + 1) * c)
        z = dd['z'][sl, :]
        log_lb, l1p, om = dd['gp']
        ls = jnp.minimum(z, 0.0) - jnp.log(1.0 + jnp.exp(-jnp.abs(z)))
        bterm = l1p + ls
        lf = jnp.maximum(log_lb, bterm) + jnp.log(1.0 + jnp.exp(-jnp.abs(log_lb - bterm)))
        k = om / (1.0 + jnp.exp(z))
        gsum = sum(jnp.dot(dd['tri'], part, preferred_element_type=F32) for part in _split3(lf))
        ph1.append((k, gsum))

    ph2 = []
    for (fwd, n), (k, gsum) in zip(units, ph1):
        dd = dirs[fwd]
        sl = slice(n * c, (n + 1) * c)
        q, v = dd['q'][sl, :], dd['v'][sl, :]
        tot = gsum[c - 1:c, :] if fwd else gsum[0:1, :]
        gm = gsum[c // 2:c // 2 + 1, :]
        qd = (q * jnp.exp(gsum - gm)).astype(BF16)
        kinv = k * jnp.exp(gm - gsum)
        q_in = (q * jnp.exp(gsum)).astype(BF16)
        k_end = (k * jnp.exp(tot - gsum)).astype(BF16)
        kstack = jnp.concatenate([jnp.where(hm, kinv, 0.0) for hm in hmask], axis=0).astype(BF16)
        vstack = jnp.concatenate([jnp.where(hm, v, 0.0) for hm in hmask], axis=0).astype(BF16)
        ph2.append((qd, kstack, vstack, q_in, k_end, v.astype(BF16), jnp.exp(tot)))

    ph3 = []
    for (fwd, n), (qd, kstack, vstack, q_in, k_end, vb, dec) in zip(units, ph2):
        att = lax.dot_general(qd, kstack, (((1,), (1,)), ((), ())), preferred_element_type=F32)
        att = jnp.where(dirs[fwd]['causal'], att, 0.0).astype(BF16)
        o_intra = jnp.dot(att, vstack, preferred_element_type=F32)
        kv_t = lax.dot_general(vb, k_end, (((0,), (0,)), ((), ())), preferred_element_type=F32)
        ph3.append((o_intra, jnp.where(diag, kv_t, 0.0)))

    st = {True: st_ref[0], False: st_ref[1]}
    for (fwd, n), (_, _, _, q_in, _, _, dec), (o_intra, kv_t) in zip(units, ph2, ph3):
        sl = slice(n * c, (n + 1) * c)
        o_inter = lax.dot_general(q_in, st[fwd].astype(BF16), (((1,), (1,)), ((), ())), preferred_element_type=F32)
        dirs[fwd]['o'][sl, :] = o_intra + o_inter
        st[fwd] = st[fwd] * dec + kv_t
    st_ref[0] = st[True]
    st_ref[1] = st[False]

    @pl.when(i == pl.num_programs(1) - 1)
    def _():
        sf_ref[...] = st[True]
        sb_ref[...] = st[False]


def _hgrn(q, zf, zb, v, gp, s0f, s0b):
    b, t, w = q.shape
    tb = _tile(t, 256)
    nblk = t // tb
    fw = pl.BlockSpec((None, tb, w), lambda bb, i: (bb, i, 0))
    bw = pl.BlockSpec((None, tb, w), lambda bb, i: (bb, nblk - 1 - i, 0))
    st = pl.BlockSpec((None, w, w), lambda bb, i: (bb, 0, 0))
    return pl.pallas_call(
        functools.partial(_hg_kernel, nchunk=tb // HG_CHUNK),
        out_shape=(jax.ShapeDtypeStruct((b, t, w), F32), jax.ShapeDtypeStruct((b, t, w), F32),
                   jax.ShapeDtypeStruct((b, w, w), F32), jax.ShapeDtypeStruct((b, w, w), F32)),
        grid=(b, nblk),
        in_specs=[_full((8, w)), fw, fw, fw, bw, bw, bw, st, st],
        out_specs=(fw, bw, st, st),
        scratch_shapes=[pltpu.VMEM((2, w, w), F32)],
        compiler_params=_params(("parallel", "arbitrary")),
        name="hgrn2",
    )(gp, q, zf, v, q, zb, v, s0f, s0b)


def _gelu_tanh(x):
    return 0.5 * x * (1.0 + jnp.tanh(math.sqrt(2.0 / math.pi) * (x + 0.044715 * (x * x * x))))


def _outproj_kernel(x_ref, g1_ref, ya_ref, yb_ref, of_ref, ob_ref, hg_ref, yd_ref,
                    wglu_ref, bglu_ref, hn_ref, pavg_ref, wo_ref, o_ref, ys_ref):
    for j in range(S5_CHUNK):
        for v in range(2):
            ys_ref[v, pl.ds(j, ys_ref.shape[1] // S5_CHUNK, stride=S5_CHUNK), :] = ya_ref[j, :, v * LANE:(v + 1) * LANE]
    ya = _gelu_tanh(jnp.concatenate([ys_ref[0], ys_ref[1]], axis=1))
    gl = jnp.dot(ya.astype(BF16), wglu_ref[...], preferred_element_type=F32) + bglu_ref[...]
    o = of_ref[...] + ob_ref[...]
    o2 = o * o
    hi = o2.astype(BF16)
    lo = (o2 - hi.astype(F32)).astype(BF16)
    ms = (jnp.dot(hi, pavg_ref[...], preferred_element_type=F32)
          + jnp.dot(lo, pavg_ref[...], preferred_element_type=F32))
    y = (jnp.dot(yb_ref[...], wo_ref[256:768, :], preferred_element_type=F32)
         + jnp.dot(yd_ref[...], wo_ref[1024:1536, :], preferred_element_type=F32))
    ya = ya * (1.0 / (1.0 + jnp.exp(-gl)))
    gate = hg_ref[...]
    yc = o * lax.rsqrt(ms + EPS) * hn_ref[...] * (gate * (1.0 / (1.0 + jnp.exp(-gate))))
    y = (y + jnp.dot(ya.astype(BF16), wo_ref[0:256, :], preferred_element_type=F32)
         + jnp.dot(yc.astype(BF16), wo_ref[768:1024, :], preferred_element_type=F32))
    o_ref[...] = x_ref[...] + g1_ref[...] * y


def _outproj(x, g1, ya, yb, of, ob, hg, yd, wglu, bglu, hn, pavg, wo_p, li):
    b, t, d = x.shape
    tt = _tile(t, 512)
    xs = lambda n: pl.BlockSpec((None, tt, n), lambda bb, i: (bb, i, 0))
    vec = pl.BlockSpec((None, 1, d), lambda bb, i: (bb, 0, 0))
    return pl.pallas_call(
        _outproj_kernel,
        out_shape=jax.ShapeDtypeStruct((b, t, d), F32),
        grid=(b, t // tt),
        in_specs=[xs(d), vec, pl.BlockSpec((None, S5_CHUNK, tt // S5_CHUNK, 256), lambda bb, i: (bb, 0, i, 0)),
                  xs(512), xs(256), xs(256), xs(256), xs(512),
                  _layer(wglu, li), _full((1, 256)), _full((1, 256)), _full(pavg.shape), _layer(wo_p, li)],
        out_specs=xs(d),
        scratch_shapes=[pltpu.VMEM((2, tt, LANE), F32)],
        compiler_params=_params(("parallel", "parallel")),
        name="outproj",
    )(x, g1, ya, yb, of, ob, hg, yd, wglu, bglu, hn, pavg, wo_p)


def _ffn_kernel(x_ref, sc_ref, sh_ref, g2_ref, ng_ref, wup_ref, wdn_ref, fg_ref, o_ref, acc_ref, *, hidden, ck, final):
    x = x_ref[...]
    hb = (_rms(x, ng_ref[...]) * (1.0 + sc_ref[...]) + sh_ref[...]).astype(BF16)
    for j in range(hidden // ck):
        gate = jnp.dot(hb, wup_ref[:, j * ck:(j + 1) * ck], preferred_element_type=F32)
        up = jnp.dot(hb, wup_ref[:, hidden + j * ck:hidden + (j + 1) * ck], preferred_element_type=F32)
        a = (gate * (1.0 / (1.0 + jnp.exp(-gate))) * up).astype(BF16)
        part = jnp.dot(a, wdn_ref[j * ck:(j + 1) * ck, :], preferred_element_type=F32)
        if j == 0:
            acc_ref[...] = part
        else:
            acc_ref[...] += part
    y = x + g2_ref[...] * acc_ref[...]
    if final:
        y = _rms(y, fg_ref[...])
    o_ref[...] = y


def _ffn(x, sc, sh, g2, ng, wup, wdn, fg, final, li):
    b, t, d = x.shape
    hidden = wdn.shape[1]
    tt = _tile(t, 512)
    xs = pl.BlockSpec((None, tt, d), lambda bb, i: (bb, i, 0))
    vec = pl.BlockSpec((None, 1, d), lambda bb, i: (bb, 0, 0))
    return pl.pallas_call(
        functools.partial(_ffn_kernel, hidden=hidden, ck=_tile(hidden, 256), final=final),
        out_shape=jax.ShapeDtypeStruct((b, t, d), F32),
        grid=(b, t // tt),
        in_specs=[xs, vec, vec, vec, _full((1, d)), _layer(wup, li), _layer(wdn, li), _full((1, d))],
        out_specs=xs,
        scratch_shapes=[pltpu.VMEM((tt, d), F32)],
        compiler_params=_params(("parallel", "parallel")),
        name="ffn",
    )(x, sc, sh, g2, ng, wup, wdn, fg)


def _pad_heads(w, heads, dim):
    w = w.reshape(w.shape[:-1] + (heads, dim))
    w = jnp.pad(w, [(0, 0)] * (w.ndim - 1) + [(0, LANE - dim)])
    return w.reshape(w.shape[:-2] + (heads * LANE,))


def _layer_weights(w_in, w_out, mla_w_qb, mla_w_kvb):
    depth, d, n_in = w_in.shape
    qscale = jnp.ones((n_in,), F32).at[_C_SQ:_C_SQ + SWA_HEADS * SWA_HEAD_DIM].set(SWA_HEAD_DIM ** -0.5 * LOG2E)
    w_in_p = jnp.pad(w_in * qscale, ((0, 0), (0, 0), (0, _N_INP - n_in))).astype(BF16)
    wqb_p = _pad_heads(mla_w_qb * (MLA_SCALE * LOG2E), MLA_HEADS, MLA_NOPE + MLA_ROPE).astype(BF16)
    kvb = mla_w_kvb.reshape(depth, MLA_KV_RANK, MLA_HEADS, MLA_NOPE + MLA_V)
    wk = _pad_heads(kvb[..., :MLA_NOPE].reshape(depth, MLA_KV_RANK, -1), MLA_HEADS, MLA_NOPE)
    wv = _pad_heads(kvb[..., MLA_NOPE:].reshape(depth, MLA_KV_RANK, -1), MLA_HEADS, MLA_V)
    wkv_p = jnp.concatenate([wk, wv], axis=-1).astype(BF16)
    assert w_out.shape[1] == 4 * 256
    pad_rows = lambda w, heads, dim: jnp.swapaxes(_pad_heads(jnp.swapaxes(w, -1, -2), heads, dim), -1, -2)
    wo_p = jnp.concatenate([w_out[:, 0:256], pad_rows(w_out[:, 256:512], SWA_HEADS, SWA_HEAD_DIM),
                            w_out[:, 512:768], pad_rows(w_out[:, 768:1024], MLA_HEADS, MLA_V)], axis=1).astype(BF16)
    return w_in_p, wqb_p, wkv_p, wo_p


def _rope_tables(length, dim, lo, ident, repeat=1):
    n_freq = dim // 4
    rows = length // GRID_W
    row = jnp.repeat(jnp.arange(rows, dtype=F32), GRID_W)
    col = jnp.tile(jnp.arange(GRID_W, dtype=F32), rows)
    inv = ROPE_BASE ** (-jnp.arange(n_freq, dtype=F32) / n_freq)
    ang = jnp.stack([row[:, None] * inv, col[:, None] * inv], axis=1)
    cos, sin = jnp.cos(ang), jnp.sin(ang)
    z = jnp.zeros_like(sin)
    cos_l = jnp.stack([cos, cos], axis=2).reshape(length, dim)
    sina = jnp.stack([-sin, z], axis=2).reshape(length, dim)
    sinb = jnp.stack([z, sin], axis=2).reshape(length, dim)
    if ident:
        cos_l, sina, sinb = jnp.ones_like(cos_l), jnp.zeros_like(sina), jnp.zeros_like(sinb)
    cos_l, sina, sinb = (jnp.tile(a, (1, repeat)) for a in (cos_l, sina, sinb))
    pad = lambda a, fill: jnp.pad(a, ((0, 0), (lo, LANE - lo - dim * repeat)), constant_values=fill)
    return jnp.stack([pad(cos_l, 1.0), pad(sina, 0.0), pad(sinb, 0.0)], axis=0)


def kernel(x, c, ctx, c_ctx, w_mod, b_mod, norm1_g, norm2_g, w_in, w_out, s5_lam_re, s5_lam_im, s5_log_dt,
           s5_b_re, s5_b_im, s5_c_re, s5_c_im, s5_d, s5_w_glu, s5_b_glu, swa_sink, hg_lb, hg_norm_g,
           mla_q_norm_g, mla_w_qb, mla_kv_norm_g, mla_w_kvb, ffn_w_up, ffn_w_down, final_norm_g):
    b, t, d = x.shape
    lc = ctx.shape[1]
    depth = w_mod.shape[0]

    cc = jnp.zeros((8, d), F32).at[:b].set(c).at[b].set(c_ctx)
    mods = _modulation(cc, w_mod, b_mod)

    lb_cum = jnp.cumsum(jax.nn.softmax(hg_lb.astype(F32), axis=1), axis=1)
    lb = lb_cum - lb_cum[:, :1]

    tab_swa = _rope_tables(t, SWA_HEAD_DIM, 0, False, repeat=2)
    tab_mla = _rope_tables(t, MLA_ROPE, MLA_NOPE, False)
    tab_swa_c = _rope_tables(lc, SWA_HEAD_DIM, 0, True, repeat=2)
    tab_mla_c = _rope_tables(lc, MLA_ROPE, MLA_NOPE, True)
    pavg = jnp.kron(jnp.eye(HG_HEADS, dtype=F32), jnp.full((HG_DV, HG_DV), 1.0 / HG_DV, F32)).astype(BF16)
    zero_st = jnp.zeros((b, HG_HEADS * HG_DV, HG_HEADS * HG_DK), F32)
    zero_s5 = jnp.zeros((b, S5_GROUPS, 4 * S5_STATE), F32)

    w_in_p, wqb_p, wkv_p, wo_p = _layer_weights(w_in, w_out, mla_w_qb, mla_w_kvb)
    mats = _s5_matrices(s5_lam_re, s5_lam_im, s5_log_dt, s5_b_re, s5_b_im, s5_c_re, s5_c_im, s5_d)
    wglu = s5_w_glu.astype(BF16)
    wup = ffn_w_up.astype(BF16)
    wdn = ffn_w_down.astype(BF16)
    fg = final_norm_g.reshape(1, d)

    for i in range(depth):
        need_ctx = i < depth - 1
        mod = mods[i, :b].reshape(b, 6, 1, d)
        mod_c = jnp.broadcast_to(mods[i, b].reshape(1, 6, 1, d), (b, 6, 1, d))
        sh1, sc1, g1, sh2, sc2, g2 = (mod[:, j] for j in range(6))
        csh1, csc1, cg1, csh2, csc2, cg2 = (mod_c[:, j] for j in range(6))
        n1 = norm1_g[i].reshape(1, d)
        qg = mla_q_norm_g[i].reshape(1, -1)
        kvg = mla_kv_norm_g[i].reshape(1, -1)
        px = _inproj(x, sc1, sh1, n1, w_in_p, tab_swa, tab_mla, qg, wqb_p, kvg, wkv_p, i)
        pc = _inproj(ctx, csc1, csh1, n1, w_in_p, tab_swa_c, tab_mla_c, qg, wqb_p, kvg, wkv_p, i)
        (xu, xsq, xsk, xsv, xhq, xhzf, xhzb, xhi, xhg, xmq, xmk, xmv) = px
        (cu, csq, csk, csv, chq, chzf, chzb, chi, chg, cmq, cmk, cmv) = pc

        ya_c, s5_fin = _s5_scan(cu, mats, zero_s5, i)
        ya, _ = _s5_scan(xu, mats, s5_fin, i)

        grp = SWA_HEADS // SWA_KV_HEADS
        sink = swa_sink[i].astype(F32) * LOG2E
        yb = _attention(xsq, csk, csv, xsk, xsv, grp=grp, sink=sink, window=SWA_WINDOW)
        lbf, lbb = lb[0, i], lb[1, i]
        gp = jnp.stack([jnp.log(lbf), jnp.log1p(-lbf), 1.0 - lbf,
                        jnp.log(lbb), jnp.log1p(-lbb), 1.0 - lbb, lbf, lbb], axis=0)
        of_c, ob_c, stf, stb = _hgrn(chq, chzf, chzb, chi, gp, zero_st, zero_st)
        of, ob, _, _ = _hgrn(xhq, xhzf, xhzb, xhi, gp, stf, stb)
        yd = _attention(xmq, cmk, cmv, xmk, xmv, grp=1, bq=1024)

        bglu = s5_b_glu[i].reshape(1, -1)
        hn = jnp.tile(hg_norm_g[i], HG_HEADS).reshape(1, -1)
        x = _outproj(x, g1, ya, yb, of, ob, xhg, yd, wglu, bglu, hn, pavg, wo_p, i)
        n2 = norm2_g[i].reshape(1, d)
        x = _ffn(x, sc2, sh2, g2, n2, wup, wdn, fg, not need_ctx, i)
        if need_ctx:
            yb_c = _attention(csq, csk, csv, grp=grp, sink=sink)
            yd_c = _attention(cmq, cmk, cmv, grp=1)
            ctx = _outproj(ctx, cg1, ya_c, yb_c, of_c, ob_c, chg, yd_c, wglu, bglu, hn, pavg, wo_p, i)
            ctx = _ffn(ctx, csc2, csh2, cg2, n2, wup, wdn, fg, False, i)
    return x
```

```python
import functools
import math

import jax
import jax.numpy as jnp
from jax import lax
from jax.experimental import pallas as pl
from jax.experimental.pallas import tpu as pltpu

F32 = jnp.float32
BF16 = jnp.bfloat16

EPS = 1e-6
NEG_INF = -1e30
ROPE_BASE = 10000.0
GRID_W = 64
LANE = 128
VMEM_LIMIT = 56 * 1024 * 1024

S5_CH, S5_GROUP, S5_STATE = 256, 16, 64
S5_GROUPS = S5_CH // S5_GROUP
S5_CHUNK = 16
SWA_HEADS, SWA_KV_HEADS, SWA_HEAD_DIM, SWA_WINDOW = 4, 2, 64, 128
HG_HEADS, HG_DK, HG_DV = 4, 64, 64
HG_CHUNK = 64
MLA_HEADS, MLA_Q_RANK, MLA_KV_RANK = 4, 256, 128
MLA_NOPE, MLA_ROPE, MLA_V = 64, 32, 64
MLA_SCALE = (MLA_NOPE + MLA_ROPE) ** -0.5
LOG2E = 1.4426950408889634
V_ONE = 64


def _tile(n, pref):
    t = min(n, pref)
    assert n % t == 0, (n, pref)
    return t


def _params(sem):
    return pltpu.CompilerParams(dimension_semantics=sem, vmem_limit_bytes=VMEM_LIMIT)


def _full(shape):
    nd = len(shape)
    return pl.BlockSpec(shape, lambda *_: (0,) * nd, pipeline_mode=pl.Buffered(1))


def _layer(w, li):
    nd = w.ndim - 1
    return pl.BlockSpec((None,) + w.shape[1:], lambda *_: (li,) + (0,) * nd, pipeline_mode=pl.Buffered(1))


def _mod_kernel(c_ref, w_ref, b_ref, o_ref):
    c = c_ref[...]
    s = c * (1.0 / (1.0 + jnp.exp(-c)))
    o_ref[...] = jnp.dot(s.astype(BF16), w_ref[...].astype(BF16), preferred_element_type=F32) + b_ref[...]


def _modulation(cc, w_mod, b_mod):
    depth, d, n = w_mod.shape
    tn = _tile(n, 1536)
    return pl.pallas_call(
        _mod_kernel,
        out_shape=jax.ShapeDtypeStruct((depth, 8, n), F32),
        grid=(depth, n // tn),
        in_specs=[pl.BlockSpec((8, d), lambda l, j: (0, 0)),
                  pl.BlockSpec((None, d, tn), lambda l, j: (l, 0, j)),
                  pl.BlockSpec((None, 1, tn), lambda l, j: (l, 0, j))],
        out_specs=pl.BlockSpec((None, 8, tn), lambda l, j: (l, 0, j)),
        compiler_params=_params(("parallel", "parallel")),
        name="modulation",
    )(cc, w_mod, b_mod.reshape(depth, 1, n))


def _rope_block(x, t_ref, half):
    return (x * t_ref[0] + pltpu.roll(x, LANE - half, 1) * t_ref[1]
            + pltpu.roll(x, half, 1) * t_ref[2])


def _rms(x, g):
    return x * lax.rsqrt(jnp.mean(x * x, axis=-1, keepdims=True) + EPS) * g


_C_U = 0
_C_SQ = 256
_C_SKV = 512
_C_HG = 768
_C_CQ = _C_HG + 5 * 256
_C_CKV = _C_CQ + 256
_N_INP = _C_CKV + 256


def _inproj_kernel(x_ref, sc_ref, sh_ref, g_ref, w_ref, ts_ref, tm_ref, qg_ref, wqb_ref, kvg_ref, wkv_ref,
                   u_ref, sq_ref, sk_ref, sv_ref, hq_ref, hzf_ref, hzb_ref, hi_ref, hg_ref,
                   mq_ref, mk_ref, mv_ref, us_ref):
    h = _rms(x_ref[...], g_ref[...]) * (1.0 + sc_ref[...]) + sh_ref[...]
    hb = h.astype(BF16)

    def proj(lo, n):
        return jnp.dot(hb, w_ref[:, lo:lo + n], preferred_element_type=F32)

    cq = _rms(proj(_C_CQ, 256), qg_ref[...]).astype(BF16)
    ckv_kr = proj(_C_CKV, 256)
    ckv = _rms(ckv_kr[:, :LANE], kvg_ref[...]).astype(BF16)
    kr = pltpu.roll(ckv_kr[:, LANE:], MLA_NOPE, 1)

    u = proj(_C_U, 256)
    for v in range(2):
        us_ref[v] = u[:, v * LANE:(v + 1) * LANE]
    for j in range(S5_CHUNK):
        for v in range(2):
            u_ref[j, :, v * LANE:(v + 1) * LANE] = us_ref[v, pl.ds(j, us_ref.shape[1] // S5_CHUNK, stride=S5_CHUNK), :]
    lane = lax.broadcasted_iota(jnp.int32, (1, LANE), 1)
    low = lane < SWA_HEAD_DIM
    one = jnp.where(lane == V_ONE, 1.0, 0.0)

    def spread(pair, fill):
        return [jnp.where(low, blk, fill).astype(BF16) for blk in (pair, pltpu.roll(pair, SWA_HEAD_DIM, 1))]

    sq = proj(_C_SQ, 256)
    for pp in range(SWA_HEADS // 2):
        heads = spread(_rope_block(sq[:, pp * LANE:(pp + 1) * LANE], ts_ref, 16), 0.0)
        for e in range(2):
            sq_ref[:, (2 * pp + e) * LANE:(2 * pp + e + 1) * LANE] = heads[e]
    skv = proj(_C_SKV, 256)
    for e, blk in enumerate(spread(_rope_block(skv[:, :LANE], ts_ref, 16), 0.0)):
        sk_ref[:, e * LANE:(e + 1) * LANE] = blk
    for e, blk in enumerate(spread(skv[:, LANE:], one)):
        sv_ref[:, e * LANE:(e + 1) * LANE] = blk

    q = jnp.dot(cq, wqb_ref[...], preferred_element_type=F32)
    for hh in range(MLA_HEADS):
        sl = slice(hh * LANE, (hh + 1) * LANE)
        mq_ref[:, sl] = _rope_block(q[:, sl], tm_ref, 8).astype(BF16)
    for i, r in enumerate((hq_ref, hzf_ref, hzb_ref)):
        r[...] = proj(_C_HG + i * 256, 256)
    kv = jnp.dot(ckv, wkv_ref[...], preferred_element_type=F32)
    for hh in range(MLA_HEADS):
        sl = slice(hh * LANE, (hh + 1) * LANE)
        mk_ref[:, sl] = _rope_block(kv[:, sl] + kr, tm_ref, 8).astype(BF16)
        mv_ref[:, sl] = jnp.where(lane == V_ONE, 1.0, kv[:, MLA_HEADS * LANE + hh * LANE:][:, :LANE]).astype(BF16)
    for i, r in enumerate((hi_ref, hg_ref)):
        r[...] = proj(_C_HG + (3 + i) * 256, 256)


def _inproj(x, sc, sh, g, w_in_p, tab_swa, tab_mla, qg, wqb_p, kvg, wkv_p, li):
    b, t, d = x.shape
    tt = _tile(t, 512)
    row = lambda n, dt: jax.ShapeDtypeStruct((b, t, n), dt)
    out_shape = (jax.ShapeDtypeStruct((b, S5_CHUNK, t // S5_CHUNK, 256), F32),
                 row(512, BF16), row(256, BF16), row(256, BF16),
                 row(256, F32), row(256, F32), row(256, F32), row(256, F32), row(256, F32),
                 row(512, BF16), row(512, BF16), row(512, BF16))
    xs = lambda n: pl.BlockSpec((None, tt, n), lambda i, bb: (bb, i, 0))
    vec = pl.BlockSpec((None, 1, d), lambda i, bb: (bb, 0, 0))
    tab = pl.BlockSpec((3, tt, LANE), lambda i, bb: (0, i, 0))
    return pl.pallas_call(
        _inproj_kernel,
        out_shape=out_shape,
        grid=(t // tt, b),
        in_specs=[xs(d), vec, vec, _full((1, d)), _layer(w_in_p, li), tab, tab,
                  _full((1, 256)), _layer(wqb_p, li), _full((1, 128)), _layer(wkv_p, li)],
        out_specs=(pl.BlockSpec((None, S5_CHUNK, tt // S5_CHUNK, 256), lambda i, bb: (bb, 0, i, 0)),)
        + tuple(xs(s.shape[-1]) for s in out_shape[1:]),
        scratch_shapes=[pltpu.VMEM((2, tt, LANE), F32)],
        compiler_params=_params(("parallel", "parallel")),
        name="inproj",
    )(x, sc, sh, g, w_in_p, tab_swa, tab_mla, qg, wqb_p, kvg, wkv_p)


def _s5_kernel(u_ref, mt_ref, ft_ref, et_ref, a_ref, s0_ref, y_ref, sfin_ref, zs_ref, ut_ref, yt_ref, *, nc):
    ng, half = S5_GROUPS, LANE // S5_GROUP
    for j in range(S5_CHUNK):
        for v in range(2):
            t = u_ref[j, :, v * LANE:(v + 1) * LANE].T
            for gl in range(half):
                ut_ref[half * v + gl, S5_GROUP * j:S5_GROUP * (j + 1), :] = (
                    t[S5_GROUP * gl:S5_GROUP * (gl + 1), :].astype(BF16))
    for g in range(ng):
        z = jnp.dot(ft_ref[g], ut_ref[g], preferred_element_type=F32).T
        for c in range(2):
            zs_ref[c, pl.ds(g, nc, stride=ng), :] = z[:, c * LANE:(c + 1) * LANE]

    fmask = lax.broadcasted_iota(jnp.int32, (1, LANE), 1) < S5_STATE
    ar, ai = a_ref[0], a_ref[1]

    def step(i, s):
        rf = pl.multiple_of(i * ng, ng)
        rb = pl.multiple_of((nc - 1 - i) * ng, ng)
        zre, zim = (jnp.where(fmask, zs_ref[c, pl.ds(rf, ng), :], zs_ref[c, pl.ds(rb, ng), :]) for c in range(2))
        for c in range(2):
            zs_ref[c, pl.ds(rf, ng), 0:S5_STATE] = s[c][:, 0:S5_STATE]
            zs_ref[c, pl.ds(rb, ng), S5_STATE:LANE] = s[c][:, S5_STATE:LANE]
        re, im = s
        return ar * re - ai * im + zre, ar * im + ai * re + zim

    re, im = lax.fori_loop(0, nc, step, (s0_ref[:, :LANE], s0_ref[:, LANE:]), unroll=4)
    sfin_ref[:, :LANE] = re
    sfin_ref[:, LANE:] = im

    for g in range(ng):
        ss = jnp.concatenate([zs_ref[c, pl.ds(g, nc, stride=ng), :] for c in range(2)], axis=1)
        yt = (jnp.dot(mt_ref[g], ut_ref[g], preferred_element_type=F32)
              + jnp.dot(et_ref[g], ss.T.astype(BF16), preferred_element_type=F32))
        for t in range(S5_CHUNK):
            yt_ref[t, S5_GROUP * g:S5_GROUP * (g + 1), :] = yt[S5_GROUP * t:S5_GROUP * (t + 1), :]
    for t in range(S5_CHUNK):
        y_ref[t] = yt_ref[t].T


def _s5_scan(uj, mats, s0, li):
    mt, ft, et, a16 = mats
    b, _, nc, _ = uj.shape
    slab = pl.BlockSpec((None, S5_CHUNK, nc, 256), lambda bb: (bb, 0, 0, 0), pipeline_mode=pl.Buffered(1))
    st = pl.BlockSpec((None, S5_GROUPS, 256), lambda bb: (bb, 0, 0))
    return pl.pallas_call(
        functools.partial(_s5_kernel, nc=nc),
        out_shape=(jax.ShapeDtypeStruct(uj.shape, F32), jax.ShapeDtypeStruct((b, S5_GROUPS, 256), F32)),
        grid=(b,),
        in_specs=[slab, _layer(mt, li), _layer(ft, li), _layer(et, li), _layer(a16, li), st],
        out_specs=(slab, st),
        scratch_shapes=[pltpu.VMEM((2, S5_GROUPS * nc, LANE), F32), pltpu.VMEM((S5_GROUPS, 256, nc), BF16),
                        pltpu.VMEM((S5_CHUNK, 256, nc), F32)],
        compiler_params=_params(("parallel",)),
        name="s5",
    )(uj, mt, ft, et, a16, s0)


def _s5_matrices(lam_re, lam_im, log_dt, b_re, b_im, c_re, c_im, d_skip):
    L = S5_CHUNK
    depth = lam_re.shape[0]
    hp = lax.Precision.HIGHEST
    lam = lax.complex(lam_re, lam_im)
    ldt = lam * jnp.exp(log_dt)
    bb = ((jnp.exp(ldt) - 1.0) / lam)[..., None] * lax.complex(b_re, b_im)
    cc = lax.complex(c_re, c_im)
    k = jnp.arange(L + 1, dtype=F32)
    apow = jnp.exp(ldt[:, None] * k[None, :, None, None, None])
    kern = jnp.real(jnp.einsum('ndghp,nkdgp,ndgpi->ndkgih', cc, apow[:, :L], bb, precision=hp))
    idx = jnp.arange(L)
    lag = idx[None, :] - idx[:, None]
    oh_f = (lag[:, :, None] == idx[None, None, :]).astype(F32)
    oh_b = (-lag[:, :, None] == idx[None, None, :]).astype(F32)
    eye = jnp.eye(L, dtype=F32)
    dg = d_skip.reshape(depth, S5_GROUPS, S5_GROUP)
    m = (jnp.einsum('jtk,nkgih->ngjith', oh_f, kern[:, 0], precision=hp)
         + jnp.einsum('jtk,nkgih->ngjith', oh_b, kern[:, 1], precision=hp)
         + jnp.einsum('jt,ih,ngh->ngjith', eye, jnp.eye(S5_GROUP, dtype=F32), dg, precision=hp))
    m = m.reshape(depth, S5_GROUPS, L * S5_GROUP, L * S5_GROUP)
    zf = apow[:, L - 1 - idx, 0][..., None] * bb[:, 0][:, None]
    zb = apow[:, idx, 1][..., None] * bb[:, 1][:, None]
    fmat = jnp.concatenate([jnp.real(zf), jnp.real(zb), jnp.imag(zf), jnp.imag(zb)], axis=3)
    fmat = fmat.transpose(0, 2, 1, 4, 3).reshape(depth, S5_GROUPS, L * S5_GROUP, 4 * S5_STATE)
    wf = cc[:, 0][:, None] * apow[:, 1 + idx, 0][:, :, :, None, :]
    wb = cc[:, 1][:, None] * apow[:, L - idx, 1][:, :, :, None, :]
    emat = jnp.concatenate([jnp.real(wf), jnp.real(wb), -jnp.imag(wf), -jnp.imag(wb)], axis=4)
    emat = emat.transpose(0, 2, 4, 1, 3).reshape(depth, S5_GROUPS, 4 * S5_STATE, L * S5_GROUP)
    al = apow[:, L]
    a16 = jnp.stack([jnp.concatenate([jnp.real(al[:, 0]), jnp.real(al[:, 1])], axis=-1),
                     jnp.concatenate([jnp.imag(al[:, 0]), jnp.imag(al[:, 1])], axis=-1)], axis=1)
    tr = lambda w: jnp.swapaxes(w, -1, -2).astype(BF16)
    return tr(m), tr(fmat), tr(emat), a16.astype(F32)


def _attn_kernel(*refs, grp, bq, sub, has_sink, has_seq, window, t_seq, bk):
    refs = list(refs)
    sink_ref = refs.pop(0) if has_sink else None
    q_ref, kc_ref, vc_ref = refs[:3]
    ks_ref, vs_ref = (refs[3], refs[4]) if has_seq else (None, None)
    o_ref = refs[-1]
    hk = pl.program_id(1)
    qi = pl.program_id(2)

    def scores(q, k, mask=None):
        s = lax.dot_general(q, k, (((1,), (1,)), ((), ())), preferred_element_type=F32)
        return s if mask is None else jnp.where(mask, s, NEG_INF)

    def first_pass(r0, nq):
        rows = grp * nq
        q = jnp.concatenate([q_ref[r0:r0 + nq, g * LANE:(g + 1) * LANE] for g in range(grp)], axis=0)
        s_parts, v_parts = [scores(q, kc_ref[...])], [vc_ref[...]]
        if has_seq and window is not None:
            wlen = nq + 2 * window
            q0 = qi * bq + r0
            start = pl.multiple_of(jnp.clip(q0 - window, 0, t_seq - wlen), LANE)
            qpos = q0 + lax.broadcasted_iota(jnp.int32, (rows, wlen), 0) % nq
            kpos = start + lax.broadcasted_iota(jnp.int32, (rows, wlen), 1)
            s_parts.append(scores(q, ks_ref[pl.ds(start, wlen), :], jnp.abs(qpos - kpos) <= window))
            v_parts.append(vs_ref[pl.ds(start, wlen), :])
        elif has_seq:
            s_parts.append(scores(q, ks_ref[0:bk, :]))
            v_parts.append(vs_ref[0:bk, :])
        m = functools.reduce(jnp.maximum, [jnp.max(s, axis=-1, keepdims=True) for s in s_parts])
        if has_sink:
            sink = jnp.concatenate([jnp.full((nq, 1), sink_ref[hk * grp + g], F32) for g in range(grp)], axis=0)
            m = jnp.maximum(m, sink)
        acc = sum(jnp.dot(jnp.exp2(s - m).astype(BF16), v, preferred_element_type=F32)
                  for s, v in zip(s_parts, v_parts))
        if has_sink:
            one_lane = lax.broadcasted_iota(jnp.int32, (1, LANE), 1) == V_ONE
            acc = acc + jnp.where(one_lane, jnp.exp2(sink - m), 0.0)
        return q, m, acc

    def finish(r0, nq, acc):
        o = acc * (1.0 / acc[:, V_ONE:V_ONE + 1])
        for g in range(grp):
            o_ref[r0:r0 + nq, g * LANE:(g + 1) * LANE] = o[g * nq:(g + 1) * nq].astype(o_ref.dtype)

    if has_seq and window is not None:
        for r0 in range(0, bq, sub):
            _, _, acc = first_pass(r0, sub)
            finish(r0, sub, acc)
        return
    q, m, acc = first_pass(0, bq)
    if has_seq:
        def body(j, carry):
            m, acc = carry
            st = pl.multiple_of(j * bk, bk)
            s = scores(q, ks_ref[pl.ds(st, bk), :])
            m_new = jnp.maximum(m, jnp.max(s, axis=-1, keepdims=True))
            p = jnp.exp2(s - m_new).astype(BF16)
            acc = jnp.exp2(m - m_new) * acc + jnp.dot(p, vs_ref[pl.ds(st, bk), :], preferred_element_type=F32)
            return m_new, acc
        m, acc = lax.fori_loop(1, t_seq // bk, body, (m, acc), unroll=True)
    finish(0, bq, acc)


def _attention(q, kc, vc, ks=None, vs=None, *, grp, sink=None, window=None, bq=512, bk=1024):
    b, tq, wq = q.shape
    hkv = wq // (grp * LANE)
    lc = kc.shape[1]
    has_seq = ks is not None
    t_seq = ks.shape[1] if has_seq else 0
    bq = _tile(tq, bq)
    sub = min(bq, LANE)
    if has_seq and window is None:
        bk = _tile(t_seq, bk)
    kern = functools.partial(_attn_kernel, grp=grp, bq=bq, sub=sub, has_sink=sink is not None, has_seq=has_seq,
                             window=window, t_seq=t_seq, bk=bk)
    in_specs, args = [], []
    if sink is not None:
        in_specs.append(pl.BlockSpec(memory_space=pltpu.SMEM))
        args.append(sink)
    in_specs.append(pl.BlockSpec((None, bq, grp * LANE), lambda bb, h, i: (bb, i, h)))
    args.append(q)
    ctx_spec = pl.BlockSpec((None, lc, LANE), lambda bb, h, i: (bb, 0, h))
    in_specs += [ctx_spec, ctx_spec]
    args += [kc, vc]
    if has_seq:
        seq_spec = pl.BlockSpec((None, t_seq, LANE), lambda bb, h, i: (bb, 0, h))
        in_specs += [seq_spec, seq_spec]
        args += [ks, vs]
    return pl.pallas_call(
        kern,
        out_shape=jax.ShapeDtypeStruct((b, tq, wq), BF16),
        grid=(b, hkv, tq // bq),
        in_specs=in_specs,
        out_specs=pl.BlockSpec((None, bq, grp * LANE), lambda bb, h, i: (bb, i, h)),
        compiler_params=_params(("parallel", "parallel", "parallel")),
        name="attention",
    )(*args)


def _split3(x):
    hi = x.astype(BF16)
    r = x - hi.astype(F32)
    mid = r.astype(BF16)
    lo = (r - mid.astype(F32)).astype(BF16)
    return hi, mid, lo


def _hg_kernel(gp_ref, qf_ref, zf_ref, vf_ref, qb_ref, zb_ref, vb_ref, s0f_ref, s0b_ref,
               of_ref, ob_ref, sf_ref, sb_ref, st_ref, *, nchunk):
    i = pl.program_id(1)
    c = HG_CHUNK
    w = HG_HEADS * HG_DK

    @pl.when(i == 0)
    def _():
        st_ref[0] = s0f_ref[...]
        st_ref[1] = s0b_ref[...]

    ri = lax.broadcasted_iota(jnp.int32, (c, c), 0)
    ci = lax.broadcasted_iota(jnp.int32, (c, c), 1)
    ti = lax.broadcasted_iota(jnp.int32, (c, HG_HEADS * c), 0)
    si = lax.broadcasted_iota(jnp.int32, (c, HG_HEADS * c), 1) % c
    lane = lax.broadcasted_iota(jnp.int32, (1, w), 1)
    hmask = [(lane // HG_DK) == h for h in range(HG_HEADS)]
    r2 = lax.broadcasted_iota(jnp.int32, (w, w), 0) // HG_DV
    c2 = lax.broadcasted_iota(jnp.int32, (w, w), 1) // HG_DK
    diag = r2 == c2
    dirs = {
        True: dict(q=qf_ref, z=zf_ref, v=vf_ref, o=of_ref, gp=tuple(gp_ref[r:r + 1, :] for r in range(3)),
                   tri=jnp.where(ci <= ri, 1.0, 0.0).astype(BF16), causal=si <= ti),
        False: dict(q=qb_ref, z=zb_ref, v=vb_ref, o=ob_ref, gp=tuple(gp_ref[r:r + 1, :] for r in range(3, 6)),
                    tri=jnp.where(ci >= ri, 1.0, 0.0).astype(BF16), causal=si >= ti),
    }
    units = []
    for n in range(nchunk):
        units += [(True, n), (False, nchunk - 1 - n)]

    ph1 = []
    for fwd, n in units:
        dd = dirs[fwd]
        sl = slice(n * c, (n + 1) * c)
        z = dd['z'][sl, :]
        log_lb, l1p, om = dd['gp']
        ls = jnp.minimum(z, 0.0) - jnp.log(1.0 + jnp.exp(-jnp.abs(z)))
        bterm = l1p + ls
        lf = jnp.maximum(log_lb, bterm) + jnp.log(1.0 + jnp.exp(-jnp.abs(log_lb - bterm)))
        k = om / (1.0 + jnp.exp(z))
        gsum = sum(jnp.dot(dd['tri'], part, preferred_element_type=F32) for part in _split3(lf))
        ph1.append((k, gsum))

    ph2 = []
    for (fwd, n), (k, gsum) in zip(units, ph1):
        dd = dirs[fwd]
        sl = slice(n * c, (n + 1) * c)
        q, v = dd['q'][sl, :], dd['v'][sl, :]
        tot = gsum[c - 1:c, :] if fwd else gsum[0:1, :]
        gm = gsum[c // 2:c // 2 + 1, :]
        qd = (q * jnp.exp(gsum - gm)).astype(BF16)
        kinv = k * jnp.exp(gm - gsum)
        q_in = (q * jnp.exp(gsum)).astype(BF16)
        k_end = (k * jnp.exp(tot - gsum)).astype(BF16)
        kstack = jnp.concatenate([jnp.where(hm, kinv, 0.0) for hm in hmask], axis=0).astype(BF16)
        vstack = jnp.concatenate([jnp.where(hm, v, 0.0) for hm in hmask], axis=0).astype(BF16)
        ph2.append((qd, kstack, vstack, q_in, k_end, v.astype(BF16), jnp.exp(tot)))

    ph3 = []
    for (fwd, n), (qd, kstack, vstack, q_in, k_end, vb, dec) in zip(units, ph2):
        att = lax.dot_general(qd, kstack, (((1,), (1,)), ((), ())), preferred_element_type=F32)
        att = jnp.where(dirs[fwd]['causal'], att, 0.0).astype(BF16)
        o_intra = jnp.dot(att, vstack, preferred_element_type=F32)
        kv_t = lax.dot_general(vb, k_end, (((0,), (0,)), ((), ())), preferred_element_type=F32)
        ph3.append((o_intra, jnp.where(diag, kv_t, 0.0)))

    st = {True: st_ref[0], False: st_ref[1]}
    for (fwd, n), (_, _, _, q_in, _, _, dec), (o_intra, kv_t) in zip(units, ph2, ph3):
        sl = slice(n * c, (n + 1) * c)
        o_inter = lax.dot_general(q_in, st[fwd].astype(BF16), (((1,), (1,)), ((), ())), preferred_element_type=F32)
        dirs[fwd]['o'][sl, :] = o_intra + o_inter
        st[fwd] = st[fwd] * dec + kv_t
    st_ref[0] = st[True]
    st_ref[1] = st[False]

    @pl.when(i == pl.num_programs(1) - 1)
    def _():
        sf_ref[...] = st[True]
        sb_ref[...] = st[False]


def _hgrn(q, zf, zb, v, gp, s0f, s0b):
    b, t, w = q.shape
    tb = _tile(t, 512)
    nblk = t // tb
    fw = pl.BlockSpec((None, tb, w), lambda bb, i: (bb, i, 0))
    bw = pl.BlockSpec((None, tb, w), lambda bb, i: (bb, nblk - 1 - i, 0))
    st = pl.BlockSpec((None, w, w), lambda bb, i: (bb, 0, 0))
    return pl.pallas_call(
        functools.partial(_hg_kernel, nchunk=tb // HG_CHUNK),
        out_shape=(jax.ShapeDtypeStruct((b, t, w), F32), jax.ShapeDtypeStruct((b, t, w), F32),
                   jax.ShapeDtypeStruct((b, w, w), F32), jax.ShapeDtypeStruct((b, w, w), F32)),
        grid=(b, nblk),
        in_specs=[_full((8, w)), fw, fw, fw, bw, bw, bw, st, st],
        out_specs=(fw, bw, st, st),
        scratch_shapes=[pltpu.VMEM((2, w, w), F32)],
        compiler_params=_params(("parallel", "arbitrary")),
        name="hgrn2",
    )(gp, q, zf, v, q, zb, v, s0f, s0b)


def _gelu_tanh(x):
    return 0.5 * x * (1.0 + jnp.tanh(math.sqrt(2.0 / math.pi) * (x + 0.044715 * (x * x * x))))


def _outproj_kernel(x_ref, g1_ref, ya_ref, yb_ref, of_ref, ob_ref, hg_ref, yd_ref,
                    wglu_ref, bglu_ref, hn_ref, pavg_ref, wo_ref, o_ref, ys_ref):
    for j in range(S5_CHUNK):
        for v in range(2):
            ys_ref[v, pl.ds(j, ys_ref.shape[1] // S5_CHUNK, stride=S5_CHUNK), :] = ya_ref[j, :, v * LANE:(v + 1) * LANE]
    ya = _gelu_tanh(jnp.concatenate([ys_ref[0], ys_ref[1]], axis=1))
    gl = jnp.dot(ya.astype(BF16), wglu_ref[...], preferred_element_type=F32) + bglu_ref[...]
    o = of_ref[...] + ob_ref[...]
    o2 = o * o
    hi = o2.astype(BF16)
    lo = (o2 - hi.astype(F32)).astype(BF16)
    ms = (jnp.dot(hi, pavg_ref[...], preferred_element_type=F32)
          + jnp.dot(lo, pavg_ref[...], preferred_element_type=F32))
    y = (jnp.dot(yb_ref[...], wo_ref[256:768, :], preferred_element_type=F32)
         + jnp.dot(yd_ref[...], wo_ref[1024:1536, :], preferred_element_type=F32))
    ya = ya * (1.0 / (1.0 + jnp.exp(-gl)))
    gate = hg_ref[...]
    yc = o * lax.rsqrt(ms + EPS) * hn_ref[...] * (gate * (1.0 / (1.0 + jnp.exp(-gate))))
    y = (y + jnp.dot(ya.astype(BF16), wo_ref[0:256, :], preferred_element_type=F32)
         + jnp.dot(yc.astype(BF16), wo_ref[768:1024, :], preferred_element_type=F32))
    o_ref[...] = x_ref[...] + g1_ref[...] * y


def _outproj(x, g1, ya, yb, of, ob, hg, yd, wglu, bglu, hn, pavg, wo_p, li):
    b, t, d = x.shape
    tt = _tile(t, 512)
    xs = lambda n: pl.BlockSpec((None, tt, n), lambda bb, i: (bb, i, 0))
    vec = pl.BlockSpec((None, 1, d), lambda bb, i: (bb, 0, 0))
    return pl.pallas_call(
        _outproj_kernel,
        out_shape=jax.ShapeDtypeStruct((b, t, d), F32),
        grid=(b, t // tt),
        in_specs=[xs(d), vec, pl.BlockSpec((None, S5_CHUNK, tt // S5_CHUNK, 256), lambda bb, i: (bb, 0, i, 0)),
                  xs(512), xs(256), xs(256), xs(256), xs(512),
                  _layer(wglu, li), _full((1, 256)), _full((1, 256)), _full(pavg.shape), _layer(wo_p, li)],
        out_specs=xs(d),
        scratch_shapes=[pltpu.VMEM((2, tt, LANE), F32)],
        compiler_params=_params(("parallel", "parallel")),
        name="outproj",
    )(x, g1, ya, yb, of, ob, hg, yd, wglu, bglu, hn, pavg, wo_p)


def _ffn_kernel(x_ref, sc_ref, sh_ref, g2_ref, ng_ref, wup_ref, wdn_ref, fg_ref, o_ref, acc_ref, *, hidden, ck, final):
    x = x_ref[...]
    hb = (_rms(x, ng_ref[...]) * (1.0 + sc_ref[...]) + sh_ref[...]).astype(BF16)
    for j in range(hidden // ck):
        gate = jnp.dot(hb, wup_ref[:, j * ck:(j + 1) * ck], preferred_element_type=F32)
        up = jnp.dot(hb, wup_ref[:, hidden + j * ck:hidden + (j + 1) * ck], preferred_element_type=F32)
        a = (gate * (1.0 / (1.0 + jnp.exp(-gate))) * up).astype(BF16)
        part = jnp.dot(a, wdn_ref[j * ck:(j + 1) * ck, :], preferred_element_type=F32)
        if j == 0:
            acc_ref[...] = part
        else:
            acc_ref[...] += part
    y = x + g2_ref[...] * acc_ref[...]
    if final:
        y = _rms(y, fg_ref[...])
    o_ref[...] = y


def _ffn(x, sc, sh, g2, ng, wup, wdn, fg, final, li):
    b, t, d = x.shape
    hidden = wdn.shape[1]
    tt = _tile(t, 512)
    xs = pl.BlockSpec((None, tt, d), lambda bb, i: (bb, i, 0))
    vec = pl.BlockSpec((None, 1, d), lambda bb, i: (bb, 0, 0))
    return pl.pallas_call(
        functools.partial(_ffn_kernel, hidden=hidden, ck=_tile(hidden, 256), final=final),
        out_shape=jax.ShapeDtypeStruct((b, t, d), F32),
        grid=(b, t // tt),
        in_specs=[xs, vec, vec, vec, _full((1, d)), _layer(wup, li), _layer(wdn, li), _full((1, d))],
        out_specs=xs,
        scratch_shapes=[pltpu.VMEM((tt, d), F32)],
        compiler_params=_params(("parallel", "parallel")),
        name="ffn",
    )(x, sc, sh, g2, ng, wup, wdn, fg)


def _pad_heads(w, heads, dim):
    w = w.reshape(w.shape[:-1] + (heads, dim))
    w = jnp.pad(w, [(0, 0)] * (w.ndim - 1) + [(0, LANE - dim)])
    return w.reshape(w.shape[:-2] + (heads * LANE,))


def _layer_weights(w_in, w_out, mla_w_qb, mla_w_kvb):
    depth, d, n_in = w_in.shape
    qscale = jnp.ones((n_in,), F32).at[_C_SQ:_C_SQ + SWA_HEADS * SWA_HEAD_DIM].set(SWA_HEAD_DIM ** -0.5 * LOG2E)
    w_in_p = jnp.pad(w_in * qscale, ((0, 0), (0, 0), (0, _N_INP - n_in))).astype(BF16)
    wqb_p = _pad_heads(mla_w_qb * (MLA_SCALE * LOG2E), MLA_HEADS, MLA_NOPE + MLA_ROPE).astype(BF16)
    kvb = mla_w_kvb.reshape(depth, MLA_KV_RANK, MLA_HEADS, MLA_NOPE + MLA_V)
    wk = _pad_heads(kvb[..., :MLA_NOPE].reshape(depth, MLA_KV_RANK, -1), MLA_HEADS, MLA_NOPE)
    wv = _pad_heads(kvb[..., MLA_NOPE:].reshape(depth, MLA_KV_RANK, -1), MLA_HEADS, MLA_V)
    wkv_p = jnp.concatenate([wk, wv], axis=-1).astype(BF16)
    assert w_out.shape[1] == 4 * 256
    pad_rows = lambda w, heads, dim: jnp.swapaxes(_pad_heads(jnp.swapaxes(w, -1, -2), heads, dim), -1, -2)
    wo_p = jnp.concatenate([w_out[:, 0:256], pad_rows(w_out[:, 256:512], SWA_HEADS, SWA_HEAD_DIM),
                            w_out[:, 512:768], pad_rows(w_out[:, 768:1024], MLA_HEADS, MLA_V)], axis=1).astype(BF16)
    return w_in_p, wqb_p, wkv_p, wo_p


def _rope_tables(length, dim, lo, ident, repeat=1):
    n_freq = dim // 4
    rows = length // GRID_W
    row = jnp.repeat(jnp.arange(rows, dtype=F32), GRID_W)
    col = jnp.tile(jnp.arange(GRID_W, dtype=F32), rows)
    inv = ROPE_BASE ** (-jnp.arange(n_freq, dtype=F32) / n_freq)
    ang = jnp.stack([row[:, None] * inv, col[:, None] * inv], axis=1)
    cos, sin = jnp.cos(ang), jnp.sin(ang)
    z = jnp.zeros_like(sin)
    cos_l = jnp.stack([cos, cos], axis=2).reshape(length, dim)
    sina = jnp.stack([-sin, z], axis=2).reshape(length, dim)
    sinb = jnp.stack([z, sin], axis=2).reshape(length, dim)
    if ident:
        cos_l, sina, sinb = jnp.ones_like(cos_l), jnp.zeros_like(sina), jnp.zeros_like(sinb)
    cos_l, sina, sinb = (jnp.tile(a, (1, repeat)) for a in (cos_l, sina, sinb))
    pad = lambda a, fill: jnp.pad(a, ((0, 0), (lo, LANE - lo - dim * repeat)), constant_values=fill)
    return jnp.stack([pad(cos_l, 1.0), pad(sina, 0.0), pad(sinb, 0.0)], axis=0)


def kernel(x, c, ctx, c_ctx, w_mod, b_mod, norm1_g, norm2_g, w_in, w_out, s5_lam_re, s5_lam_im, s5_log_dt,
           s5_b_re, s5_b_im, s5_c_re, s5_c_im, s5_d, s5_w_glu, s5_b_glu, swa_sink, hg_lb, hg_norm_g,
           mla_q_norm_g, mla_w_qb, mla_kv_norm_g, mla_w_kvb, ffn_w_up, ffn_w_down, final_norm_g):
    b, t, d = x.shape
    lc = ctx.shape[1]
    depth = w_mod.shape[0]

    cc = jnp.zeros((8, d), F32).at[:b].set(c).at[b].set(c_ctx)
    mods = _modulation(cc, w_mod, b_mod)

    lb_cum = jnp.cumsum(jax.nn.softmax(hg_lb.astype(F32), axis=1), axis=1)
    lb = lb_cum - lb_cum[:, :1]

    tab_swa = _rope_tables(t, SWA_HEAD_DIM, 0, False, repeat=2)
    tab_mla = _rope_tables(t, MLA_ROPE, MLA_NOPE, False)
    tab_swa_c = _rope_tables(lc, SWA_HEAD_DIM, 0, True, repeat=2)
    tab_mla_c = _rope_tables(lc, MLA_ROPE, MLA_NOPE, True)
    pavg = jnp.kron(jnp.eye(HG_HEADS, dtype=F32), jnp.full((HG_DV, HG_DV), 1.0 / HG_DV, F32)).astype(BF16)
    zero_st = jnp.zeros((b, HG_HEADS * HG_DV, HG_HEADS * HG_DK), F32)
    zero_s5 = jnp.zeros((b, S5_GROUPS, 4 * S5_STATE), F32)

    w_in_p, wqb_p, wkv_p, wo_p = _layer_weights(w_in, w_out, mla_w_qb, mla_w_kvb)
    mats = _s5_matrices(s5_lam_re, s5_lam_im, s5_log_dt, s5_b_re, s5_b_im, s5_c_re, s5_c_im, s5_d)
    wglu = s5_w_glu.astype(BF16)
    wup = ffn_w_up.astype(BF16)
    wdn = ffn_w_down.astype(BF16)
    fg = final_norm_g.reshape(1, d)

    for i in range(depth):
        need_ctx = i < depth - 1
        mod = mods[i, :b].reshape(b, 6, 1, d)
        mod_c = jnp.broadcast_to(mods[i, b].reshape(1, 6, 1, d), (b, 6, 1, d))
        sh1, sc1, g1, sh2, sc2, g2 = (mod[:, j] for j in range(6))
        csh1, csc1, cg1, csh2, csc2, cg2 = (mod_c[:, j] for j in range(6))
        n1 = norm1_g[i].reshape(1, d)
        qg = mla_q_norm_g[i].reshape(1, -1)
        kvg = mla_kv_norm_g[i].reshape(1, -1)
        px = _inproj(x, sc1, sh1, n1, w_in_p, tab_swa, tab_mla, qg, wqb_p, kvg, wkv_p, i)
        pc = _inproj(ctx, csc1, csh1, n1, w_in_p, tab_swa_c, tab_mla_c, qg, wqb_p, kvg, wkv_p, i)
        (xu, xsq, xsk, xsv, xhq, xhzf, xhzb, xhi, xhg, xmq, xmk, xmv) = px
        (cu, csq, csk, csv, chq, chzf, chzb, chi, chg, cmq, cmk, cmv) = pc

        ya_c, s5_fin = _s5_scan(cu, mats, zero_s5, i)
        ya, _ = _s5_scan(xu, mats, s5_fin, i)

        grp = SWA_HEADS // SWA_KV_HEADS
        sink = swa_sink[i].astype(F32) * LOG2E
        yb = _attention(xsq, csk, csv, xsk, xsv, grp=grp, sink=sink, window=SWA_WINDOW)
        lbf, lbb = lb[0, i], lb[1, i]
        gp = jnp.stack([jnp.log(lbf), jnp.log1p(-lbf), 1.0 - lbf,
                        jnp.log(lbb), jnp.log1p(-lbb), 1.0 - lbb, lbf, lbb], axis=0)
        of_c, ob_c, stf, stb = _hgrn(chq, chzf, chzb, chi, gp, zero_st, zero_st)
        of, ob, _, _ = _hgrn(xhq, xhzf, xhzb, xhi, gp, stf, stb)
        yd = _attention(xmq, cmk, cmv, xmk, xmv, grp=1, bq=1024, bk=2048)

        bglu = s5_b_glu[i].reshape(1, -1)
        hn = jnp.tile(hg_norm_g[i], HG_HEADS).reshape(1, -1)
        x = _outproj(x, g1, ya, yb, of, ob, xhg, yd, wglu, bglu, hn, pavg, wo_p, i)
        n2 = norm2_g[i].reshape(1, d)
        x = _ffn(x, sc2, sh2, g2, n2, wup, wdn, fg, not need_ctx, i)
        if need_ctx:
            yb_c = _attention(csq, csk, csv, grp=grp, sink=sink)
            yd_c = _attention(cmq, cmk, cmv, grp=1)
            ctx = _outproj(ctx, cg1, ya_c, yb_c, of_c, ob_c, chg, yd_c, wglu, bglu, hn, pavg, wo_p, i)
            ctx = _ffn(ctx, csc2, csh2, cg2, n2, wup, wdn, fg, False, i)
    return x
```

```python
import functools
import math

import jax
import jax.numpy as jnp
from jax import lax
from jax.experimental import pallas as pl
from jax.experimental.pallas import tpu as pltpu

F32 = jnp.float32
BF16 = jnp.bfloat16

EPS = 1e-6
NEG_INF = -1e30
ROPE_BASE = 10000.0
GRID_W = 64
LANE = 128
VMEM_LIMIT = 56 * 1024 * 1024

S5_CH, S5_GROUP, S5_STATE = 256, 16, 64
S5_GROUPS = S5_CH // S5_GROUP
S5_CHUNK = 16
SWA_HEADS, SWA_KV_HEADS, SWA_HEAD_DIM, SWA_WINDOW = 4, 2, 64, 128
HG_HEADS, HG_DK, HG_DV = 4, 64, 64
HG_CHUNK = 64
MLA_HEADS, MLA_Q_RANK, MLA_KV_RANK = 4, 256, 128
MLA_NOPE, MLA_ROPE, MLA_V = 64, 32, 64
MLA_SCALE = (MLA_NOPE + MLA_ROPE) ** -0.5
LOG2E = 1.4426950408889634
V_ONE = 64


def _tile(n, pref):
    t = min(n, pref)
    assert n % t == 0, (n, pref)
    return t


def _params(sem):
    return pltpu.CompilerParams(dimension_semantics=sem, vmem_limit_bytes=VMEM_LIMIT)


def _full(shape):
    nd = len(shape)
    return pl.BlockSpec(shape, lambda *_: (0,) * nd, pipeline_mode=pl.Buffered(1))


def _layer(w, li):
    nd = w.ndim - 1
    return pl.BlockSpec((None,) + w.shape[1:], lambda *_: (li,) + (0,) * nd, pipeline_mode=pl.Buffered(1))


def _mod_kernel(c_ref, w_ref, b_ref, o_ref):
    c = c_ref[...]
    s = c * (1.0 / (1.0 + jnp.exp(-c)))
    o_ref[...] = jnp.dot(s.astype(BF16), w_ref[...].astype(BF16), preferred_element_type=F32) + b_ref[...]


def _modulation(cc, w_mod, b_mod):
    depth, d, n = w_mod.shape
    tn = _tile(n, 1536)
    return pl.pallas_call(
        _mod_kernel,
        out_shape=jax.ShapeDtypeStruct((depth, 8, n), F32),
        grid=(depth, n // tn),
        in_specs=[pl.BlockSpec((8, d), lambda l, j: (0, 0)),
                  pl.BlockSpec((None, d, tn), lambda l, j: (l, 0, j)),
                  pl.BlockSpec((None, 1, tn), lambda l, j: (l, 0, j))],
        out_specs=pl.BlockSpec((None, 8, tn), lambda l, j: (l, 0, j)),
        compiler_params=_params(("parallel", "parallel")),
        name="modulation",
    )(cc, w_mod, b_mod.reshape(depth, 1, n))


def _rope_block(x, t_ref, half):
    return (x * t_ref[0] + pltpu.roll(x, LANE - half, 1) * t_ref[1]
            + pltpu.roll(x, half, 1) * t_ref[2])


def _rms(x, g):
    return x * lax.rsqrt(jnp.mean(x * x, axis=-1, keepdims=True) + EPS) * g


_C_U = 0
_C_SQ = 256
_C_SKV = 512
_C_HG = 768
_C_CQ = _C_HG + 5 * 256
_C_CKV = _C_CQ + 256
_N_INP = _C_CKV + 256


def _inproj_kernel(x_ref, sc_ref, sh_ref, g_ref, w_ref, ts_ref, tm_ref, qg_ref, wqb_ref, kvg_ref, wkv_ref,
                   u_ref, sq_ref, sk_ref, sv_ref, hq_ref, hzf_ref, hzb_ref, hi_ref, hg_ref,
                   mq_ref, mk_ref, mv_ref, us_ref):
    h = _rms(x_ref[...], g_ref[...]) * (1.0 + sc_ref[...]) + sh_ref[...]
    hb = h.astype(BF16)

    def proj(lo, n):
        return jnp.dot(hb, w_ref[:, lo:lo + n], preferred_element_type=F32)

    cq = _rms(proj(_C_CQ, 256), qg_ref[...]).astype(BF16)
    ckv_kr = proj(_C_CKV, 256)
    ckv = _rms(ckv_kr[:, :LANE], kvg_ref[...]).astype(BF16)
    kr = pltpu.roll(ckv_kr[:, LANE:], MLA_NOPE, 1)

    u = proj(_C_U, 256)
    for v in range(2):
        us_ref[v] = u[:, v * LANE:(v + 1) * LANE]
    for j in range(S5_CHUNK):
        for v in range(2):
            u_ref[j, :, v * LANE:(v + 1) * LANE] = us_ref[v, pl.ds(j, us_ref.shape[1] // S5_CHUNK, stride=S5_CHUNK), :]
    lane = lax.broadcasted_iota(jnp.int32, (1, LANE), 1)
    low = lane < SWA_HEAD_DIM
    one = jnp.where(lane == V_ONE, 1.0, 0.0)

    def spread(pair, fill):
        return [jnp.where(low, blk, fill).astype(BF16) for blk in (pair, pltpu.roll(pair, SWA_HEAD_DIM, 1))]

    sq = proj(_C_SQ, 256)
    for pp in range(SWA_HEADS // 2):
        heads = spread(_rope_block(sq[:, pp * LANE:(pp + 1) * LANE], ts_ref, 16), 0.0)
        for e in range(2):
            sq_ref[:, (2 * pp + e) * LANE:(2 * pp + e + 1) * LANE] = heads[e]
    skv = proj(_C_SKV, 256)
    for e, blk in enumerate(spread(_rope_block(skv[:, :LANE], ts_ref, 16), 0.0)):
        sk_ref[:, e * LANE:(e + 1) * LANE] = blk
    for e, blk in enumerate(spread(skv[:, LANE:], one)):
        sv_ref[:, e * LANE:(e + 1) * LANE] = blk

    q = jnp.dot(cq, wqb_ref[...], preferred_element_type=F32)
    for hh in range(MLA_HEADS):
        sl = slice(hh * LANE, (hh + 1) * LANE)
        mq_ref[:, sl] = _rope_block(q[:, sl], tm_ref, 8).astype(BF16)
    for i, r in enumerate((hq_ref, hzf_ref, hzb_ref)):
        r[...] = proj(_C_HG + i * 256, 256)
    kv = jnp.dot(ckv, wkv_ref[...], preferred_element_type=F32)
    for hh in range(MLA_HEADS):
        sl = slice(hh * LANE, (hh + 1) * LANE)
        mk_ref[:, sl] = _rope_block(kv[:, sl] + kr, tm_ref, 8).astype(BF16)
        mv_ref[:, sl] = jnp.where(lane == V_ONE, 1.0, kv[:, MLA_HEADS * LANE + hh * LANE:][:, :LANE]).astype(BF16)
    for i, r in enumerate((hi_ref, hg_ref)):
        r[...] = proj(_C_HG + (3 + i) * 256, 256)


def _inproj(x, sc, sh, g, w_in_p, tab_swa, tab_mla, qg, wqb_p, kvg, wkv_p, li):
    b, t, d = x.shape
    tt = _tile(t, 1024)
    row = lambda n, dt: jax.ShapeDtypeStruct((b, t, n), dt)
    out_shape = (jax.ShapeDtypeStruct((b, S5_CHUNK, t // S5_CHUNK, 256), F32),
                 row(512, BF16), row(256, BF16), row(256, BF16),
                 row(256, F32), row(256, F32), row(256, F32), row(256, F32), row(256, F32),
                 row(512, BF16), row(512, BF16), row(512, BF16))
    xs = lambda n: pl.BlockSpec((None, tt, n), lambda i, bb: (bb, i, 0))
    vec = pl.BlockSpec((None, 1, d), lambda i, bb: (bb, 0, 0))
    tab = pl.BlockSpec((3, tt, LANE), lambda i, bb: (0, i, 0))
    return pl.pallas_call(
        _inproj_kernel,
        out_shape=out_shape,
        grid=(t // tt, b),
        in_specs=[xs(d), vec, vec, _full((1, d)), _layer(w_in_p, li), tab, tab,
                  _full((1, 256)), _layer(wqb_p, li), _full((1, 128)), _layer(wkv_p, li)],
        out_specs=(pl.BlockSpec((None, S5_CHUNK, tt // S5_CHUNK, 256), lambda i, bb: (bb, 0, i, 0)),)
        + tuple(xs(s.shape[-1]) for s in out_shape[1:]),
        scratch_shapes=[pltpu.VMEM((2, tt, LANE), F32)],
        compiler_params=_params(("parallel", "parallel")),
        name="inproj",
    )(x, sc, sh, g, w_in_p, tab_swa, tab_mla, qg, wqb_p, kvg, wkv_p)


def _s5_kernel(u_ref, mt_ref, ft_ref, et_ref, a_ref, s0_ref, y_ref, sfin_ref, zs_ref, ut_ref, yt_ref, *, nc):
    ng, half = S5_GROUPS, LANE // S5_GROUP
    for j in range(S5_CHUNK):
        for v in range(2):
            t = u_ref[j, :, v * LANE:(v + 1) * LANE].T
            for gl in range(half):
                ut_ref[half * v + gl, S5_GROUP * j:S5_GROUP * (j + 1), :] = (
                    t[S5_GROUP * gl:S5_GROUP * (gl + 1), :].astype(BF16))
    for g in range(ng):
        z = jnp.dot(ft_ref[g], ut_ref[g], preferred_element_type=F32).T
        for c in range(2):
            zs_ref[c, pl.ds(g, nc, stride=ng), :] = z[:, c * LANE:(c + 1) * LANE]

    fmask = lax.broadcasted_iota(jnp.int32, (1, LANE), 1) < S5_STATE
    ar, ai = a_ref[0], a_ref[1]

    def step(i, s):
        rf = pl.multiple_of(i * ng, ng)
        rb = pl.multiple_of((nc - 1 - i) * ng, ng)
        zre, zim = (jnp.where(fmask, zs_ref[c, pl.ds(rf, ng), :], zs_ref[c, pl.ds(rb, ng), :]) for c in range(2))
        for c in range(2):
            zs_ref[c, pl.ds(rf, ng), 0:S5_STATE] = s[c][:, 0:S5_STATE]
            zs_ref[c, pl.ds(rb, ng), S5_STATE:LANE] = s[c][:, S5_STATE:LANE]
        re, im = s
        return ar * re - ai * im + zre, ar * im + ai * re + zim

    re, im = lax.fori_loop(0, nc, step, (s0_ref[:, :LANE], s0_ref[:, LANE:]), unroll=4)
    sfin_ref[:, :LANE] = re
    sfin_ref[:, LANE:] = im

    for g in range(ng):
        ss = jnp.concatenate([zs_ref[c, pl.ds(g, nc, stride=ng), :] for c in range(2)], axis=1)
        yt = (jnp.dot(mt_ref[g], ut_ref[g], preferred_element_type=F32)
              + jnp.dot(et_ref[g], ss.T.astype(BF16), preferred_element_type=F32))
        for t in range(S5_CHUNK):
            yt_ref[t, S5_GROUP * g:S5_GROUP * (g + 1), :] = yt[S5_GROUP * t:S5_GROUP * (t + 1), :]
    for t in range(S5_CHUNK):
        y_ref[t] = yt_ref[t].T


def _s5_scan(uj, mats, s0, li):
    mt, ft, et, a16 = mats
    b, _, nc, _ = uj.shape
    slab = pl.BlockSpec((None, S5_CHUNK, nc, 256), lambda bb: (bb, 0, 0, 0), pipeline_mode=pl.Buffered(1))
    st = pl.BlockSpec((None, S5_GROUPS, 256), lambda bb: (bb, 0, 0))
    return pl.pallas_call(
        functools.partial(_s5_kernel, nc=nc),
        out_shape=(jax.ShapeDtypeStruct(uj.shape, F32), jax.ShapeDtypeStruct((b, S5_GROUPS, 256), F32)),
        grid=(b,),
        in_specs=[slab, _layer(mt, li), _layer(ft, li), _layer(et, li), _layer(a16, li), st],
        out_specs=(slab, st),
        scratch_shapes=[pltpu.VMEM((2, S5_GROUPS * nc, LANE), F32), pltpu.VMEM((S5_GROUPS, 256, nc), BF16),
                        pltpu.VMEM((S5_CHUNK, 256, nc), F32)],
        compiler_params=_params(("parallel",)),
        name="s5",
    )(uj, mt, ft, et, a16, s0)


def _s5_matrices(lam_re, lam_im, log_dt, b_re, b_im, c_re, c_im, d_skip):
    L = S5_CHUNK
    depth = lam_re.shape[0]
    hp = lax.Precision.HIGHEST
    lam = lax.complex(lam_re, lam_im)
    ldt = lam * jnp.exp(log_dt)
    bb = ((jnp.exp(ldt) - 1.0) / lam)[..., None] * lax.complex(b_re, b_im)
    cc = lax.complex(c_re, c_im)
    k = jnp.arange(L + 1, dtype=F32)
    apow = jnp.exp(ldt[:, None] * k[None, :, None, None, None])
    kern = jnp.real(jnp.einsum('ndghp,nkdgp,ndgpi->ndkgih', cc, apow[:, :L], bb, precision=hp))
    idx = jnp.arange(L)
    lag = idx[None, :] - idx[:, None]
    oh_f = (lag[:, :, None] == idx[None, None, :]).astype(F32)
    oh_b = (-lag[:, :, None] == idx[None, None, :]).astype(F32)
    eye = jnp.eye(L, dtype=F32)
    dg = d_skip.reshape(depth, S5_GROUPS, S5_GROUP)
    m = (jnp.einsum('jtk,nkgih->ngjith', oh_f, kern[:, 0], precision=hp)
         + jnp.einsum('jtk,nkgih->ngjith', oh_b, kern[:, 1], precision=hp)
         + jnp.einsum('jt,ih,ngh->ngjith', eye, jnp.eye(S5_GROUP, dtype=F32), dg, precision=hp))
    m = m.reshape(depth, S5_GROUPS, L * S5_GROUP, L * S5_GROUP)
    zf = apow[:, L - 1 - idx, 0][..., None] * bb[:, 0][:, None]
    zb = apow[:, idx, 1][..., None] * bb[:, 1][:, None]
    fmat = jnp.concatenate([jnp.real(zf), jnp.real(zb), jnp.imag(zf), jnp.imag(zb)], axis=3)
    fmat = fmat.transpose(0, 2, 1, 4, 3).reshape(depth, S5_GROUPS, L * S5_GROUP, 4 * S5_STATE)
    wf = cc[:, 0][:, None] * apow[:, 1 + idx, 0][:, :, :, None, :]
    wb = cc[:, 1][:, None] * apow[:, L - idx, 1][:, :, :, None, :]
    emat = jnp.concatenate([jnp.real(wf), jnp.real(wb), -jnp.imag(wf), -jnp.imag(wb)], axis=4)
    emat = emat.transpose(0, 2, 4, 1, 3).reshape(depth, S5_GROUPS, 4 * S5_STATE, L * S5_GROUP)
    al = apow[:, L]
    a16 = jnp.stack([jnp.concatenate([jnp.real(al[:, 0]), jnp.real(al[:, 1])], axis=-1),
                     jnp.concatenate([jnp.imag(al[:, 0]), jnp.imag(al[:, 1])], axis=-1)], axis=1)
    tr = lambda w: jnp.swapaxes(w, -1, -2).astype(BF16)
    return tr(m), tr(fmat), tr(emat), a16.astype(F32)


def _attn_kernel(*refs, grp, bq, sub, has_sink, has_seq, window, t_seq, bk):
    refs = list(refs)
    sink_ref = refs.pop(0) if has_sink else None
    q_ref, kc_ref, vc_ref = refs[:3]
    ks_ref, vs_ref = (refs[3], refs[4]) if has_seq else (None, None)
    o_ref = refs[-1]
    hk = pl.program_id(1)
    qi = pl.program_id(2)

    def scores(q, k, mask=None):
        s = lax.dot_general(q, k, (((1,), (1,)), ((), ())), preferred_element_type=F32)
        return s if mask is None else jnp.where(mask, s, NEG_INF)

    band_cache = {}

    def col_minus_row(rows, wlen, nq):
        if (rows, wlen, nq) not in band_cache:
            band_cache[rows, wlen, nq] = (lax.broadcasted_iota(jnp.int32, (rows, wlen), 1)
                                          - lax.broadcasted_iota(jnp.int32, (rows, wlen), 0) % nq)
        return band_cache[rows, wlen, nq]

    def logits(r0, nq):
        rows = grp * nq
        q = jnp.concatenate([q_ref[r0:r0 + nq, g * LANE:(g + 1) * LANE] for g in range(grp)], axis=0)
        s_parts, v_parts = [scores(q, kc_ref[...])], [vc_ref[...]]
        if has_seq and window is not None:
            wlen = nq + 2 * window
            q0 = qi * bq + r0
            start = pl.multiple_of(jnp.clip(q0 - window, 0, t_seq - wlen), LANE)
            in_band = jnp.abs(col_minus_row(rows, wlen, nq) + (start - q0)) <= window
            s_parts.append(scores(q, ks_ref[pl.ds(start, wlen), :], in_band))
            v_parts.append(vs_ref[pl.ds(start, wlen), :])
        elif has_seq:
            s_parts.append(scores(q, ks_ref[0:bk, :]))
            v_parts.append(vs_ref[0:bk, :])
        return q, s_parts, v_parts

    def softmax_pv(nq, s_parts, v_parts):
        m = functools.reduce(jnp.maximum, [jnp.max(s, axis=-1, keepdims=True) for s in s_parts])
        if has_sink:
            sink = jnp.concatenate([jnp.full((nq, 1), sink_ref[hk * grp + g], F32) for g in range(grp)], axis=0)
            m = jnp.maximum(m, sink)
        acc = sum(jnp.dot(jnp.exp2(s - m).astype(BF16), v, preferred_element_type=F32)
                  for s, v in zip(s_parts, v_parts))
        if has_sink:
            one_lane = lax.broadcasted_iota(jnp.int32, (1, LANE), 1) == V_ONE
            acc = acc + jnp.where(one_lane, jnp.exp2(sink - m), 0.0)
        return m, acc

    def finish(r0, nq, acc):
        o = acc * (1.0 / acc[:, V_ONE:V_ONE + 1])
        for g in range(grp):
            o_ref[r0:r0 + nq, g * LANE:(g + 1) * LANE] = o[g * nq:(g + 1) * nq].astype(o_ref.dtype)

    if has_seq and window is not None:
        for r0 in range(0, bq, sub):
            _, s_parts, v_parts = logits(r0, sub)
            finish(r0, sub, softmax_pv(sub, s_parts, v_parts)[1])
        return
    q, s_parts, v_parts = logits(0, bq)
    m, acc = softmax_pv(bq, s_parts, v_parts)
    if has_seq:
        def body(j, carry):
            m, acc = carry
            st = pl.multiple_of(j * bk, bk)
            s = scores(q, ks_ref[pl.ds(st, bk), :])
            m_new = jnp.maximum(m, jnp.max(s, axis=-1, keepdims=True))
            p = jnp.exp2(s - m_new).astype(BF16)
            acc = jnp.exp2(m - m_new) * acc + jnp.dot(p, vs_ref[pl.ds(st, bk), :], preferred_element_type=F32)
            return m_new, acc
        m, acc = lax.fori_loop(1, t_seq // bk, body, (m, acc), unroll=True)
    finish(0, bq, acc)


def _attention(q, kc, vc, ks=None, vs=None, *, grp, sink=None, window=None, bq=512, bk=1024):
    b, tq, wq = q.shape
    hkv = wq // (grp * LANE)
    lc = kc.shape[1]
    has_seq = ks is not None
    t_seq = ks.shape[1] if has_seq else 0
    bq = _tile(tq, bq)
    sub = min(bq, LANE)
    if has_seq and window is None:
        bk = _tile(t_seq, bk)
    kern = functools.partial(_attn_kernel, grp=grp, bq=bq, sub=sub, has_sink=sink is not None, has_seq=has_seq,
                             window=window, t_seq=t_seq, bk=bk)
    in_specs, args = [], []
    if sink is not None:
        in_specs.append(pl.BlockSpec(memory_space=pltpu.SMEM))
        args.append(sink)
    in_specs.append(pl.BlockSpec((None, bq, grp * LANE), lambda bb, h, i: (bb, i, h)))
    args.append(q)
    ctx_spec = pl.BlockSpec((None, lc, LANE), lambda bb, h, i: (bb, 0, h))
    in_specs += [ctx_spec, ctx_spec]
    args += [kc, vc]
    if has_seq:
        seq_spec = pl.BlockSpec((None, t_seq, LANE), lambda bb, h, i: (bb, 0, h))
        in_specs += [seq_spec, seq_spec]
        args += [ks, vs]
    return pl.pallas_call(
        kern,
        out_shape=jax.ShapeDtypeStruct((b, tq, wq), BF16),
        grid=(b, hkv, tq // bq),
        in_specs=in_specs,
        out_specs=pl.BlockSpec((None, bq, grp * LANE), lambda bb, h, i: (bb, i, h)),
        compiler_params=_params(("parallel", "parallel", "parallel")),
        name="attention",
    )(*args)


def _split3(x):
    hi = x.astype(BF16)
    r = x - hi.astype(F32)
    mid = r.astype(BF16)
    lo = (r - mid.astype(F32)).astype(BF16)
    return hi, mid, lo


def _hg_kernel(gp_ref, qf_ref, zf_ref, vf_ref, qb_ref, zb_ref, vb_ref, s0f_ref, s0b_ref,
               of_ref, ob_ref, sf_ref, sb_ref, st_ref, *, nchunk):
    i = pl.program_id(1)
    c = HG_CHUNK
    w = HG_HEADS * HG_DK

    @pl.when(i == 0)
    def _():
        st_ref[0] = s0f_ref[...]
        st_ref[1] = s0b_ref[...]

    ri = lax.broadcasted_iota(jnp.int32, (c, c), 0)
    ci = lax.broadcasted_iota(jnp.int32, (c, c), 1)
    ti = lax.broadcasted_iota(jnp.int32, (c, HG_HEADS * c), 0)
    si = lax.broadcasted_iota(jnp.int32, (c, HG_HEADS * c), 1) % c
    lane = lax.broadcasted_iota(jnp.int32, (1, w), 1)
    hmask = [(lane // HG_DK) == h for h in range(HG_HEADS)]
    r2 = lax.broadcasted_iota(jnp.int32, (w, w), 0) // HG_DV
    c2 = lax.broadcasted_iota(jnp.int32, (w, w), 1) // HG_DK
    diag = r2 == c2
    dirs = {
        True: dict(q=qf_ref, z=zf_ref, v=vf_ref, o=of_ref, gp=tuple(gp_ref[r:r + 1, :] for r in range(3)),
                   tri=jnp.where(ci <= ri, 1.0, 0.0).astype(BF16), causal=si <= ti),
        False: dict(q=qb_ref, z=zb_ref, v=vb_ref, o=ob_ref, gp=tuple(gp_ref[r:r + 1, :] for r in range(3, 6)),
                    tri=jnp.where(ci >= ri, 1.0, 0.0).astype(BF16), causal=si >= ti),
    }
    units = []
    for n in range(nchunk):
        units += [(True, n), (False, nchunk - 1 - n)]

    ph1 = []
    for fwd, n in units:
        dd = dirs[fwd]
        sl = slice(n * c, (n + 1) * c)
        z = dd['z'][sl, :]
        log_lb, l1p, om = dd['gp']
        ls = jnp.minimum(z, 0.0) - jnp.log(1.0 + jnp.exp(-jnp.abs(z)))
        bterm = l1p + ls
        lf = jnp.maximum(log_lb, bterm) + jnp.log(1.0 + jnp.exp(-jnp.abs(log_lb - bterm)))
        k = om / (1.0 + jnp.exp(z))
        gsum = sum(jnp.dot(dd['tri'], part, preferred_element_type=F32) for part in _split3(lf))
        ph1.append((k, gsum))

    ph2 = []
    for (fwd, n), (k, gsum) in zip(units, ph1):
        dd = dirs[fwd]
        sl = slice(n * c, (n + 1) * c)
        q, v = dd['q'][sl, :], dd['v'][sl, :]
        tot = gsum[c - 1:c, :] if fwd else gsum[0:1, :]
        gm = gsum[c // 2:c // 2 + 1, :]
        qd = (q * jnp.exp(gsum - gm)).astype(BF16)
        kinv = k * jnp.exp(gm - gsum)
        q_in = (q * jnp.exp(gsum)).astype(BF16)
        k_end = (k * jnp.exp(tot - gsum)).astype(BF16)
        kstack = jnp.concatenate([jnp.where(hm, kinv, 0.0) for hm in hmask], axis=0).astype(BF16)
        vstack = jnp.concatenate([jnp.where(hm, v, 0.0) for hm in hmask], axis=0).astype(BF16)
        ph2.append((qd, kstack, vstack, q_in, k_end, v.astype(BF16), jnp.exp(tot)))

    ph3 = []
    for (fwd, n), (qd, kstack, vstack, q_in, k_end, vb, dec) in zip(units, ph2):
        att = lax.dot_general(qd, kstack, (((1,), (1,)), ((), ())), preferred_element_type=F32)
        att = jnp.where(dirs[fwd]['causal'], att, 0.0).astype(BF16)
        o_intra = jnp.dot(att, vstack, preferred_element_type=F32)
        kv_t = lax.dot_general(vb, k_end, (((0,), (0,)), ((), ())), preferred_element_type=F32)
        ph3.append((o_intra, jnp.where(diag, kv_t, 0.0)))

    st = {True: st_ref[0], False: st_ref[1]}
    for (fwd, n), (_, _, _, q_in, _, _, dec), (o_intra, kv_t) in zip(units, ph2, ph3):
        sl = slice(n * c, (n + 1) * c)
        o_inter = lax.dot_general(q_in, st[fwd].astype(BF16), (((1,), (1,)), ((), ())), preferred_element_type=F32)
        dirs[fwd]['o'][sl, :] = o_intra + o_inter
        st[fwd] = st[fwd] * dec + kv_t
    st_ref[0] = st[True]
    st_ref[1] = st[False]

    @pl.when(i == pl.num_programs(1) - 1)
    def _():
        sf_ref[...] = st[True]
        sb_ref[...] = st[False]


def _hgrn(q, zf, zb, v, gp, s0f, s0b):
    b, t, w = q.shape
    tb = _tile(t, 1024)
    nblk = t // tb
    fw = pl.BlockSpec((None, tb, w), lambda bb, i: (bb, i, 0))
    bw = pl.BlockSpec((None, tb, w), lambda bb, i: (bb, nblk - 1 - i, 0))
    st = pl.BlockSpec((None, w, w), lambda bb, i: (bb, 0, 0))
    return pl.pallas_call(
        functools.partial(_hg_kernel, nchunk=tb // HG_CHUNK),
        out_shape=(jax.ShapeDtypeStruct((b, t, w), F32), jax.ShapeDtypeStruct((b, t, w), F32),
                   jax.ShapeDtypeStruct((b, w, w), F32), jax.ShapeDtypeStruct((b, w, w), F32)),
        grid=(b, nblk),
        in_specs=[_full((8, w)), fw, fw, fw, bw, bw, bw, st, st],
        out_specs=(fw, bw, st, st),
        scratch_shapes=[pltpu.VMEM((2, w, w), F32)],
        compiler_params=_params(("parallel", "arbitrary")),
        name="hgrn2",
    )(gp, q, zf, v, q, zb, v, s0f, s0b)


def _gelu_tanh(x):
    return 0.5 * x * (1.0 + jnp.tanh(math.sqrt(2.0 / math.pi) * (x + 0.044715 * (x * x * x))))


def _outproj_kernel(x_ref, g1_ref, ya_ref, yb_ref, of_ref, ob_ref, hg_ref, yd_ref,
                    wglu_ref, bglu_ref, hn_ref, pavg_ref, wo_ref, o_ref, ys_ref):
    for j in range(S5_CHUNK):
        for v in range(2):
            ys_ref[v, pl.ds(j, ys_ref.shape[1] // S5_CHUNK, stride=S5_CHUNK), :] = ya_ref[j, :, v * LANE:(v + 1) * LANE]
    ya = _gelu_tanh(jnp.concatenate([ys_ref[0], ys_ref[1]], axis=1))
    gl = jnp.dot(ya.astype(BF16), wglu_ref[...], preferred_element_type=F32) + bglu_ref[...]
    o = of_ref[...] + ob_ref[...]
    o2 = o * o
    hi = o2.astype(BF16)
    lo = (o2 - hi.astype(F32)).astype(BF16)
    ms = (jnp.dot(hi, pavg_ref[...], preferred_element_type=F32)
          + jnp.dot(lo, pavg_ref[...], preferred_element_type=F32))
    y = (jnp.dot(yb_ref[...], wo_ref[256:768, :], preferred_element_type=F32)
         + jnp.dot(yd_ref[...], wo_ref[1024:1536, :], preferred_element_type=F32))
    ya = ya * (1.0 / (1.0 + jnp.exp(-gl)))
    gate = hg_ref[...]
    yc = o * lax.rsqrt(ms + EPS) * hn_ref[...] * (gate * (1.0 / (1.0 + jnp.exp(-gate))))
    y = (y + jnp.dot(ya.astype(BF16), wo_ref[0:256, :], preferred_element_type=F32)
         + jnp.dot(yc.astype(BF16), wo_ref[768:1024, :], preferred_element_type=F32))
    o_ref[...] = x_ref[...] + g1_ref[...] * y


def _outproj(x, g1, ya, yb, of, ob, hg, yd, wglu, bglu, hn, pavg, wo_p, li):
    b, t, d = x.shape
    tt = _tile(t, 512)
    xs = lambda n: pl.BlockSpec((None, tt, n), lambda bb, i: (bb, i, 0))
    vec = pl.BlockSpec((None, 1, d), lambda bb, i: (bb, 0, 0))
    return pl.pallas_call(
        _outproj_kernel,
        out_shape=jax.ShapeDtypeStruct((b, t, d), F32),
        grid=(b, t // tt),
        in_specs=[xs(d), vec, pl.BlockSpec((None, S5_CHUNK, tt // S5_CHUNK, 256), lambda bb, i: (bb, 0, i, 0)),
                  xs(512), xs(256), xs(256), xs(256), xs(512),
                  _layer(wglu, li), _full((1, 256)), _full((1, 256)), _full(pavg.shape), _layer(wo_p, li)],
        out_specs=xs(d),
        scratch_shapes=[pltpu.VMEM((2, tt, LANE), F32)],
        compiler_params=_params(("parallel", "parallel")),
        name="outproj",
    )(x, g1, ya, yb, of, ob, hg, yd, wglu, bglu, hn, pavg, wo_p)


def _ffn_kernel(x_ref, sc_ref, sh_ref, g2_ref, ng_ref, wup_ref, wdn_ref, fg_ref, o_ref, acc_ref, *, hidden, ck, final):
    x = x_ref[...]
    hb = (_rms(x, ng_ref[...]) * (1.0 + sc_ref[...]) + sh_ref[...]).astype(BF16)
    for j in range(hidden // ck):
        gate = jnp.dot(hb, wup_ref[:, j * ck:(j + 1) * ck], preferred_element_type=F32)
        up = jnp.dot(hb, wup_ref[:, hidden + j * ck:hidden + (j + 1) * ck], preferred_element_type=F32)
        a = (gate * (1.0 / (1.0 + jnp.exp(-gate))) * up).astype(BF16)
        part = jnp.dot(a, wdn_ref[j * ck:(j + 1) * ck, :], preferred_element_type=F32)
        if j == 0:
            acc_ref[...] = part
        else:
            acc_ref[...] += part
    y = x + g2_ref[...] * acc_ref[...]
    if final:
        y = _rms(y, fg_ref[...])
    o_ref[...] = y


def _ffn(x, sc, sh, g2, ng, wup, wdn, fg, final, li):
    b, t, d = x.shape
    hidden = wdn.shape[1]
    tt = _tile(t, 512)
    xs = pl.BlockSpec((None, tt, d), lambda bb, i: (bb, i, 0))
    vec = pl.BlockSpec((None, 1, d), lambda bb, i: (bb, 0, 0))
    return pl.pallas_call(
        functools.partial(_ffn_kernel, hidden=hidden, ck=_tile(hidden, 256), final=final),
        out_shape=jax.ShapeDtypeStruct((b, t, d), F32),
        grid=(b, t // tt),
        in_specs=[xs, vec, vec, vec, _full((1, d)), _layer(wup, li), _layer(wdn, li), _full((1, d))],
        out_specs=xs,
        scratch_shapes=[pltpu.VMEM((tt, d), F32)],
        compiler_params=_params(("parallel", "parallel")),
        name="ffn",
    )(x, sc, sh, g2, ng, wup, wdn, fg)


def _pad_heads(w, heads, dim):
    w = w.reshape(w.shape[:-1] + (heads, dim))
    w = jnp.pad(w, [(0, 0)] * (w.ndim - 1) + [(0, LANE - dim)])
    return w.reshape(w.shape[:-2] + (heads * LANE,))


def _layer_weights(w_in, w_out, mla_w_qb, mla_w_kvb):
    depth, d, n_in = w_in.shape
    qscale = jnp.ones((n_in,), F32).at[_C_SQ:_C_SQ + SWA_HEADS * SWA_HEAD_DIM].set(SWA_HEAD_DIM ** -0.5 * LOG2E)
    w_in_p = jnp.pad(w_in * qscale, ((0, 0), (0, 0), (0, _N_INP - n_in))).astype(BF16)
    wqb_p = _pad_heads(mla_w_qb * (MLA_SCALE * LOG2E), MLA_HEADS, MLA_NOPE + MLA_ROPE).astype(BF16)
    kvb = mla_w_kvb.reshape(depth, MLA_KV_RANK, MLA_HEADS, MLA_NOPE + MLA_V)
    wk = _pad_heads(kvb[..., :MLA_NOPE].reshape(depth, MLA_KV_RANK, -1), MLA_HEADS, MLA_NOPE)
    wv = _pad_heads(kvb[..., MLA_NOPE:].reshape(depth, MLA_KV_RANK, -1), MLA_HEADS, MLA_V)
    wkv_p = jnp.concatenate([wk, wv], axis=-1).astype(BF16)
    assert w_out.shape[1] == 4 * 256
    pad_rows = lambda w, heads, dim: jnp.swapaxes(_pad_heads(jnp.swapaxes(w, -1, -2), heads, dim), -1, -2)
    wo_p = jnp.concatenate([w_out[:, 0:256], pad_rows(w_out[:, 256:512], SWA_HEADS, SWA_HEAD_DIM),
                            w_out[:, 512:768], pad_rows(w_out[:, 768:1024], MLA_HEADS, MLA_V)], axis=1).astype(BF16)
    return w_in_p, wqb_p, wkv_p, wo_p


def _rope_tables(length, dim, lo, ident, repeat=1):
    n_freq = dim // 4
    rows = length // GRID_W
    row = jnp.repeat(jnp.arange(rows, dtype=F32), GRID_W)
    col = jnp.tile(jnp.arange(GRID_W, dtype=F32), rows)
    inv = ROPE_BASE ** (-jnp.arange(n_freq, dtype=F32) / n_freq)
    ang = jnp.stack([row[:, None] * inv, col[:, None] * inv], axis=1)
    cos, sin = jnp.cos(ang), jnp.sin(ang)
    z = jnp.zeros_like(sin)
    cos_l = jnp.stack([cos, cos], axis=2).reshape(length, dim)
    sina = jnp.stack([-sin, z], axis=2).reshape(length, dim)
    sinb = jnp.stack([z, sin], axis=2).reshape(length, dim)
    if ident:
        cos_l, sina, sinb = jnp.ones_like(cos_l), jnp.zeros_like(sina), jnp.zeros_like(sinb)
    cos_l, sina, sinb = (jnp.tile(a, (1, repeat)) for a in (cos_l, sina, sinb))
    pad = lambda a, fill: jnp.pad(a, ((0, 0), (lo, LANE - lo - dim * repeat)), constant_values=fill)
    return jnp.stack([pad(cos_l, 1.0), pad(sina, 0.0), pad(sinb, 0.0)], axis=0)


def kernel(x, c, ctx, c_ctx, w_mod, b_mod, norm1_g, norm2_g, w_in, w_out, s5_lam_re, s5_lam_im, s5_log_dt,
           s5_b_re, s5_b_im, s5_c_re, s5_c_im, s5_d, s5_w_glu, s5_b_glu, swa_sink, hg_lb, hg_norm_g,
           mla_q_norm_g, mla_w_qb, mla_kv_norm_g, mla_w_kvb, ffn_w_up, ffn_w_down, final_norm_g):
    b, t, d = x.shape
    lc = ctx.shape[1]
    depth = w_mod.shape[0]

    cc = jnp.zeros((8, d), F32).at[:b].set(c).at[b].set(c_ctx)
    mods = _modulation(cc, w_mod, b_mod)

    lb_cum = jnp.cumsum(jax.nn.softmax(hg_lb.astype(F32), axis=1), axis=1)
    lb = lb_cum - lb_cum[:, :1]

    tab_swa = _rope_tables(t, SWA_HEAD_DIM, 0, False, repeat=2)
    tab_mla = _rope_tables(t, MLA_ROPE, MLA_NOPE, False)
    tab_swa_c = _rope_tables(lc, SWA_HEAD_DIM, 0, True, repeat=2)
    tab_mla_c = _rope_tables(lc, MLA_ROPE, MLA_NOPE, True)
    pavg = jnp.kron(jnp.eye(HG_HEADS, dtype=F32), jnp.full((HG_DV, HG_DV), 1.0 / HG_DV, F32)).astype(BF16)
    zero_st = jnp.zeros((b, HG_HEADS * HG_DV, HG_HEADS * HG_DK), F32)
    zero_s5 = jnp.zeros((b, S5_GROUPS, 4 * S5_STATE), F32)

    w_in_p, wqb_p, wkv_p, wo_p = _layer_weights(w_in, w_out, mla_w_qb, mla_w_kvb)
    mats = _s5_matrices(s5_lam_re, s5_lam_im, s5_log_dt, s5_b_re, s5_b_im, s5_c_re, s5_c_im, s5_d)
    wglu = s5_w_glu.astype(BF16)
    wup = ffn_w_up.astype(BF16)
    wdn = ffn_w_down.astype(BF16)
    fg = final_norm_g.reshape(1, d)

    for i in range(depth):
        need_ctx = i < depth - 1
        mod = mods[i, :b].reshape(b, 6, 1, d)
        mod_c = jnp.broadcast_to(mods[i, b].reshape(1, 6, 1, d), (b, 6, 1, d))
        sh1, sc1, g1, sh2, sc2, g2 = (mod[:, j] for j in range(6))
        csh1, csc1, cg1, csh2, csc2, cg2 = (mod_c[:, j] for j in range(6))
        n1 = norm1_g[i].reshape(1, d)
        qg = mla_q_norm_g[i].reshape(1, -1)
        kvg = mla_kv_norm_g[i].reshape(1, -1)
        px = _inproj(x, sc1, sh1, n1, w_in_p, tab_swa, tab_mla, qg, wqb_p, kvg, wkv_p, i)
        pc = _inproj(ctx, csc1, csh1, n1, w_in_p, tab_swa_c, tab_mla_c, qg, wqb_p, kvg, wkv_p, i)
        (xu, xsq, xsk, xsv, xhq, xhzf, xhzb, xhi, xhg, xmq, xmk, xmv) = px
        (cu, csq, csk, csv, chq, chzf, chzb, chi, chg, cmq, cmk, cmv) = pc

        ya_c, s5_fin = _s5_scan(cu, mats, zero_s5, i)
        ya, _ = _s5_scan(xu, mats, s5_fin, i)

        grp = SWA_HEADS // SWA_KV_HEADS
        sink = swa_sink[i].astype(F32) * LOG2E
        yb = _attention(xsq, csk, csv, xsk, xsv, grp=grp, sink=sink, window=SWA_WINDOW, bq=1024)
        lbf, lbb = lb[0, i], lb[1, i]
        gp = jnp.stack([jnp.log(lbf), jnp.log1p(-lbf), 1.0 - lbf,
                        jnp.log(lbb), jnp.log1p(-lbb), 1.0 - lbb, lbf, lbb], axis=0)
        of_c, ob_c, stf, stb = _hgrn(chq, chzf, chzb, chi, gp, zero_st, zero_st)
        of, ob, _, _ = _hgrn(xhq, xhzf, xhzb, xhi, gp, stf, stb)
        yd = _attention(xmq, cmk, cmv, xmk, xmv, grp=1, bq=1024, bk=2048)

        bglu = s5_b_glu[i].reshape(1, -1)
        hn = jnp.tile(hg_norm_g[i], HG_HEADS).reshape(1, -1)
        x = _outproj(x, g1, ya, yb, of, ob, xhg, yd, wglu, bglu, hn, pavg, wo_p, i)
        n2 = norm2_g[i].reshape(1, d)
        x = _ffn(x, sc2, sh2, g2, n2, wup, wdn, fg, not need_ctx, i)
        if need_ctx:
            yb_c = _attention(csq, csk, csv, grp=grp, sink=sink)
            yd_c = _attention(cmq, cmk, cmv, grp=1)
            ctx = _outproj(ctx, cg1, ya_c, yb_c, of_c, ob_c, chg, yd_c, wglu, bglu, hn, pavg, wo_p, i)
            ctx = _ffn(ctx, csc2, csh2, cg2, n2, wup, wdn, fg, False, i)
    return x
```

```python
import functools
import math

import jax
import jax.numpy as jnp
from jax import lax
from jax.experimental import pallas as pl
from jax.experimental.pallas import tpu as pltpu

F32 = jnp.float32
BF16 = jnp.bfloat16

EPS = 1e-6
NEG_INF = -1e30
ROPE_BASE = 10000.0
GRID_W = 64
LANE = 128
VMEM_LIMIT = 56 * 1024 * 1024

S5_CH, S5_GROUP, S5_STATE = 256, 16, 64
S5_GROUPS = S5_CH // S5_GROUP
S5_CHUNK = 16
SWA_HEADS, SWA_KV_HEADS, SWA_HEAD_DIM, SWA_WINDOW = 4, 2, 64, 128
HG_HEADS, HG_DK, HG_DV = 4, 64, 64
HG_CHUNK = 64
MLA_HEADS, MLA_Q_RANK, MLA_KV_RANK = 4, 256, 128
MLA_NOPE, MLA_ROPE, MLA_V = 64, 32, 64
MLA_SCALE = (MLA_NOPE + MLA_ROPE) ** -0.5
LOG2E = 1.4426950408889634
V_ONE = 64


def _tile(n, pref):
    t = min(n, pref)
    assert n % t == 0, (n, pref)
    return t


def _params(sem):
    return pltpu.CompilerParams(dimension_semantics=sem, vmem_limit_bytes=VMEM_LIMIT)


def _full(shape):
    nd = len(shape)
    return pl.BlockSpec(shape, lambda *_: (0,) * nd, pipeline_mode=pl.Buffered(1))


def _layer(w, li):
    nd = w.ndim - 1
    return pl.BlockSpec((None,) + w.shape[1:], lambda *_: (li,) + (0,) * nd, pipeline_mode=pl.Buffered(1))


def _mod_kernel(c_ref, w_ref, b_ref, o_ref):
    c = c_ref[...]
    s = c * (1.0 / (1.0 + jnp.exp(-c)))
    o_ref[...] = jnp.dot(s.astype(BF16), w_ref[...].astype(BF16), preferred_element_type=F32) + b_ref[...]


def _modulation(cc, w_mod, b_mod):
    depth, d, n = w_mod.shape
    tn = _tile(n, 1536)
    return pl.pallas_call(
        _mod_kernel,
        out_shape=jax.ShapeDtypeStruct((depth, 8, n), F32),
        grid=(depth, n // tn),
        in_specs=[pl.BlockSpec((8, d), lambda l, j: (0, 0)),
                  pl.BlockSpec((None, d, tn), lambda l, j: (l, 0, j)),
                  pl.BlockSpec((None, 1, tn), lambda l, j: (l, 0, j))],
        out_specs=pl.BlockSpec((None, 8, tn), lambda l, j: (l, 0, j)),
        compiler_params=_params(("parallel", "parallel")),
        name="modulation",
    )(cc, w_mod, b_mod.reshape(depth, 1, n))


def _rope_block(x, t_ref, half):
    return (x * t_ref[0] + pltpu.roll(x, LANE - half, 1) * t_ref[1]
            + pltpu.roll(x, half, 1) * t_ref[2])


def _rms(x, g):
    return x * lax.rsqrt(jnp.mean(x * x, axis=-1, keepdims=True) + EPS) * g


_C_U = 0
_C_SQ = 256
_C_SKV = 512
_C_HG = 768
_C_CQ = _C_HG + 5 * 256
_C_CKV = _C_CQ + 256
_N_INP = _C_CKV + 256


def _inproj_kernel(x_ref, sc_ref, sh_ref, g_ref, w_ref, ts_ref, tm_ref, qg_ref, wqb_ref, kvg_ref, wkv_ref,
                   u_ref, sq_ref, sk_ref, sv_ref, hq_ref, hzf_ref, hzb_ref, hi_ref, hg_ref,
                   mq_ref, mk_ref, mv_ref, us_ref):
    h = _rms(x_ref[...], g_ref[...]) * (1.0 + sc_ref[...]) + sh_ref[...]
    hb = h.astype(BF16)

    def proj(lo, n):
        return jnp.dot(hb, w_ref[:, lo:lo + n], preferred_element_type=F32)

    cq = _rms(proj(_C_CQ, 256), qg_ref[...]).astype(BF16)
    ckv_kr = proj(_C_CKV, 256)
    ckv = _rms(ckv_kr[:, :LANE], kvg_ref[...]).astype(BF16)
    kr = pltpu.roll(ckv_kr[:, LANE:], MLA_NOPE, 1)

    u = proj(_C_U, 256)
    for v in range(2):
        us_ref[v] = u[:, v * LANE:(v + 1) * LANE]
    for j in range(S5_CHUNK):
        for v in range(2):
            u_ref[j, :, v * LANE:(v + 1) * LANE] = us_ref[v, pl.ds(j, us_ref.shape[1] // S5_CHUNK, stride=S5_CHUNK), :]
    lane = lax.broadcasted_iota(jnp.int32, (1, LANE), 1)
    low = lane < SWA_HEAD_DIM
    one = jnp.where(lane == V_ONE, 1.0, 0.0)

    def spread(pair, fill):
        return [jnp.where(low, blk, fill).astype(BF16) for blk in (pair, pltpu.roll(pair, SWA_HEAD_DIM, 1))]

    sq = proj(_C_SQ, 256)
    for pp in range(SWA_HEADS // 2):
        heads = spread(_rope_block(sq[:, pp * LANE:(pp + 1) * LANE], ts_ref, 16), 0.0)
        for e in range(2):
            sq_ref[:, (2 * pp + e) * LANE:(2 * pp + e + 1) * LANE] = heads[e]
    skv = proj(_C_SKV, 256)
    for e, blk in enumerate(spread(_rope_block(skv[:, :LANE], ts_ref, 16), 0.0)):
        sk_ref[:, e * LANE:(e + 1) * LANE] = blk
    for e, blk in enumerate(spread(skv[:, LANE:], one)):
        sv_ref[:, e * LANE:(e + 1) * LANE] = blk

    q = jnp.dot(cq, wqb_ref[...], preferred_element_type=F32)
    for hh in range(MLA_HEADS):
        sl = slice(hh * LANE, (hh + 1) * LANE)
        mq_ref[:, sl] = _rope_block(q[:, sl], tm_ref, 8).astype(BF16)
    for i, r in enumerate((hq_ref, hzf_ref, hzb_ref)):
        r[...] = proj(_C_HG + i * 256, 256)
    kv = jnp.dot(ckv, wkv_ref[...], preferred_element_type=F32)
    for hh in range(MLA_HEADS):
        sl = slice(hh * LANE, (hh + 1) * LANE)
        mk_ref[:, sl] = _rope_block(kv[:, sl] + kr, tm_ref, 8).astype(BF16)
        mv_ref[:, sl] = jnp.where(lane == V_ONE, 1.0, kv[:, MLA_HEADS * LANE + hh * LANE:][:, :LANE]).astype(BF16)
    for i, r in enumerate((hi_ref, hg_ref)):
        r[...] = proj(_C_HG + (3 + i) * 256, 256)


def _inproj(x, sc, sh, g, w_in_p, tab_swa, tab_mla, qg, wqb_p, kvg, wkv_p, li):
    b, t, d = x.shape
    tt = _tile(t, 1024)
    row = lambda n, dt: jax.ShapeDtypeStruct((b, t, n), dt)
    out_shape = (jax.ShapeDtypeStruct((b, S5_CHUNK, t // S5_CHUNK, 256), F32),
                 row(512, BF16), row(256, BF16), row(256, BF16),
                 row(256, F32), row(256, F32), row(256, F32), row(256, F32), row(256, F32),
                 row(512, BF16), row(512, BF16), row(512, BF16))
    xs = lambda n: pl.BlockSpec((None, tt, n), lambda i, bb: (bb, i, 0))
    vec = pl.BlockSpec((None, 1, d), lambda i, bb: (bb, 0, 0))
    tab = pl.BlockSpec((3, tt, LANE), lambda i, bb: (0, i, 0))
    return pl.pallas_call(
        _inproj_kernel,
        out_shape=out_shape,
        grid=(t // tt, b),
        in_specs=[xs(d), vec, vec, _full((1, d)), _layer(w_in_p, li), tab, tab,
                  _full((1, 256)), _layer(wqb_p, li), _full((1, 128)), _layer(wkv_p, li)],
        out_specs=(pl.BlockSpec((None, S5_CHUNK, tt // S5_CHUNK, 256), lambda i, bb: (bb, 0, i, 0)),)
        + tuple(xs(s.shape[-1]) for s in out_shape[1:]),
        scratch_shapes=[pltpu.VMEM((2, tt, LANE), F32)],
        compiler_params=_params(("parallel", "parallel")),
        name="inproj",
    )(x, sc, sh, g, w_in_p, tab_swa, tab_mla, qg, wqb_p, kvg, wkv_p)


def _s5_kernel(u_ref, mt_ref, ft_ref, et_ref, a_ref, s0_ref, y_ref, sfin_ref, zs_ref, ut_ref, yt_ref, *, nc):
    ng, half = S5_GROUPS, LANE // S5_GROUP
    for j in range(S5_CHUNK):
        for v in range(2):
            t = u_ref[j, :, v * LANE:(v + 1) * LANE].T
            for gl in range(half):
                ut_ref[half * v + gl, S5_GROUP * j:S5_GROUP * (j + 1), :] = (
                    t[S5_GROUP * gl:S5_GROUP * (gl + 1), :].astype(BF16))
    for g in range(ng):
        z = jnp.dot(ft_ref[g], ut_ref[g], preferred_element_type=F32).T
        for c in range(2):
            zs_ref[c, pl.ds(g, nc, stride=ng), :] = z[:, c * LANE:(c + 1) * LANE]

    fmask = lax.broadcasted_iota(jnp.int32, (1, LANE), 1) < S5_STATE
    ar, ai = a_ref[0], a_ref[1]

    def step(i, s):
        rf = pl.multiple_of(i * ng, ng)
        rb = pl.multiple_of((nc - 1 - i) * ng, ng)
        zre, zim = (jnp.where(fmask, zs_ref[c, pl.ds(rf, ng), :], zs_ref[c, pl.ds(rb, ng), :]) for c in range(2))
        for c in range(2):
            zs_ref[c, pl.ds(rf, ng), 0:S5_STATE] = s[c][:, 0:S5_STATE]
            zs_ref[c, pl.ds(rb, ng), S5_STATE:LANE] = s[c][:, S5_STATE:LANE]
        re, im = s
        return ar * re - ai * im + zre, ar * im + ai * re + zim

    re, im = lax.fori_loop(0, nc, step, (s0_ref[:, :LANE], s0_ref[:, LANE:]), unroll=4)
    sfin_ref[:, :LANE] = re
    sfin_ref[:, LANE:] = im

    for g in range(ng):
        ss = jnp.concatenate([zs_ref[c, pl.ds(g, nc, stride=ng), :] for c in range(2)], axis=1)
        yt = (jnp.dot(mt_ref[g], ut_ref[g], preferred_element_type=F32)
              + jnp.dot(et_ref[g], ss.T.astype(BF16), preferred_element_type=F32))
        for t in range(S5_CHUNK):
            yt_ref[t, S5_GROUP * g:S5_GROUP * (g + 1), :] = yt[S5_GROUP * t:S5_GROUP * (t + 1), :]
    for t in range(S5_CHUNK):
        y_ref[t] = yt_ref[t].T


def _s5_scan(uj, mats, s0, li):
    mt, ft, et, a16 = mats
    b, _, nc, _ = uj.shape
    slab = pl.BlockSpec((None, S5_CHUNK, nc, 256), lambda bb: (bb, 0, 0, 0), pipeline_mode=pl.Buffered(1))
    st = pl.BlockSpec((None, S5_GROUPS, 256), lambda bb: (bb, 0, 0))
    return pl.pallas_call(
        functools.partial(_s5_kernel, nc=nc),
        out_shape=(jax.ShapeDtypeStruct(uj.shape, F32), jax.ShapeDtypeStruct((b, S5_GROUPS, 256), F32)),
        grid=(b,),
        in_specs=[slab, _layer(mt, li), _layer(ft, li), _layer(et, li), _layer(a16, li), st],
        out_specs=(slab, st),
        scratch_shapes=[pltpu.VMEM((2, S5_GROUPS * nc, LANE), F32), pltpu.VMEM((S5_GROUPS, 256, nc), BF16),
                        pltpu.VMEM((S5_CHUNK, 256, nc), F32)],
        compiler_params=_params(("parallel",)),
        name="s5",
    )(uj, mt, ft, et, a16, s0)


def _toeplitz_kernel(w_ref, o_ref):
    w = w_ref[...]
    for t in range(S5_CHUNK):
        lo = S5_GROUP * (S5_CHUNK - 1 - t)
        o_ref[S5_GROUP * t:S5_GROUP * (t + 1), :] = w[:, lo:lo + S5_CHUNK * S5_GROUP].astype(BF16)


def _toeplitz(strip):
    depth, ng, rows, width = strip.shape
    n = S5_CHUNK * S5_GROUP
    return pl.pallas_call(
        _toeplitz_kernel,
        out_shape=jax.ShapeDtypeStruct((depth, ng, n, n), BF16),
        grid=(depth, ng),
        in_specs=[pl.BlockSpec((None, None, rows, width), lambda d, g: (d, g, 0, 0))],
        out_specs=pl.BlockSpec((None, None, n, n), lambda d, g: (d, g, 0, 0)),
        compiler_params=_params(("parallel", "parallel")),
        name="toeplitz",
    )(strip)


def _s5_matrices(lam_re, lam_im, log_dt, b_re, b_im, c_re, c_im, d_skip):
    L = S5_CHUNK
    depth = lam_re.shape[0]
    hp = lax.Precision.HIGHEST
    lam = lax.complex(lam_re, lam_im)
    ldt = lam * jnp.exp(log_dt)
    bb = ((jnp.exp(ldt) - 1.0) / lam)[..., None] * lax.complex(b_re, b_im)
    cc = lax.complex(c_re, c_im)
    k = jnp.arange(L + 1, dtype=F32)
    apow = jnp.exp(ldt[:, None] * k[None, :, None, None, None])
    kern = jnp.real(jnp.einsum('ndghp,nkdgp,ndgpi->ndkgih', cc, apow[:, :L], bb, precision=hp))
    idx = jnp.arange(L)
    dg = d_skip.reshape(depth, 1, S5_GROUPS, 1, S5_GROUP) * jnp.eye(S5_GROUP, dtype=F32)[None, None, None]
    lag0 = kern[:, 0, :1] + kern[:, 1, :1] + dg
    by_lag = jnp.concatenate([kern[:, 0, :0:-1], lag0, kern[:, 1, 1:]], axis=1)
    strip = by_lag.transpose(0, 2, 4, 1, 3).reshape(depth, S5_GROUPS, S5_GROUP, (2 * L - 1) * S5_GROUP)
    mt = _toeplitz(jnp.pad(strip, ((0, 0), (0, 0), (0, 0), (0, S5_GROUP))))
    zf = apow[:, L - 1 - idx, 0][..., None] * bb[:, 0][:, None]
    zb = apow[:, idx, 1][..., None] * bb[:, 1][:, None]
    fmat = jnp.concatenate([jnp.real(zf), jnp.real(zb), jnp.imag(zf), jnp.imag(zb)], axis=3)
    fmat = fmat.transpose(0, 2, 1, 4, 3).reshape(depth, S5_GROUPS, L * S5_GROUP, 4 * S5_STATE)
    wf = cc[:, 0][:, None] * apow[:, 1 + idx, 0][:, :, :, None, :]
    wb = cc[:, 1][:, None] * apow[:, L - idx, 1][:, :, :, None, :]
    emat = jnp.concatenate([jnp.real(wf), jnp.real(wb), -jnp.imag(wf), -jnp.imag(wb)], axis=4)
    emat = emat.transpose(0, 2, 4, 1, 3).reshape(depth, S5_GROUPS, 4 * S5_STATE, L * S5_GROUP)
    al = apow[:, L]
    a16 = jnp.stack([jnp.concatenate([jnp.real(al[:, 0]), jnp.real(al[:, 1])], axis=-1),
                     jnp.concatenate([jnp.imag(al[:, 0]), jnp.imag(al[:, 1])], axis=-1)], axis=1)
    tr = lambda w: jnp.swapaxes(w, -1, -2).astype(BF16)
    return mt, tr(fmat), tr(emat), a16.astype(F32)


def _attn_kernel(*refs, grp, bq, sub, has_sink, has_seq, window, t_seq, bk):
    refs = list(refs)
    sink_ref = refs.pop(0) if has_sink else None
    q_ref, kc_ref, vc_ref = refs[:3]
    ks_ref, vs_ref = (refs[3], refs[4]) if has_seq else (None, None)
    o_ref = refs[-1]
    hk = pl.program_id(1)
    qi = pl.program_id(2)

    def scores(q, k, mask=None):
        s = lax.dot_general(q, k, (((1,), (1,)), ((), ())), preferred_element_type=F32)
        return s if mask is None else jnp.where(mask, s, NEG_INF)

    band_cache = {}

    def col_minus_row(rows, wlen, nq):
        if (rows, wlen, nq) not in band_cache:
            band_cache[rows, wlen, nq] = (lax.broadcasted_iota(jnp.int32, (rows, wlen), 1)
                                          - lax.broadcasted_iota(jnp.int32, (rows, wlen), 0) % nq)
        return band_cache[rows, wlen, nq]

    def logits(r0, nq):
        rows = grp * nq
        q = jnp.concatenate([q_ref[r0:r0 + nq, g * LANE:(g + 1) * LANE] for g in range(grp)], axis=0)
        s_parts, v_parts = [scores(q, kc_ref[...])], [vc_ref[...]]
        if has_seq and window is not None:
            wlen = nq + 2 * window
            q0 = qi * bq + r0
            start = pl.multiple_of(jnp.clip(q0 - window, 0, t_seq - wlen), LANE)
            in_band = jnp.abs(col_minus_row(rows, wlen, nq) + (start - q0)) <= window
            s_parts.append(scores(q, ks_ref[pl.ds(start, wlen), :], in_band))
            v_parts.append(vs_ref[pl.ds(start, wlen), :])
        elif has_seq:
            s_parts.append(scores(q, ks_ref[0:bk, :]))
            v_parts.append(vs_ref[0:bk, :])
        return q, s_parts, v_parts

    def softmax_pv(nq, s_parts, v_parts):
        m = functools.reduce(jnp.maximum, [jnp.max(s, axis=-1, keepdims=True) for s in s_parts])
        if has_sink:
            sink = jnp.concatenate([jnp.full((nq, 1), sink_ref[hk * grp + g], F32) for g in range(grp)], axis=0)
            m = jnp.maximum(m, sink)
        acc = sum(jnp.dot(jnp.exp2(s - m).astype(BF16), v, preferred_element_type=F32)
                  for s, v in zip(s_parts, v_parts))
        if has_sink:
            one_lane = lax.broadcasted_iota(jnp.int32, (1, LANE), 1) == V_ONE
            acc = acc + jnp.where(one_lane, jnp.exp2(sink - m), 0.0)
        return m, acc

    def finish(r0, nq, acc):
        o = acc * (1.0 / acc[:, V_ONE:V_ONE + 1])
        for g in range(grp):
            o_ref[r0:r0 + nq, g * LANE:(g + 1) * LANE] = o[g * nq:(g + 1) * nq].astype(o_ref.dtype)

    if has_seq and window is not None:
        for r0 in range(0, bq, sub):
            _, s_parts, v_parts = logits(r0, sub)
            finish(r0, sub, softmax_pv(sub, s_parts, v_parts)[1])
        return
    q, s_parts, v_parts = logits(0, bq)
    m, acc = softmax_pv(bq, s_parts, v_parts)
    if has_seq:
        def body(j, carry):
            m, acc = carry
            st = pl.multiple_of(j * bk, bk)
            s = scores(q, ks_ref[pl.ds(st, bk), :])
            m_new = jnp.maximum(m, jnp.max(s, axis=-1, keepdims=True))
            p = jnp.exp2(s - m_new).astype(BF16)
            acc = jnp.exp2(m - m_new) * acc + jnp.dot(p, vs_ref[pl.ds(st, bk), :], preferred_element_type=F32)
            return m_new, acc
        m, acc = lax.fori_loop(1, t_seq // bk, body, (m, acc), unroll=True)
    finish(0, bq, acc)


def _attention(q, kc, vc, ks=None, vs=None, *, grp, sink=None, window=None, bq=512, bk=1024):
    b, tq, wq = q.shape
    hkv = wq // (grp * LANE)
    lc = kc.shape[1]
    has_seq = ks is not None
    t_seq = ks.shape[1] if has_seq else 0
    bq = _tile(tq, bq)
    sub = min(bq, LANE)
    if has_seq and window is None:
        bk = _tile(t_seq, bk)
    kern = functools.partial(_attn_kernel, grp=grp, bq=bq, sub=sub, has_sink=sink is not None, has_seq=has_seq,
                             window=window, t_seq=t_seq, bk=bk)
    in_specs, args = [], []
    if sink is not None:
        in_specs.append(pl.BlockSpec(memory_space=pltpu.SMEM))
        args.append(sink)
    in_specs.append(pl.BlockSpec((None, bq, grp * LANE), lambda bb, h, i: (bb, i, h)))
    args.append(q)
    ctx_spec = pl.BlockSpec((None, lc, LANE), lambda bb, h, i: (bb, 0, h))
    in_specs += [ctx_spec, ctx_spec]
    args += [kc, vc]
    if has_seq:
        seq_spec = pl.BlockSpec((None, t_seq, LANE), lambda bb, h, i: (bb, 0, h))
        in_specs += [seq_spec, seq_spec]
        args += [ks, vs]
    return pl.pallas_call(
        kern,
        out_shape=jax.ShapeDtypeStruct((b, tq, wq), BF16),
        grid=(b, hkv, tq // bq),
        in_specs=in_specs,
        out_specs=pl.BlockSpec((None, bq, grp * LANE), lambda bb, h, i: (bb, i, h)),
        compiler_params=_params(("parallel", "parallel", "parallel")),
        name="attention",
    )(*args)


def _split3(x):
    hi = x.astype(BF16)
    r = x - hi.astype(F32)
    mid = r.astype(BF16)
    lo = (r - mid.astype(F32)).astype(BF16)
    return hi, mid, lo


def _hg_kernel(gp_ref, qf_ref, zf_ref, vf_ref, qb_ref, zb_ref, vb_ref, s0f_ref, s0b_ref,
               of_ref, ob_ref, sf_ref, sb_ref, st_ref, *, nchunk):
    i = pl.program_id(1)
    c = HG_CHUNK
    w = HG_HEADS * HG_DK

    @pl.when(i == 0)
    def _():
        st_ref[0] = s0f_ref[...]
        st_ref[1] = s0b_ref[...]

    ri = lax.broadcasted_iota(jnp.int32, (c, c), 0)
    ci = lax.broadcasted_iota(jnp.int32, (c, c), 1)
    ti = lax.broadcasted_iota(jnp.int32, (c, HG_HEADS * c), 0)
    si = lax.broadcasted_iota(jnp.int32, (c, HG_HEADS * c), 1) % c
    lane = lax.broadcasted_iota(jnp.int32, (1, w), 1)
    hmask = [(lane // HG_DK) == h for h in range(HG_HEADS)]
    r2 = lax.broadcasted_iota(jnp.int32, (w, w), 0) // HG_DV
    c2 = lax.broadcasted_iota(jnp.int32, (w, w), 1) // HG_DK
    diag = r2 == c2
    dirs = {
        True: dict(q=qf_ref, z=zf_ref, v=vf_ref, o=of_ref, gp=tuple(gp_ref[r:r + 1, :] for r in range(3)),
                   tri=jnp.where(ci <= ri, 1.0, 0.0).astype(BF16), causal=si <= ti),
        False: dict(q=qb_ref, z=zb_ref, v=vb_ref, o=ob_ref, gp=tuple(gp_ref[r:r + 1, :] for r in range(3, 6)),
                    tri=jnp.where(ci >= ri, 1.0, 0.0).astype(BF16), causal=si >= ti),
    }
    units = []
    for n in range(nchunk):
        units += [(True, n), (False, nchunk - 1 - n)]

    ph1 = []
    for fwd, n in units:
        dd = dirs[fwd]
        sl = slice(n * c, (n + 1) * c)
        z = dd['z'][sl, :]
        log_lb, l1p, om = dd['gp']
        ls = jnp.minimum(z, 0.0) - jnp.log(1.0 + jnp.exp(-jnp.abs(z)))
        bterm = l1p + ls
        lf = jnp.maximum(log_lb, bterm) + jnp.log(1.0 + jnp.exp(-jnp.abs(log_lb - bterm)))
        k = om / (1.0 + jnp.exp(z))
        gsum = sum(jnp.dot(dd['tri'], part, preferred_element_type=F32) for part in _split3(lf))
        ph1.append((k, gsum))

    ph2 = []
    for (fwd, n), (k, gsum) in zip(units, ph1):
        dd = dirs[fwd]
        sl = slice(n * c, (n + 1) * c)
        q, v = dd['q'][sl, :], dd['v'][sl, :]
        tot = gsum[c - 1:c, :] if fwd else gsum[0:1, :]
        gm = gsum[c // 2:c // 2 + 1, :]
        qd = (q * jnp.exp(gsum - gm)).astype(BF16)
        kinv = k * jnp.exp(gm - gsum)
        q_in = (q * jnp.exp(gsum)).astype(BF16)
        k_end = (k * jnp.exp(tot - gsum)).astype(BF16)
        kstack = jnp.concatenate([jnp.where(hm, kinv, 0.0) for hm in hmask], axis=0).astype(BF16)
        vstack = jnp.concatenate([jnp.where(hm, v, 0.0) for hm in hmask], axis=0).astype(BF16)
        ph2.append((qd, kstack, vstack, q_in, k_end, v.astype(BF16), jnp.exp(tot)))

    ph3 = []
    for (fwd, n), (qd, kstack, vstack, q_in, k_end, vb, dec) in zip(units, ph2):
        att = lax.dot_general(qd, kstack, (((1,), (1,)), ((), ())), preferred_element_type=F32)
        att = jnp.where(dirs[fwd]['causal'], att, 0.0).astype(BF16)
        o_intra = jnp.dot(att, vstack, preferred_element_type=F32)
        kv_t = lax.dot_general(vb, k_end, (((0,), (0,)), ((), ())), preferred_element_type=F32)
        ph3.append((o_intra, jnp.where(diag, kv_t, 0.0)))

    st = {True: st_ref[0], False: st_ref[1]}
    for (fwd, n), (_, _, _, q_in, _, _, dec), (o_intra, kv_t) in zip(units, ph2, ph3):
        sl = slice(n * c, (n + 1) * c)
        o_inter = lax.dot_general(q_in, st[fwd].astype(BF16), (((1,), (1,)), ((), ())), preferred_element_type=F32)
        dirs[fwd]['o'][sl, :] = o_intra + o_inter
        st[fwd] = st[fwd] * dec + kv_t
    st_ref[0] = st[True]
    st_ref[1] = st[False]

    @pl.when(i == pl.num_programs(1) - 1)
    def _():
        sf_ref[...] = st[True]
        sb_ref[...] = st[False]


def _hgrn(q, zf, zb, v, gp, s0f, s0b):
    b, t, w = q.shape
    tb = _tile(t, 1024)
    nblk = t // tb
    fw = pl.BlockSpec((None, tb, w), lambda bb, i: (bb, i, 0))
    bw = pl.BlockSpec((None, tb, w), lambda bb, i: (bb, nblk - 1 - i, 0))
    st = pl.BlockSpec((None, w, w), lambda bb, i: (bb, 0, 0))
    return pl.pallas_call(
        functools.partial(_hg_kernel, nchunk=tb // HG_CHUNK),
        out_shape=(jax.ShapeDtypeStruct((b, t, w), F32), jax.ShapeDtypeStruct((b, t, w), F32),
                   jax.ShapeDtypeStruct((b, w, w), F32), jax.ShapeDtypeStruct((b, w, w), F32)),
        grid=(b, nblk),
        in_specs=[_full((8, w)), fw, fw, fw, bw, bw, bw, st, st],
        out_specs=(fw, bw, st, st),
        scratch_shapes=[pltpu.VMEM((2, w, w), F32)],
        compiler_params=_params(("parallel", "arbitrary")),
        name="hgrn2",
    )(gp, q, zf, v, q, zb, v, s0f, s0b)


def _gelu_tanh(x):
    return 0.5 * x * (1.0 + jnp.tanh(math.sqrt(2.0 / math.pi) * (x + 0.044715 * (x * x * x))))


def _outproj_kernel(x_ref, g1_ref, ya_ref, yb_ref, of_ref, ob_ref, hg_ref, yd_ref,
                    wglu_ref, bglu_ref, hn_ref, pavg_ref, wo_ref, o_ref, ys_ref):
    for j in range(S5_CHUNK):
        for v in range(2):
            ys_ref[v, pl.ds(j, ys_ref.shape[1] // S5_CHUNK, stride=S5_CHUNK), :] = ya_ref[j, :, v * LANE:(v + 1) * LANE]
    ya = _gelu_tanh(jnp.concatenate([ys_ref[0], ys_ref[1]], axis=1))
    gl = jnp.dot(ya.astype(BF16), wglu_ref[...], preferred_element_type=F32) + bglu_ref[...]
    o = of_ref[...] + ob_ref[...]
    o2 = o * o
    hi = o2.astype(BF16)
    lo = (o2 - hi.astype(F32)).astype(BF16)
    ms = (jnp.dot(hi, pavg_ref[...], preferred_element_type=F32)
          + jnp.dot(lo, pavg_ref[...], preferred_element_type=F32))
    y = (jnp.dot(yb_ref[...], wo_ref[256:768, :], preferred_element_type=F32)
         + jnp.dot(yd_ref[...], wo_ref[1024:1536, :], preferred_element_type=F32))
    ya = ya * (1.0 / (1.0 + jnp.exp(-gl)))
    gate = hg_ref[...]
    yc = o * lax.rsqrt(ms + EPS) * hn_ref[...] * (gate * (1.0 / (1.0 + jnp.exp(-gate))))
    y = (y + jnp.dot(ya.astype(BF16), wo_ref[0:256, :], preferred_element_type=F32)
         + jnp.dot(yc.astype(BF16), wo_ref[768:1024, :], preferred_element_type=F32))
    o_ref[...] = x_ref[...] + g1_ref[...] * y


def _outproj(x, g1, ya, yb, of, ob, hg, yd, wglu, bglu, hn, pavg, wo_p, li):
    b, t, d = x.shape
    tt = _tile(t, 512)
    xs = lambda n: pl.BlockSpec((None, tt, n), lambda bb, i: (bb, i, 0))
    vec = pl.BlockSpec((None, 1, d), lambda bb, i: (bb, 0, 0))
    return pl.pallas_call(
        _outproj_kernel,
        out_shape=jax.ShapeDtypeStruct((b, t, d), F32),
        grid=(b, t // tt),
        in_specs=[xs(d), vec, pl.BlockSpec((None, S5_CHUNK, tt // S5_CHUNK, 256), lambda bb, i: (bb, 0, i, 0)),
                  xs(512), xs(256), xs(256), xs(256), xs(512),
                  _layer(wglu, li), _full((1, 256)), _full((1, 256)), _full(pavg.shape), _layer(wo_p, li)],
        out_specs=xs(d),
        scratch_shapes=[pltpu.VMEM((2, tt, LANE), F32)],
        compiler_params=_params(("parallel", "parallel")),
        name="outproj",
    )(x, g1, ya, yb, of, ob, hg, yd, wglu, bglu, hn, pavg, wo_p)


def _ffn_kernel(x_ref, sc_ref, sh_ref, g2_ref, ng_ref, wup_ref, wdn_ref, fg_ref, o_ref, acc_ref, *, hidden, ck, final):
    x = x_ref[...]
    hb = (_rms(x, ng_ref[...]) * (1.0 + sc_ref[...]) + sh_ref[...]).astype(BF16)
    for j in range(hidden // ck):
        gate = jnp.dot(hb, wup_ref[:, j * ck:(j + 1) * ck], preferred_element_type=F32)
        up = jnp.dot(hb, wup_ref[:, hidden + j * ck:hidden + (j + 1) * ck], preferred_element_type=F32)
        a = (gate * (1.0 / (1.0 + jnp.exp(-gate))) * up).astype(BF16)
        part = jnp.dot(a, wdn_ref[j * ck:(j + 1) * ck, :], preferred_element_type=F32)
        if j == 0:
            acc_ref[...] = part
        else:
            acc_ref[...] += part
    y = x + g2_ref[...] * acc_ref[...]
    if final:
        y = _rms(y, fg_ref[...])
    o_ref[...] = y


def _ffn(x, sc, sh, g2, ng, wup, wdn, fg, final, li):
    b, t, d = x.shape
    hidden = wdn.shape[1]
    tt = _tile(t, 512)
    xs = pl.BlockSpec((None, tt, d), lambda bb, i: (bb, i, 0))
    vec = pl.BlockSpec((None, 1, d), lambda bb, i: (bb, 0, 0))
    return pl.pallas_call(
        functools.partial(_ffn_kernel, hidden=hidden, ck=_tile(hidden, 256), final=final),
        out_shape=jax.ShapeDtypeStruct((b, t, d), F32),
        grid=(b, t // tt),
        in_specs=[xs, vec, vec, vec, _full((1, d)), _layer(wup, li), _layer(wdn, li), _full((1, d))],
        out_specs=xs,
        scratch_shapes=[pltpu.VMEM((tt, d), F32)],
        compiler_params=_params(("parallel", "parallel")),
        name="ffn",
    )(x, sc, sh, g2, ng, wup, wdn, fg)


def _pad_heads(w, heads, dim):
    w = w.reshape(w.shape[:-1] + (heads, dim))
    w = jnp.pad(w, [(0, 0)] * (w.ndim - 1) + [(0, LANE - dim)])
    return w.reshape(w.shape[:-2] + (heads * LANE,))


def _layer_weights(w_in, w_out, mla_w_qb, mla_w_kvb):
    depth, d, n_in = w_in.shape
    qscale = jnp.ones((n_in,), F32).at[_C_SQ:_C_SQ + SWA_HEADS * SWA_HEAD_DIM].set(SWA_HEAD_DIM ** -0.5 * LOG2E)
    w_in_p = jnp.pad(w_in * qscale, ((0, 0), (0, 0), (0, _N_INP - n_in))).astype(BF16)
    wqb_p = _pad_heads(mla_w_qb * (MLA_SCALE * LOG2E), MLA_HEADS, MLA_NOPE + MLA_ROPE).astype(BF16)
    kvb = mla_w_kvb.reshape(depth, MLA_KV_RANK, MLA_HEADS, MLA_NOPE + MLA_V)
    wk = _pad_heads(kvb[..., :MLA_NOPE].reshape(depth, MLA_KV_RANK, -1), MLA_HEADS, MLA_NOPE)
    wv = _pad_heads(kvb[..., MLA_NOPE:].reshape(depth, MLA_KV_RANK, -1), MLA_HEADS, MLA_V)
    wkv_p = jnp.concatenate([wk, wv], axis=-1).astype(BF16)
    assert w_out.shape[1] == 4 * 256
    pad_rows = lambda w, heads, dim: jnp.swapaxes(_pad_heads(jnp.swapaxes(w, -1, -2), heads, dim), -1, -2)
    wo_p = jnp.concatenate([w_out[:, 0:256], pad_rows(w_out[:, 256:512], SWA_HEADS, SWA_HEAD_DIM),
                            w_out[:, 512:768], pad_rows(w_out[:, 768:1024], MLA_HEADS, MLA_V)], axis=1).astype(BF16)
    return w_in_p, wqb_p, wkv_p, wo_p


def _rope_tables(length, dim, lo, ident, repeat=1):
    n_freq = dim // 4
    rows = length // GRID_W
    row = jnp.repeat(jnp.arange(rows, dtype=F32), GRID_W)
    col = jnp.tile(jnp.arange(GRID_W, dtype=F32), rows)
    inv = ROPE_BASE ** (-jnp.arange(n_freq, dtype=F32) / n_freq)
    ang = jnp.stack([row[:, None] * inv, col[:, None] * inv], axis=1)
    cos, sin = jnp.cos(ang), jnp.sin(ang)
    z = jnp.zeros_like(sin)
    cos_l = jnp.stack([cos, cos], axis=2).reshape(length, dim)
    sina = jnp.stack([-sin, z], axis=2).reshape(length, dim)
    sinb = jnp.stack([z, sin], axis=2).reshape(length, dim)
    if ident:
        cos_l, sina, sinb = jnp.ones_like(cos_l), jnp.zeros_like(sina), jnp.zeros_like(sinb)
    cos_l, sina, sinb = (jnp.tile(a, (1, repeat)) for a in (cos_l, sina, sinb))
    pad = lambda a, fill: jnp.pad(a, ((0, 0), (lo, LANE - lo - dim * repeat)), constant_values=fill)
    return jnp.stack([pad(cos_l, 1.0), pad(sina, 0.0), pad(sinb, 0.0)], axis=0)


def kernel(x, c, ctx, c_ctx, w_mod, b_mod, norm1_g, norm2_g, w_in, w_out, s5_lam_re, s5_lam_im, s5_log_dt,
           s5_b_re, s5_b_im, s5_c_re, s5_c_im, s5_d, s5_w_glu, s5_b_glu, swa_sink, hg_lb, hg_norm_g,
           mla_q_norm_g, mla_w_qb, mla_kv_norm_g, mla_w_kvb, ffn_w_up, ffn_w_down, final_norm_g):
    b, t, d = x.shape
    lc = ctx.shape[1]
    depth = w_mod.shape[0]

    cc = jnp.zeros((8, d), F32).at[:b].set(c).at[b].set(c_ctx)
    mods = _modulation(cc, w_mod, b_mod)

    lb_cum = jnp.cumsum(jax.nn.softmax(hg_lb.astype(F32), axis=1), axis=1)
    lb = lb_cum - lb_cum[:, :1]

    tab_swa = _rope_tables(t, SWA_HEAD_DIM, 0, False, repeat=2)
    tab_mla = _rope_tables(t, MLA_ROPE, MLA_NOPE, False)
    tab_swa_c = _rope_tables(lc, SWA_HEAD_DIM, 0, True, repeat=2)
    tab_mla_c = _rope_tables(lc, MLA_ROPE, MLA_NOPE, True)
    pavg = jnp.kron(jnp.eye(HG_HEADS, dtype=F32), jnp.full((HG_DV, HG_DV), 1.0 / HG_DV, F32)).astype(BF16)
    zero_st = jnp.zeros((b, HG_HEADS * HG_DV, HG_HEADS * HG_DK), F32)
    zero_s5 = jnp.zeros((b, S5_GROUPS, 4 * S5_STATE), F32)

    w_in_p, wqb_p, wkv_p, wo_p = _layer_weights(w_in, w_out, mla_w_qb, mla_w_kvb)
    mats = _s5_matrices(s5_lam_re, s5_lam_im, s5_log_dt, s5_b_re, s5_b_im, s5_c_re, s5_c_im, s5_d)
    wglu = s5_w_glu.astype(BF16)
    wup = ffn_w_up.astype(BF16)
    wdn = ffn_w_down.astype(BF16)
    fg = final_norm_g.reshape(1, d)

    for i in range(depth):
        need_ctx = i < depth - 1
        mod = mods[i, :b].reshape(b, 6, 1, d)
        mod_c = jnp.broadcast_to(mods[i, b].reshape(1, 6, 1, d), (b, 6, 1, d))
        sh1, sc1, g1, sh2, sc2, g2 = (mod[:, j] for j in range(6))
        csh1, csc1, cg1, csh2, csc2, cg2 = (mod_c[:, j] for j in range(6))
        n1 = norm1_g[i].reshape(1, d)
        qg = mla_q_norm_g[i].reshape(1, -1)
        kvg = mla_kv_norm_g[i].reshape(1, -1)
        px = _inproj(x, sc1, sh1, n1, w_in_p, tab_swa, tab_mla, qg, wqb_p, kvg, wkv_p, i)
        pc = _inproj(ctx, csc1, csh1, n1, w_in_p, tab_swa_c, tab_mla_c, qg, wqb_p, kvg, wkv_p, i)
        (xu, xsq, xsk, xsv, xhq, xhzf, xhzb, xhi, xhg, xmq, xmk, xmv) = px
        (cu, csq, csk, csv, chq, chzf, chzb, chi, chg, cmq, cmk, cmv) = pc

        ya_c, s5_fin = _s5_scan(cu, mats, zero_s5, i)
        ya, _ = _s5_scan(xu, mats, s5_fin, i)

        grp = SWA_HEADS // SWA_KV_HEADS
        sink = swa_sink[i].astype(F32) * LOG2E
        yb = _attention(xsq, csk, csv, xsk, xsv, grp=grp, sink=sink, window=SWA_WINDOW, bq=1024)
        lbf, lbb = lb[0, i], lb[1, i]
        gp = jnp.stack([jnp.log(lbf), jnp.log1p(-lbf), 1.0 - lbf,
                        jnp.log(lbb), jnp.log1p(-lbb), 1.0 - lbb, lbf, lbb], axis=0)
        of_c, ob_c, stf, stb = _hgrn(chq, chzf, chzb, chi, gp, zero_st, zero_st)
        of, ob, _, _ = _hgrn(xhq, xhzf, xhzb, xhi, gp, stf, stb)
        yd = _attention(xmq, cmk, cmv, xmk, xmv, grp=1, bq=1024, bk=2048)

        bglu = s5_b_glu[i].reshape(1, -1)
        hn = jnp.tile(hg_norm_g[i], HG_HEADS).reshape(1, -1)
        x = _outproj(x, g1, ya, yb, of, ob, xhg, yd, wglu, bglu, hn, pavg, wo_p, i)
        n2 = norm2_g[i].reshape(1, d)
        x = _ffn(x, sc2, sh2, g2, n2, wup, wdn, fg, not need_ctx, i)
        if need_ctx:
            yb_c = _attention(csq, csk, csv, grp=grp, sink=sink)
            yd_c = _attention(cmq, cmk, cmv, grp=1)
            ctx = _outproj(ctx, cg1, ya_c, yb_c, of_c, ob_c, chg, yd_c, wglu, bglu, hn, pavg, wo_p, i)
            ctx = _ffn(ctx, csc2, csh2, cg2, n2, wup, wdn, fg, False, i)
    return x
```

```python
import functools
import math

import jax
import jax.numpy as jnp
from jax import lax
from jax.experimental import pallas as pl
from jax.experimental.pallas import tpu as pltpu

F32 = jnp.float32
BF16 = jnp.bfloat16

EPS = 1e-6
NEG_INF = -1e30
ROPE_BASE = 10000.0
GRID_W = 64
LANE = 128
VMEM_V7X = 64 * 1024 * 1024
VMEM_LIMIT = VMEM_V7X - 8 * 1024 * 1024

S5_CH, S5_GROUP, S5_STATE = 256, 16, 64
S5_GROUPS = S5_CH // S5_GROUP
S5_CHUNK = 16
SWA_HEADS, SWA_KV_HEADS, SWA_HEAD_DIM, SWA_WINDOW = 4, 2, 64, 128
HG_HEADS, HG_DK, HG_DV = 4, 64, 64
HG_CHUNK = 64
MLA_HEADS, MLA_Q_RANK, MLA_KV_RANK = 4, 256, 128
MLA_NOPE, MLA_ROPE, MLA_V = 64, 32, 64
MLA_SCALE = (MLA_NOPE + MLA_ROPE) ** -0.5
LOG2E = 1.4426950408889634
V_ONE = 64


TILES = dict(
    mod_cols=1536,
    inproj=1024,
    hgrn=1024,
    outproj=512,
    ffn=512,
    ffn_hidden=256,
    attn_q=512,
    swa_q=1024,
    swa_sub=LANE,
    mla_q=1024,
    mla_k=2048,
)
S5_SCAN_UNROLL = 8


def _tile(n, pref):
    t = min(n, pref)
    assert n % t == 0, (n, pref)
    return t


def _params(sem):
    return pltpu.CompilerParams(dimension_semantics=sem, vmem_limit_bytes=VMEM_LIMIT)


def _full(shape):
    nd = len(shape)
    return pl.BlockSpec(shape, lambda *_: (0,) * nd, pipeline_mode=pl.Buffered(1))


def _layer(w, li):
    nd = w.ndim - 1
    return pl.BlockSpec((None,) + w.shape[1:], lambda *_: (li,) + (0,) * nd, pipeline_mode=pl.Buffered(1))


def _mod_kernel(c_ref, w_ref, b_ref, o_ref):
    c = c_ref[...]
    s = c * (1.0 / (1.0 + jnp.exp(-c)))
    o_ref[...] = jnp.dot(s.astype(BF16), w_ref[...].astype(BF16), preferred_element_type=F32) + b_ref[...]


def _modulation(cc, w_mod, b_mod):
    depth, d, n = w_mod.shape
    tn = _tile(n, TILES['mod_cols'])
    return pl.pallas_call(
        _mod_kernel,
        out_shape=jax.ShapeDtypeStruct((depth, 8, n), F32),
        grid=(depth, n // tn),
        in_specs=[pl.BlockSpec((8, d), lambda l, j: (0, 0)),
                  pl.BlockSpec((None, d, tn), lambda l, j: (l, 0, j)),
                  pl.BlockSpec((None, 1, tn), lambda l, j: (l, 0, j))],
        out_specs=pl.BlockSpec((None, 8, tn), lambda l, j: (l, 0, j)),
        compiler_params=_params(("parallel", "parallel")),
        name="modulation",
    )(cc, w_mod, b_mod.reshape(depth, 1, n))


def _rope_block(x, t_ref, half):
    return (x * t_ref[0] + pltpu.roll(x, LANE - half, 1) * t_ref[1]
            + pltpu.roll(x, half, 1) * t_ref[2])


def _rms(x, g):
    return x * lax.rsqrt(jnp.mean(x * x, axis=-1, keepdims=True) + EPS) * g


_C_U = 0
_C_SQ = 256
_C_SKV = 512
_C_HG = 768
_C_CQ = _C_HG + 5 * 256
_C_CKV = _C_CQ + 256
_N_INP = _C_CKV + 256


def _inproj_kernel(x_ref, sc_ref, sh_ref, g_ref, w_ref, ts_ref, tm_ref, qg_ref, wqb_ref, kvg_ref, wkv_ref,
                   u_ref, sq_ref, sk_ref, sv_ref, hq_ref, hzf_ref, hzb_ref, hi_ref, hg_ref,
                   mq_ref, mk_ref, mv_ref, us_ref):
    h = _rms(x_ref[...], g_ref[...]) * (1.0 + sc_ref[...]) + sh_ref[...]
    hb = h.astype(BF16)

    def proj(lo, n):
        return jnp.dot(hb, w_ref[:, lo:lo + n], preferred_element_type=F32)

    cq = _rms(proj(_C_CQ, 256), qg_ref[...]).astype(BF16)
    ckv_kr = proj(_C_CKV, 256)
    ckv = _rms(ckv_kr[:, :LANE], kvg_ref[...]).astype(BF16)
    kr = pltpu.roll(ckv_kr[:, LANE:], MLA_NOPE, 1)

    u = proj(_C_U, 256)
    for v in range(2):
        us_ref[v] = u[:, v * LANE:(v + 1) * LANE]
    for j in range(S5_CHUNK):
        for v in range(2):
            u_ref[j, :, v * LANE:(v + 1) * LANE] = us_ref[v, pl.ds(j, us_ref.shape[1] // S5_CHUNK, stride=S5_CHUNK), :]
    lane = lax.broadcasted_iota(jnp.int32, (1, LANE), 1)
    low = lane < SWA_HEAD_DIM
    one = jnp.where(lane == V_ONE, 1.0, 0.0)

    def spread(pair, fill):
        return [jnp.where(low, blk, fill).astype(BF16) for blk in (pair, pltpu.roll(pair, SWA_HEAD_DIM, 1))]

    sq = proj(_C_SQ, 256)
    for pp in range(SWA_HEADS // 2):
        heads = spread(_rope_block(sq[:, pp * LANE:(pp + 1) * LANE], ts_ref, 16), 0.0)
        for e in range(2):
            sq_ref[:, (2 * pp + e) * LANE:(2 * pp + e + 1) * LANE] = heads[e]
    skv = proj(_C_SKV, 256)
    for e, blk in enumerate(spread(_rope_block(skv[:, :LANE], ts_ref, 16), 0.0)):
        sk_ref[:, e * LANE:(e + 1) * LANE] = blk
    for e, blk in enumerate(spread(skv[:, LANE:], one)):
        sv_ref[:, e * LANE:(e + 1) * LANE] = blk

    q = jnp.dot(cq, wqb_ref[...], preferred_element_type=F32)
    for hh in range(MLA_HEADS):
        sl = slice(hh * LANE, (hh + 1) * LANE)
        mq_ref[:, sl] = _rope_block(q[:, sl], tm_ref, 8).astype(BF16)
    for i, r in enumerate((hq_ref, hzf_ref, hzb_ref)):
        r[...] = proj(_C_HG + i * 256, 256)
    kv = jnp.dot(ckv, wkv_ref[...], preferred_element_type=F32)
    for hh in range(MLA_HEADS):
        sl = slice(hh * LANE, (hh + 1) * LANE)
        mk_ref[:, sl] = _rope_block(kv[:, sl] + kr, tm_ref, 8).astype(BF16)
        mv_ref[:, sl] = jnp.where(lane == V_ONE, 1.0, kv[:, MLA_HEADS * LANE + hh * LANE:][:, :LANE]).astype(BF16)
    for i, r in enumerate((hi_ref, hg_ref)):
        r[...] = proj(_C_HG + (3 + i) * 256, 256)


def _inproj(x, sc, sh, g, w_in_p, tab_swa, tab_mla, qg, wqb_p, kvg, wkv_p, li):
    b, t, d = x.shape
    tt = _tile(t, TILES['inproj'])
    row = lambda n, dt: jax.ShapeDtypeStruct((b, t, n), dt)
    out_shape = (jax.ShapeDtypeStruct((b, S5_CHUNK, t // S5_CHUNK, 256), F32),
                 row(512, BF16), row(256, BF16), row(256, BF16),
                 row(256, F32), row(256, F32), row(256, F32), row(256, F32), row(256, F32),
                 row(512, BF16), row(512, BF16), row(512, BF16))
    xs = lambda n: pl.BlockSpec((None, tt, n), lambda i, bb: (bb, i, 0))
    vec = pl.BlockSpec((None, 1, d), lambda i, bb: (bb, 0, 0))
    tab = pl.BlockSpec((3, tt, LANE), lambda i, bb: (0, i, 0))
    return pl.pallas_call(
        _inproj_kernel,
        out_shape=out_shape,
        grid=(t // tt, b),
        in_specs=[xs(d), vec, vec, _full((1, d)), _layer(w_in_p, li), tab, tab,
                  _full((1, 256)), _layer(wqb_p, li), _full((1, 128)), _layer(wkv_p, li)],
        out_specs=(pl.BlockSpec((None, S5_CHUNK, tt // S5_CHUNK, 256), lambda i, bb: (bb, 0, i, 0)),)
        + tuple(xs(s.shape[-1]) for s in out_shape[1:]),
        scratch_shapes=[pltpu.VMEM((2, tt, LANE), F32)],
        compiler_params=_params(("parallel", "parallel")),
        name="inproj",
    )(x, sc, sh, g, w_in_p, tab_swa, tab_mla, qg, wqb_p, kvg, wkv_p)


def _s5_kernel(u_ref, mt_ref, ft_ref, et_ref, a_ref, s0_ref, y_ref, sfin_ref, zs_ref, ut_ref, yt_ref, *, nc):
    ng, half = S5_GROUPS, LANE // S5_GROUP
    for j in range(S5_CHUNK):
        for v in range(2):
            t = u_ref[j, :, v * LANE:(v + 1) * LANE].T
            for gl in range(half):
                ut_ref[half * v + gl, S5_GROUP * j:S5_GROUP * (j + 1), :] = (
                    t[S5_GROUP * gl:S5_GROUP * (gl + 1), :].astype(BF16))
    for g in range(ng):
        z = jnp.dot(ft_ref[g], ut_ref[g], preferred_element_type=F32).T
        for c in range(2):
            zs_ref[c, pl.ds(g, nc, stride=ng), :] = z[:, c * LANE:(c + 1) * LANE]

    fmask = lax.broadcasted_iota(jnp.int32, (1, LANE), 1) < S5_STATE
    ar, ai = a_ref[0], a_ref[1]

    def step(i, s):
        rf = pl.multiple_of(i * ng, ng)
        rb = pl.multiple_of((nc - 1 - i) * ng, ng)
        zre, zim = (jnp.where(fmask, zs_ref[c, pl.ds(rf, ng), :], zs_ref[c, pl.ds(rb, ng), :]) for c in range(2))
        for c in range(2):
            zs_ref[c, pl.ds(rf, ng), 0:S5_STATE] = s[c][:, 0:S5_STATE]
            zs_ref[c, pl.ds(rb, ng), S5_STATE:LANE] = s[c][:, S5_STATE:LANE]
        re, im = s
        return ar * re - ai * im + zre, ar * im + ai * re + zim

    re, im = lax.fori_loop(0, nc, step, (s0_ref[:, :LANE], s0_ref[:, LANE:]), unroll=S5_SCAN_UNROLL)
    sfin_ref[:, :LANE] = re
    sfin_ref[:, LANE:] = im

    for g in range(ng):
        ss = jnp.concatenate([zs_ref[c, pl.ds(g, nc, stride=ng), :] for c in range(2)], axis=1)
        yt = (jnp.dot(mt_ref[g], ut_ref[g], preferred_element_type=F32)
              + jnp.dot(et_ref[g], ss.T.astype(BF16), preferred_element_type=F32))
        for t in range(S5_CHUNK):
            yt_ref[t, S5_GROUP * g:S5_GROUP * (g + 1), :] = yt[S5_GROUP * t:S5_GROUP * (t + 1), :]
    for t in range(S5_CHUNK):
        y_ref[t] = yt_ref[t].T


def _s5_scan(uj, mats, s0, li):
    mt, ft, et, a16 = mats
    b, _, nc, _ = uj.shape
    slab = pl.BlockSpec((None, S5_CHUNK, nc, 256), lambda bb: (bb, 0, 0, 0), pipeline_mode=pl.Buffered(1))
    st = pl.BlockSpec((None, S5_GROUPS, 256), lambda bb: (bb, 0, 0))
    return pl.pallas_call(
        functools.partial(_s5_kernel, nc=nc),
        out_shape=(jax.ShapeDtypeStruct(uj.shape, F32), jax.ShapeDtypeStruct((b, S5_GROUPS, 256), F32)),
        grid=(b,),
        in_specs=[slab, _layer(mt, li), _layer(ft, li), _layer(et, li), _layer(a16, li), st],
        out_specs=(slab, st),
        scratch_shapes=[pltpu.VMEM((2, S5_GROUPS * nc, LANE), F32), pltpu.VMEM((S5_GROUPS, 256, nc), BF16),
                        pltpu.VMEM((S5_CHUNK, 256, nc), F32)],
        compiler_params=_params(("parallel",)),
        name="s5",
    )(uj, mt, ft, et, a16, s0)


def _toeplitz_kernel(w_ref, o_ref):
    w = w_ref[...]
    for t in range(S5_CHUNK):
        lo = S5_GROUP * (S5_CHUNK - 1 - t)
        o_ref[S5_GROUP * t:S5_GROUP * (t + 1), :] = w[:, lo:lo + S5_CHUNK * S5_GROUP].astype(BF16)


def _toeplitz(strip):
    depth, ng, rows, width = strip.shape
    n = S5_CHUNK * S5_GROUP
    return pl.pallas_call(
        _toeplitz_kernel,
        out_shape=jax.ShapeDtypeStruct((depth, ng, n, n), BF16),
        grid=(depth, ng),
        in_specs=[pl.BlockSpec((None, None, rows, width), lambda d, g: (d, g, 0, 0))],
        out_specs=pl.BlockSpec((None, None, n, n), lambda d, g: (d, g, 0, 0)),
        compiler_params=_params(("parallel", "parallel")),
        name="toeplitz",
    )(strip)


def _s5_matrices(lam_re, lam_im, log_dt, b_re, b_im, c_re, c_im, d_skip):
    L = S5_CHUNK
    depth = lam_re.shape[0]
    hp = lax.Precision.HIGHEST
    lam = lax.complex(lam_re, lam_im)
    ldt = lam * jnp.exp(log_dt)
    bb = ((jnp.exp(ldt) - 1.0) / lam)[..., None] * lax.complex(b_re, b_im)
    cc = lax.complex(c_re, c_im)
    k = jnp.arange(L + 1, dtype=F32)
    apow = jnp.exp(ldt[:, None] * k[None, :, None, None, None])
    kern = jnp.real(jnp.einsum('ndghp,nkdgp,ndgpi->ndkgih', cc, apow[:, :L], bb, precision=hp))
    idx = jnp.arange(L)
    dg = d_skip.reshape(depth, 1, S5_GROUPS, 1, S5_GROUP) * jnp.eye(S5_GROUP, dtype=F32)[None, None, None]
    lag0 = kern[:, 0, :1] + kern[:, 1, :1] + dg
    by_lag = jnp.concatenate([kern[:, 0, :0:-1], lag0, kern[:, 1, 1:]], axis=1)
    strip = by_lag.transpose(0, 2, 4, 1, 3).reshape(depth, S5_GROUPS, S5_GROUP, (2 * L - 1) * S5_GROUP)
    mt = _toeplitz(jnp.pad(strip, ((0, 0), (0, 0), (0, 0), (0, S5_GROUP))))
    zf = apow[:, L - 1 - idx, 0][..., None] * bb[:, 0][:, None]
    zb = apow[:, idx, 1][..., None] * bb[:, 1][:, None]
    fmat = jnp.concatenate([jnp.real(zf), jnp.real(zb), jnp.imag(zf), jnp.imag(zb)], axis=3)
    fmat = fmat.transpose(0, 2, 1, 4, 3).reshape(depth, S5_GROUPS, L * S5_GROUP, 4 * S5_STATE)
    wf = cc[:, 0][:, None] * apow[:, 1 + idx, 0][:, :, :, None, :]
    wb = cc[:, 1][:, None] * apow[:, L - idx, 1][:, :, :, None, :]
    emat = jnp.concatenate([jnp.real(wf), jnp.real(wb), -jnp.imag(wf), -jnp.imag(wb)], axis=4)
    emat = emat.transpose(0, 2, 4, 1, 3).reshape(depth, S5_GROUPS, 4 * S5_STATE, L * S5_GROUP)
    al = apow[:, L]
    a16 = jnp.stack([jnp.concatenate([jnp.real(al[:, 0]), jnp.real(al[:, 1])], axis=-1),
                     jnp.concatenate([jnp.imag(al[:, 0]), jnp.imag(al[:, 1])], axis=-1)], axis=1)
    tr = lambda w: jnp.swapaxes(w, -1, -2).astype(BF16)
    return mt, tr(fmat), tr(emat), a16.astype(F32)


def _attn_kernel(*refs, grp, bq, sub, has_sink, has_seq, window, t_seq, bk):
    refs = list(refs)
    sink_ref = refs.pop(0) if has_sink else None
    q_ref, kc_ref, vc_ref = refs[:3]
    ks_ref, vs_ref = (refs[3], refs[4]) if has_seq else (None, None)
    o_ref = refs[-1]
    hk = pl.program_id(1)
    qi = pl.program_id(2)

    def scores(q, k, mask=None):
        s = lax.dot_general(q, k, (((1,), (1,)), ((), ())), preferred_element_type=F32)
        return s if mask is None else jnp.where(mask, s, NEG_INF)

    band_cache = {}

    def col_minus_row(rows, wlen, nq):
        if (rows, wlen, nq) not in band_cache:
            band_cache[rows, wlen, nq] = (lax.broadcasted_iota(jnp.int32, (rows, wlen), 1)
                                          - lax.broadcasted_iota(jnp.int32, (rows, wlen), 0) % nq)
        return band_cache[rows, wlen, nq]

    def logits(r0, nq):
        rows = grp * nq
        q = jnp.concatenate([q_ref[r0:r0 + nq, g * LANE:(g + 1) * LANE] for g in range(grp)], axis=0)
        s_parts, v_parts = [scores(q, kc_ref[...])], [vc_ref[...]]
        if has_seq and window is not None:
            wlen = nq + 2 * window
            q0 = qi * bq + r0
            start = pl.multiple_of(jnp.clip(q0 - window, 0, t_seq - wlen), LANE)
            in_band = jnp.abs(col_minus_row(rows, wlen, nq) + (start - q0)) <= window
            s_parts.append(scores(q, ks_ref[pl.ds(start, wlen), :], in_band))
            v_parts.append(vs_ref[pl.ds(start, wlen), :])
        elif has_seq:
            s_parts.append(scores(q, ks_ref[0:bk, :]))
            v_parts.append(vs_ref[0:bk, :])
        return q, s_parts, v_parts

    def softmax_pv(nq, s_parts, v_parts):
        m = functools.reduce(jnp.maximum, [jnp.max(s, axis=-1, keepdims=True) for s in s_parts])
        if has_sink:
            sink = jnp.concatenate([jnp.full((nq, 1), sink_ref[hk * grp + g], F32) for g in range(grp)], axis=0)
            m = jnp.maximum(m, sink)
        acc = sum(jnp.dot(jnp.exp2(s - m).astype(BF16), v, preferred_element_type=F32)
                  for s, v in zip(s_parts, v_parts))
        if has_sink:
            one_lane = lax.broadcasted_iota(jnp.int32, (1, LANE), 1) == V_ONE
            acc = acc + jnp.where(one_lane, jnp.exp2(sink - m), 0.0)
        return m, acc

    def finish(r0, nq, acc):
        o = acc * (1.0 / acc[:, V_ONE:V_ONE + 1])
        for g in range(grp):
            o_ref[r0:r0 + nq, g * LANE:(g + 1) * LANE] = o[g * nq:(g + 1) * nq].astype(o_ref.dtype)

    if has_seq and window is not None:
        for r0 in range(0, bq, sub):
            _, s_parts, v_parts = logits(r0, sub)
            finish(r0, sub, softmax_pv(sub, s_parts, v_parts)[1])
        return
    q, s_parts, v_parts = logits(0, bq)
    m, acc = softmax_pv(bq, s_parts, v_parts)
    if has_seq:
        def body(j, carry):
            m, acc = carry
            st = pl.multiple_of(j * bk, bk)
            s = scores(q, ks_ref[pl.ds(st, bk), :])
            m_new = jnp.maximum(m, jnp.max(s, axis=-1, keepdims=True))
            p = jnp.exp2(s - m_new).astype(BF16)
            acc = jnp.exp2(m - m_new) * acc + jnp.dot(p, vs_ref[pl.ds(st, bk), :], preferred_element_type=F32)
            return m_new, acc
        m, acc = lax.fori_loop(1, t_seq // bk, body, (m, acc), unroll=True)
    finish(0, bq, acc)


def _attention(q, kc, vc, ks=None, vs=None, *, grp, sink=None, window=None, bq=TILES['attn_q'], bk=TILES['mla_k']):
    b, tq, wq = q.shape
    hkv = wq // (grp * LANE)
    lc = kc.shape[1]
    has_seq = ks is not None
    t_seq = ks.shape[1] if has_seq else 0
    bq = _tile(tq, bq)
    sub = min(bq, TILES['swa_sub'])
    if has_seq and window is None:
        bk = _tile(t_seq, bk)
    kern = functools.partial(_attn_kernel, grp=grp, bq=bq, sub=sub, has_sink=sink is not None, has_seq=has_seq,
                             window=window, t_seq=t_seq, bk=bk)
    in_specs, args = [], []
    if sink is not None:
        in_specs.append(pl.BlockSpec(memory_space=pltpu.SMEM))
        args.append(sink)
    in_specs.append(pl.BlockSpec((None, bq, grp * LANE), lambda bb, h, i: (bb, i, h)))
    args.append(q)
    ctx_spec = pl.BlockSpec((None, lc, LANE), lambda bb, h, i: (bb, 0, h))
    in_specs += [ctx_spec, ctx_spec]
    args += [kc, vc]
    if has_seq:
        seq_spec = pl.BlockSpec((None, t_seq, LANE), lambda bb, h, i: (bb, 0, h))
        in_specs += [seq_spec, seq_spec]
        args += [ks, vs]
    return pl.pallas_call(
        kern,
        out_shape=jax.ShapeDtypeStruct((b, tq, wq), BF16),
        grid=(b, hkv, tq // bq),
        in_specs=in_specs,
        out_specs=pl.BlockSpec((None, bq, grp * LANE), lambda bb, h, i: (bb, i, h)),
        compiler_params=_params(("parallel", "parallel", "parallel")),
        name="attention",
    )(*args)


def _split3(x):
    hi = x.astype(BF16)
    r = x - hi.astype(F32)
    mid = r.astype(BF16)
    lo = (r - mid.astype(F32)).astype(BF16)
    return hi, mid, lo


def _hg_kernel(gp_ref, qf_ref, zf_ref, vf_ref, qb_ref, zb_ref, vb_ref, s0f_ref, s0b_ref,
               of_ref, ob_ref, sf_ref, sb_ref, st_ref, *, nchunk):
    i = pl.program_id(1)
    c = HG_CHUNK
    w = HG_HEADS * HG_DK

    @pl.when(i == 0)
    def _():
        st_ref[0] = s0f_ref[...]
        st_ref[1] = s0b_ref[...]

    ri = lax.broadcasted_iota(jnp.int32, (c, c), 0)
    ci = lax.broadcasted_iota(jnp.int32, (c, c), 1)
    ti = lax.broadcasted_iota(jnp.int32, (c, HG_HEADS * c), 0)
    si = lax.broadcasted_iota(jnp.int32, (c, HG_HEADS * c), 1) % c
    lane = lax.broadcasted_iota(jnp.int32, (1, w), 1)
    hmask = [(lane // HG_DK) == h for h in range(HG_HEADS)]
    r2 = lax.broadcasted_iota(jnp.int32, (w, w), 0) // HG_DV
    c2 = lax.broadcasted_iota(jnp.int32, (w, w), 1) // HG_DK
    diag = r2 == c2
    dirs = {
        True: dict(q=qf_ref, z=zf_ref, v=vf_ref, o=of_ref, gp=tuple(gp_ref[r:r + 1, :] for r in range(3)),
                   tri=jnp.where(ci <= ri, 1.0, 0.0).astype(BF16), causal=si <= ti),
        False: dict(q=qb_ref, z=zb_ref, v=vb_ref, o=ob_ref, gp=tuple(gp_ref[r:r + 1, :] for r in range(3, 6)),
                    tri=jnp.where(ci >= ri, 1.0, 0.0).astype(BF16), causal=si >= ti),
    }
    units = []
    for n in range(nchunk):
        units += [(True, n), (False, nchunk - 1 - n)]

    ph1 = []
    for fwd, n in units:
        dd = dirs[fwd]
        sl = slice(n * c, (n + 1) * c)
        z = dd['z'][sl, :]
        log_lb, l1p, om = dd['gp']
        ls = jnp.minimum(z, 0.0) - jnp.log(1.0 + jnp.exp(-jnp.abs(z)))
        bterm = l1p + ls
        lf = jnp.maximum(log_lb, bterm) + jnp.log(1.0 + jnp.exp(-jnp.abs(log_lb - bterm)))
        k = om / (1.0 + jnp.exp(z))
        gsum = sum(jnp.dot(dd['tri'], part, preferred_element_type=F32) for part in _split3(lf))
        ph1.append((k, gsum))

    ph2 = []
    for (fwd, n), (k, gsum) in zip(units, ph1):
        dd = dirs[fwd]
        sl = slice(n * c, (n + 1) * c)
        q, v = dd['q'][sl, :], dd['v'][sl, :]
        tot = gsum[c - 1:c, :] if fwd else gsum[0:1, :]
        gm = gsum[c // 2:c // 2 + 1, :]
        qd = (q * jnp.exp(gsum - gm)).astype(BF16)
        kinv = k * jnp.exp(gm - gsum)
        q_in = (q * jnp.exp(gsum)).astype(BF16)
        k_end = (k * jnp.exp(tot - gsum)).astype(BF16)
        kstack = jnp.concatenate([jnp.where(hm, kinv, 0.0) for hm in hmask], axis=0).astype(BF16)
        vstack = jnp.concatenate([jnp.where(hm, v, 0.0) for hm in hmask], axis=0).astype(BF16)
        ph2.append((qd, kstack, vstack, q_in, k_end, v.astype(BF16), jnp.exp(tot)))

    ph3 = []
    for (fwd, n), (qd, kstack, vstack, q_in, k_end, vb, dec) in zip(units, ph2):
        att = lax.dot_general(qd, kstack, (((1,), (1,)), ((), ())), preferred_element_type=F32)
        att = jnp.where(dirs[fwd]['causal'], att, 0.0).astype(BF16)
        o_intra = jnp.dot(att, vstack, preferred_element_type=F32)
        kv_t = lax.dot_general(vb, k_end, (((0,), (0,)), ((), ())), preferred_element_type=F32)
        ph3.append((o_intra, jnp.where(diag, kv_t, 0.0)))

    st = {True: st_ref[0], False: st_ref[1]}
    for (fwd, n), (_, _, _, q_in, _, _, dec), (o_intra, kv_t) in zip(units, ph2, ph3):
        sl = slice(n * c, (n + 1) * c)
        o_inter = lax.dot_general(q_in, st[fwd].astype(BF16), (((1,), (1,)), ((), ())), preferred_element_type=F32)
        dirs[fwd]['o'][sl, :] = o_intra + o_inter
        st[fwd] = st[fwd] * dec + kv_t
    st_ref[0] = st[True]
    st_ref[1] = st[False]

    @pl.when(i == pl.num_programs(1) - 1)
    def _():
        sf_ref[...] = st[True]
        sb_ref[...] = st[False]


def _hgrn(q, zf, zb, v, gp, s0f, s0b):
    b, t, w = q.shape
    tb = _tile(t, TILES['hgrn'])
    nblk = t // tb
    fw = pl.BlockSpec((None, tb, w), lambda bb, i: (bb, i, 0))
    bw = pl.BlockSpec((None, tb, w), lambda bb, i: (bb, nblk - 1 - i, 0))
    st = pl.BlockSpec((None, w, w), lambda bb, i: (bb, 0, 0))
    return pl.pallas_call(
        functools.partial(_hg_kernel, nchunk=tb // HG_CHUNK),
        out_shape=(jax.ShapeDtypeStruct((b, t, w), F32), jax.ShapeDtypeStruct((b, t, w), F32),
                   jax.ShapeDtypeStruct((b, w, w), F32), jax.ShapeDtypeStruct((b, w, w), F32)),
        grid=(b, nblk),
        in_specs=[_full((8, w)), fw, fw, fw, bw, bw, bw, st, st],
        out_specs=(fw, bw, st, st),
        scratch_shapes=[pltpu.VMEM((2, w, w), F32)],
        compiler_params=_params(("parallel", "arbitrary")),
        name="hgrn2",
    )(gp, q, zf, v, q, zb, v, s0f, s0b)


def _gelu_tanh(x):
    return 0.5 * x * (1.0 + jnp.tanh(math.sqrt(2.0 / math.pi) * (x + 0.044715 * (x * x * x))))


def _outproj_kernel(x_ref, g1_ref, ya_ref, yb_ref, of_ref, ob_ref, hg_ref, yd_ref,
                    wglu_ref, bglu_ref, hn_ref, pavg_ref, wo_ref, o_ref, ys_ref):
    for j in range(S5_CHUNK):
        for v in range(2):
            ys_ref[v, pl.ds(j, ys_ref.shape[1] // S5_CHUNK, stride=S5_CHUNK), :] = ya_ref[j, :, v * LANE:(v + 1) * LANE]
    ya = _gelu_tanh(jnp.concatenate([ys_ref[0], ys_ref[1]], axis=1))
    gl = jnp.dot(ya.astype(BF16), wglu_ref[...], preferred_element_type=F32) + bglu_ref[...]
    o = of_ref[...] + ob_ref[...]
    o2 = o * o
    hi = o2.astype(BF16)
    lo = (o2 - hi.astype(F32)).astype(BF16)
    ms = (jnp.dot(hi, pavg_ref[...], preferred_element_type=F32)
          + jnp.dot(lo, pavg_ref[...], preferred_element_type=F32))
    y = (jnp.dot(yb_ref[...], wo_ref[256:768, :], preferred_element_type=F32)
         + jnp.dot(yd_ref[...], wo_ref[1024:1536, :], preferred_element_type=F32))
    ya = ya * (1.0 / (1.0 + jnp.exp(-gl)))
    gate = hg_ref[...]
    yc = o * lax.rsqrt(ms + EPS) * hn_ref[...] * (gate * (1.0 / (1.0 + jnp.exp(-gate))))
    y = (y + jnp.dot(ya.astype(BF16), wo_ref[0:256, :], preferred_element_type=F32)
         + jnp.dot(yc.astype(BF16), wo_ref[768:1024, :], preferred_element_type=F32))
    o_ref[...] = x_ref[...] + g1_ref[...] * y


def _outproj(x, g1, ya, yb, of, ob, hg, yd, wglu, bglu, hn, pavg, wo_p, li):
    b, t, d = x.shape
    tt = _tile(t, TILES['outproj'])
    xs = lambda n: pl.BlockSpec((None, tt, n), lambda bb, i: (bb, i, 0))
    vec = pl.BlockSpec((None, 1, d), lambda bb, i: (bb, 0, 0))
    return pl.pallas_call(
        _outproj_kernel,
        out_shape=jax.ShapeDtypeStruct((b, t, d), F32),
        grid=(b, t // tt),
        in_specs=[xs(d), vec, pl.BlockSpec((None, S5_CHUNK, tt // S5_CHUNK, 256), lambda bb, i: (bb, 0, i, 0)),
                  xs(512), xs(256), xs(256), xs(256), xs(512),
                  _layer(wglu, li), _full((1, 256)), _full((1, 256)), _full(pavg.shape), _layer(wo_p, li)],
        out_specs=xs(d),
        scratch_shapes=[pltpu.VMEM((2, tt, LANE), F32)],
        compiler_params=_params(("parallel", "parallel")),
        name="outproj",
    )(x, g1, ya, yb, of, ob, hg, yd, wglu, bglu, hn, pavg, wo_p)


def _ffn_kernel(x_ref, sc_ref, sh_ref, g2_ref, ng_ref, wup_ref, wdn_ref, fg_ref, o_ref, acc_ref, *, hidden, ck, final):
    x = x_ref[...]
    hb = (_rms(x, ng_ref[...]) * (1.0 + sc_ref[...]) + sh_ref[...]).astype(BF16)
    for j in range(hidden // ck):
        gate = jnp.dot(hb, wup_ref[:, j * ck:(j + 1) * ck], preferred_element_type=F32)
        up = jnp.dot(hb, wup_ref[:, hidden + j * ck:hidden + (j + 1) * ck], preferred_element_type=F32)
        a = (gate * (1.0 / (1.0 + jnp.exp(-gate))) * up).astype(BF16)
        part = jnp.dot(a, wdn_ref[j * ck:(j + 1) * ck, :], preferred_element_type=F32)
        if j == 0:
            acc_ref[...] = part
        else:
            acc_ref[...] += part
    y = x + g2_ref[...] * acc_ref[...]
    if final:
        y = _rms(y, fg_ref[...])
    o_ref[...] = y


def _ffn(x, sc, sh, g2, ng, wup, wdn, fg, final, li):
    b, t, d = x.shape
    hidden = wdn.shape[1]
    tt = _tile(t, TILES['ffn'])
    xs = pl.BlockSpec((None, tt, d), lambda bb, i: (bb, i, 0))
    vec = pl.BlockSpec((None, 1, d), lambda bb, i: (bb, 0, 0))
    return pl.pallas_call(
        functools.partial(_ffn_kernel, hidden=hidden, ck=_tile(hidden, TILES['ffn_hidden']), final=final),
        out_shape=jax.ShapeDtypeStruct((b, t, d), F32),
        grid=(b, t // tt),
        in_specs=[xs, vec, vec, vec, _full((1, d)), _layer(wup, li), _layer(wdn, li), _full((1, d))],
        out_specs=xs,
        scratch_shapes=[pltpu.VMEM((tt, d), F32)],
        compiler_params=_params(("parallel", "parallel")),
        name="ffn",
    )(x, sc, sh, g2, ng, wup, wdn, fg)


def _pad_heads(w, heads, dim):
    w = w.reshape(w.shape[:-1] + (heads, dim))
    w = jnp.pad(w, [(0, 0)] * (w.ndim - 1) + [(0, LANE - dim)])
    return w.reshape(w.shape[:-2] + (heads * LANE,))


def _layer_weights(w_in, w_out, mla_w_qb, mla_w_kvb):
    depth, d, n_in = w_in.shape
    qscale = jnp.ones((n_in,), F32).at[_C_SQ:_C_SQ + SWA_HEADS * SWA_HEAD_DIM].set(SWA_HEAD_DIM ** -0.5 * LOG2E)
    w_in_p = jnp.pad(w_in * qscale, ((0, 0), (0, 0), (0, _N_INP - n_in))).astype(BF16)
    wqb_p = _pad_heads(mla_w_qb * (MLA_SCALE * LOG2E), MLA_HEADS, MLA_NOPE + MLA_ROPE).astype(BF16)
    kvb = mla_w_kvb.reshape(depth, MLA_KV_RANK, MLA_HEADS, MLA_NOPE + MLA_V)
    wk = _pad_heads(kvb[..., :MLA_NOPE].reshape(depth, MLA_KV_RANK, -1), MLA_HEADS, MLA_NOPE)
    wv = _pad_heads(kvb[..., MLA_NOPE:].reshape(depth, MLA_KV_RANK, -1), MLA_HEADS, MLA_V)
    wkv_p = jnp.concatenate([wk, wv], axis=-1).astype(BF16)
    assert w_out.shape[1] == 4 * 256
    pad_rows = lambda w, heads, dim: jnp.swapaxes(_pad_heads(jnp.swapaxes(w, -1, -2), heads, dim), -1, -2)
    wo_p = jnp.concatenate([w_out[:, 0:256], pad_rows(w_out[:, 256:512], SWA_HEADS, SWA_HEAD_DIM),
                            w_out[:, 512:768], pad_rows(w_out[:, 768:1024], MLA_HEADS, MLA_V)], axis=1).astype(BF16)
    return w_in_p, wqb_p, wkv_p, wo_p


def _rope_tables(length, dim, lo, ident, repeat=1):
    n_freq = dim // 4
    rows = length // GRID_W
    row = jnp.repeat(jnp.arange(rows, dtype=F32), GRID_W)
    col = jnp.tile(jnp.arange(GRID_W, dtype=F32), rows)
    inv = ROPE_BASE ** (-jnp.arange(n_freq, dtype=F32) / n_freq)
    ang = jnp.stack([row[:, None] * inv, col[:, None] * inv], axis=1)
    cos, sin = jnp.cos(ang), jnp.sin(ang)
    z = jnp.zeros_like(sin)
    cos_l = jnp.stack([cos, cos], axis=2).reshape(length, dim)
    sina = jnp.stack([-sin, z], axis=2).reshape(length, dim)
    sinb = jnp.stack([z, sin], axis=2).reshape(length, dim)
    if ident:
        cos_l, sina, sinb = jnp.ones_like(cos_l), jnp.zeros_like(sina), jnp.zeros_like(sinb)
    cos_l, sina, sinb = (jnp.tile(a, (1, repeat)) for a in (cos_l, sina, sinb))
    pad = lambda a, fill: jnp.pad(a, ((0, 0), (lo, LANE - lo - dim * repeat)), constant_values=fill)
    return jnp.stack([pad(cos_l, 1.0), pad(sina, 0.0), pad(sinb, 0.0)], axis=0)


def kernel(x, c, ctx, c_ctx, w_mod, b_mod, norm1_g, norm2_g, w_in, w_out, s5_lam_re, s5_lam_im, s5_log_dt,
           s5_b_re, s5_b_im, s5_c_re, s5_c_im, s5_d, s5_w_glu, s5_b_glu, swa_sink, hg_lb, hg_norm_g,
           mla_q_norm_g, mla_w_qb, mla_kv_norm_g, mla_w_kvb, ffn_w_up, ffn_w_down, final_norm_g):
    b, t, d = x.shape
    lc = ctx.shape[1]
    depth = w_mod.shape[0]

    cc = jnp.zeros((8, d), F32).at[:b].set(c).at[b].set(c_ctx)
    mods = _modulation(cc, w_mod, b_mod)

    lb_cum = jnp.cumsum(jax.nn.softmax(hg_lb.astype(F32), axis=1), axis=1)
    lb = lb_cum - lb_cum[:, :1]

    tab_swa = _rope_tables(t, SWA_HEAD_DIM, 0, False, repeat=2)
    tab_mla = _rope_tables(t, MLA_ROPE, MLA_NOPE, False)
    tab_swa_c = _rope_tables(lc, SWA_HEAD_DIM, 0, True, repeat=2)
    tab_mla_c = _rope_tables(lc, MLA_ROPE, MLA_NOPE, True)
    pavg = jnp.kron(jnp.eye(HG_HEADS, dtype=F32), jnp.full((HG_DV, HG_DV), 1.0 / HG_DV, F32)).astype(BF16)
    zero_st = jnp.zeros((b, HG_HEADS * HG_DV, HG_HEADS * HG_DK), F32)
    zero_s5 = jnp.zeros((b, S5_GROUPS, 4 * S5_STATE), F32)

    w_in_p, wqb_p, wkv_p, wo_p = _layer_weights(w_in, w_out, mla_w_qb, mla_w_kvb)
    mats = _s5_matrices(s5_lam_re, s5_lam_im, s5_log_dt, s5_b_re, s5_b_im, s5_c_re, s5_c_im, s5_d)
    wglu = s5_w_glu.astype(BF16)
    wup = ffn_w_up.astype(BF16)
    wdn = ffn_w_down.astype(BF16)
    fg = final_norm_g.reshape(1, d)

    for i in range(depth):
        need_ctx = i < depth - 1
        mod = mods[i, :b].reshape(b, 6, 1, d)
        mod_c = jnp.broadcast_to(mods[i, b].reshape(1, 6, 1, d), (b, 6, 1, d))
        sh1, sc1, g1, sh2, sc2, g2 = (mod[:, j] for j in range(6))
        csh1, csc1, cg1, csh2, csc2, cg2 = (mod_c[:, j] for j in range(6))
        n1 = norm1_g[i].reshape(1, d)
        qg = mla_q_norm_g[i].reshape(1, -1)
        kvg = mla_kv_norm_g[i].reshape(1, -1)
        px = _inproj(x, sc1, sh1, n1, w_in_p, tab_swa, tab_mla, qg, wqb_p, kvg, wkv_p, i)
        pc = _inproj(ctx, csc1, csh1, n1, w_in_p, tab_swa_c, tab_mla_c, qg, wqb_p, kvg, wkv_p, i)
        (xu, xsq, xsk, xsv, xhq, xhzf, xhzb, xhi, xhg, xmq, xmk, xmv) = px
        (cu, csq, csk, csv, chq, chzf, chzb, chi, chg, cmq, cmk, cmv) = pc

        ya_c, s5_fin = _s5_scan(cu, mats, zero_s5, i)
        ya, _ = _s5_scan(xu, mats, s5_fin, i)

        grp = SWA_HEADS // SWA_KV_HEADS
        sink = swa_sink[i].astype(F32) * LOG2E
        yb = _attention(xsq, csk, csv, xsk, xsv, grp=grp, sink=sink, window=SWA_WINDOW, bq=TILES['swa_q'])
        lbf, lbb = lb[0, i], lb[1, i]
        gp = jnp.stack([jnp.log(lbf), jnp.log1p(-lbf), 1.0 - lbf,
                        jnp.log(lbb), jnp.log1p(-lbb), 1.0 - lbb, lbf, lbb], axis=0)
        of_c, ob_c, stf, stb = _hgrn(chq, chzf, chzb, chi, gp, zero_st, zero_st)
        of, ob, _, _ = _hgrn(xhq, xhzf, xhzb, xhi, gp, stf, stb)
        yd = _attention(xmq, cmk, cmv, xmk, xmv, grp=1, bq=TILES['mla_q'])

        bglu = s5_b_glu[i].reshape(1, -1)
        hn = jnp.tile(hg_norm_g[i], HG_HEADS).reshape(1, -1)
        x = _outproj(x, g1, ya, yb, of, ob, xhg, yd, wglu, bglu, hn, pavg, wo_p, i)
        n2 = norm2_g[i].reshape(1, d)
        x = _ffn(x, sc2, sh2, g2, n2, wup, wdn, fg, not need_ctx, i)
        if need_ctx:
            yb_c = _attention(csq, csk, csv, grp=grp, sink=sink)
            yd_c = _attention(cmq, cmk, cmv, grp=1)
            ctx = _outproj(ctx, cg1, ya_c, yb_c, of_c, ob_c, chg, yd_c, wglu, bglu, hn, pavg, wo_p, i)
            ctx = _ffn(ctx, csc2, csh2, cg2, n2, wup, wdn, fg, False, i)
    return x
```

```python
import functools
import math

import jax
import jax.numpy as jnp
from jax import lax
from jax.experimental import pallas as pl
from jax.experimental.pallas import tpu as pltpu

F32 = jnp.float32
BF16 = jnp.bfloat16

EPS = 1e-6
NEG_INF = -1e30
ROPE_BASE = 10000.0
GRID_W = 64
LANE = 128
VMEM_V7X = 64 * 1024 * 1024
VMEM_LIMIT = VMEM_V7X - 8 * 1024 * 1024

S5_CH, S5_GROUP, S5_STATE = 256, 16, 64
S5_GROUPS = S5_CH // S5_GROUP
S5_CHUNK = 16
SWA_HEADS, SWA_KV_HEADS, SWA_HEAD_DIM, SWA_WINDOW = 4, 2, 64, 128
HG_HEADS, HG_DK, HG_DV = 4, 64, 64
HG_CHUNK = 64
MLA_HEADS, MLA_Q_RANK, MLA_KV_RANK = 4, 256, 128
MLA_NOPE, MLA_ROPE, MLA_V = 64, 32, 64
MLA_SCALE = (MLA_NOPE + MLA_ROPE) ** -0.5
LOG2E = 1.4426950408889634
V_ONE = 64


TILES = dict(
    mod_cols=1536,
    inproj=1024,
    hgrn=1024,
    outproj=512,
    ffn=512,
    ffn_hidden=256,
    attn_q=512,
    swa_q=1024,
    swa_sub=LANE,
    mla_q=1024,
    mla_k=2048,
)
S5_SCAN_UNROLL = 8


def _tile(n, pref):
    t = min(n, pref)
    assert n % t == 0, (n, pref)
    return t


def _params(sem):
    return pltpu.CompilerParams(dimension_semantics=sem, vmem_limit_bytes=VMEM_LIMIT)


def _full(shape):
    nd = len(shape)
    return pl.BlockSpec(shape, lambda *_: (0,) * nd, pipeline_mode=pl.Buffered(1))


def _layer(w, li):
    nd = w.ndim - 1
    return pl.BlockSpec((None,) + w.shape[1:], lambda *_: (li,) + (0,) * nd, pipeline_mode=pl.Buffered(1))


def _mod_kernel(c_ref, w_ref, b_ref, o_ref):
    c = c_ref[...]
    s = c * (1.0 / (1.0 + jnp.exp(-c)))
    o_ref[...] = jnp.dot(s.astype(BF16), w_ref[...].astype(BF16), preferred_element_type=F32) + b_ref[...]


def _modulation(cc, w_mod, b_mod):
    depth, d, n = w_mod.shape
    tn = _tile(n, TILES['mod_cols'])
    return pl.pallas_call(
        _mod_kernel,
        out_shape=jax.ShapeDtypeStruct((depth, 8, n), F32),
        grid=(depth, n // tn),
        in_specs=[pl.BlockSpec((8, d), lambda l, j: (0, 0)),
                  pl.BlockSpec((None, d, tn), lambda l, j: (l, 0, j)),
                  pl.BlockSpec((None, 1, tn), lambda l, j: (l, 0, j))],
        out_specs=pl.BlockSpec((None, 8, tn), lambda l, j: (l, 0, j)),
        compiler_params=_params(("parallel", "parallel")),
        name="modulation",
    )(cc, w_mod, b_mod.reshape(depth, 1, n))


def _rope_block(x, t_ref, half):
    return (x * t_ref[0] + pltpu.roll(x, LANE - half, 1) * t_ref[1]
            + pltpu.roll(x, half, 1) * t_ref[2])


def _rms(x, g):
    return x * lax.rsqrt(jnp.mean(x * x, axis=-1, keepdims=True) + EPS) * g


_C_U = 0
_C_SQ = 256
_C_SKV = 512
_C_HG = 768
_C_CQ = _C_HG + 5 * 256
_C_CKV = _C_CQ + 256
_N_INP = _C_CKV + 256


def _inproj_kernel(x_ref, sc_ref, sh_ref, g_ref, w_ref, ts_ref, tm_ref, qg_ref, wqb_ref, kvg_ref, wkv_ref,
                   u_ref, sq_ref, sk_ref, sv_ref, hq_ref, hzf_ref, hzb_ref, hi_ref, hg_ref,
                   mq_ref, mk_ref, mv_ref, us_ref):
    h = _rms(x_ref[...], g_ref[...]) * (1.0 + sc_ref[...]) + sh_ref[...]
    hb = h.astype(BF16)

    def proj(lo, n):
        return jnp.dot(hb, w_ref[:, lo:lo + n], preferred_element_type=F32)

    cq = _rms(proj(_C_CQ, 256), qg_ref[...]).astype(BF16)
    ckv_kr = proj(_C_CKV, 256)
    ckv = _rms(ckv_kr[:, :LANE], kvg_ref[...]).astype(BF16)
    kr = pltpu.roll(ckv_kr[:, LANE:], MLA_NOPE, 1)

    u = proj(_C_U, 256)
    for v in range(2):
        us_ref[v] = u[:, v * LANE:(v + 1) * LANE]
    for j in range(S5_CHUNK):
        for v in range(2):
            u_ref[j, :, v * LANE:(v + 1) * LANE] = us_ref[v, pl.ds(j, us_ref.shape[1] // S5_CHUNK, stride=S5_CHUNK), :]
    lane = lax.broadcasted_iota(jnp.int32, (1, LANE), 1)
    low = lane < SWA_HEAD_DIM
    one = jnp.where(lane == V_ONE, 1.0, 0.0)

    def spread(pair, fill):
        return [jnp.where(low, blk, fill).astype(BF16) for blk in (pair, pltpu.roll(pair, SWA_HEAD_DIM, 1))]

    sq = proj(_C_SQ, 256)
    for pp in range(SWA_HEADS // 2):
        heads = spread(_rope_block(sq[:, pp * LANE:(pp + 1) * LANE], ts_ref, 16), 0.0)
        for e in range(2):
            sq_ref[:, (2 * pp + e) * LANE:(2 * pp + e + 1) * LANE] = heads[e]
    skv = proj(_C_SKV, 256)
    for e, blk in enumerate(spread(_rope_block(skv[:, :LANE], ts_ref, 16), 0.0)):
        sk_ref[:, e * LANE:(e + 1) * LANE] = blk
    for e, blk in enumerate(spread(skv[:, LANE:], one)):
        sv_ref[:, e * LANE:(e + 1) * LANE] = blk

    q = jnp.dot(cq, wqb_ref[...], preferred_element_type=F32)
    for hh in range(MLA_HEADS):
        sl = slice(hh * LANE, (hh + 1) * LANE)
        mq_ref[:, sl] = _rope_block(q[:, sl], tm_ref, 8).astype(BF16)
    for i, r in enumerate((hq_ref, hzf_ref, hzb_ref)):
        r[...] = proj(_C_HG + i * 256, 256)
    kv = jnp.dot(ckv, wkv_ref[...], preferred_element_type=F32)
    for hh in range(MLA_HEADS):
        sl = slice(hh * LANE, (hh + 1) * LANE)
        mk_ref[:, sl] = _rope_block(kv[:, sl] + kr, tm_ref, 8).astype(BF16)
        mv_ref[:, sl] = jnp.where(lane == V_ONE, 1.0, kv[:, MLA_HEADS * LANE + hh * LANE:][:, :LANE]).astype(BF16)
    for i, r in enumerate((hi_ref, hg_ref)):
        r[...] = proj(_C_HG + (3 + i) * 256, 256).astype(r.dtype)


def _inproj(x, sc, sh, g, w_in_p, tab_swa, tab_mla, qg, wqb_p, kvg, wkv_p, li):
    b, t, d = x.shape
    tt = _tile(t, TILES['inproj'])
    row = lambda n, dt: jax.ShapeDtypeStruct((b, t, n), dt)
    out_shape = (jax.ShapeDtypeStruct((b, S5_CHUNK, t // S5_CHUNK, 256), F32),
                 row(512, BF16), row(256, BF16), row(256, BF16),
                 row(256, F32), row(256, F32), row(256, F32), row(256, F32), row(256, BF16),
                 row(512, BF16), row(512, BF16), row(512, BF16))
    xs = lambda n: pl.BlockSpec((None, tt, n), lambda i, bb: (bb, i, 0))
    vec = pl.BlockSpec((None, 1, d), lambda i, bb: (bb, 0, 0))
    tab = pl.BlockSpec((3, tt, LANE), lambda i, bb: (0, i, 0))
    return pl.pallas_call(
        _inproj_kernel,
        out_shape=out_shape,
        grid=(t // tt, b),
        in_specs=[xs(d), vec, vec, _full((1, d)), _layer(w_in_p, li), tab, tab,
                  _full((1, 256)), _layer(wqb_p, li), _full((1, 128)), _layer(wkv_p, li)],
        out_specs=(pl.BlockSpec((None, S5_CHUNK, tt // S5_CHUNK, 256), lambda i, bb: (bb, 0, i, 0)),)
        + tuple(xs(s.shape[-1]) for s in out_shape[1:]),
        scratch_shapes=[pltpu.VMEM((2, tt, LANE), F32)],
        compiler_params=_params(("parallel", "parallel")),
        name="inproj",
    )(x, sc, sh, g, w_in_p, tab_swa, tab_mla, qg, wqb_p, kvg, wkv_p)


def _s5_kernel(u_ref, mt_ref, ft_ref, et_ref, a_ref, s0_ref, y_ref, sfin_ref, zs_ref, ut_ref, yt_ref, *, nc):
    ng, half = S5_GROUPS, LANE // S5_GROUP
    for j in range(S5_CHUNK):
        for v in range(2):
            t = u_ref[j, :, v * LANE:(v + 1) * LANE].T
            for gl in range(half):
                ut_ref[half * v + gl, S5_GROUP * j:S5_GROUP * (j + 1), :] = (
                    t[S5_GROUP * gl:S5_GROUP * (gl + 1), :].astype(BF16))
    for g in range(ng):
        z = jnp.dot(ft_ref[g], ut_ref[g], preferred_element_type=F32).T
        for c in range(2):
            zs_ref[c, pl.ds(g, nc, stride=ng), :] = z[:, c * LANE:(c + 1) * LANE]

    fmask = lax.broadcasted_iota(jnp.int32, (1, LANE), 1) < S5_STATE
    ar, ai = a_ref[0], a_ref[1]

    def step(i, s):
        rf = pl.multiple_of(i * ng, ng)
        rb = pl.multiple_of((nc - 1 - i) * ng, ng)
        zre, zim = (jnp.where(fmask, zs_ref[c, pl.ds(rf, ng), :], zs_ref[c, pl.ds(rb, ng), :]) for c in range(2))
        for c in range(2):
            zs_ref[c, pl.ds(rf, ng), 0:S5_STATE] = s[c][:, 0:S5_STATE]
            zs_ref[c, pl.ds(rb, ng), S5_STATE:LANE] = s[c][:, S5_STATE:LANE]
        re, im = s
        return ar * re - ai * im + zre, ar * im + ai * re + zim

    re, im = lax.fori_loop(0, nc, step, (s0_ref[:, :LANE], s0_ref[:, LANE:]), unroll=S5_SCAN_UNROLL)
    sfin_ref[:, :LANE] = re
    sfin_ref[:, LANE:] = im

    for g in range(ng):
        ss = jnp.concatenate([zs_ref[c, pl.ds(g, nc, stride=ng), :] for c in range(2)], axis=1)
        yt = (jnp.dot(mt_ref[g], ut_ref[g], preferred_element_type=F32)
              + jnp.dot(et_ref[g], ss.T.astype(BF16), preferred_element_type=F32))
        for t in range(S5_CHUNK):
            yt_ref[t, S5_GROUP * g:S5_GROUP * (g + 1), :] = yt[S5_GROUP * t:S5_GROUP * (t + 1), :]
    for t in range(S5_CHUNK):
        y_ref[t] = yt_ref[t].T


def _s5_scan(uj, mats, s0, li):
    mt, ft, et, a16 = mats
    b, _, nc, _ = uj.shape
    slab = pl.BlockSpec((None, S5_CHUNK, nc, 256), lambda bb: (bb, 0, 0, 0), pipeline_mode=pl.Buffered(1))
    st = pl.BlockSpec((None, S5_GROUPS, 256), lambda bb: (bb, 0, 0))
    return pl.pallas_call(
        functools.partial(_s5_kernel, nc=nc),
        out_shape=(jax.ShapeDtypeStruct(uj.shape, F32), jax.ShapeDtypeStruct((b, S5_GROUPS, 256), F32)),
        grid=(b,),
        in_specs=[slab, _layer(mt, li), _layer(ft, li), _layer(et, li), _layer(a16, li), st],
        out_specs=(slab, st),
        scratch_shapes=[pltpu.VMEM((2, S5_GROUPS * nc, LANE), F32), pltpu.VMEM((S5_GROUPS, 256, nc), BF16),
                        pltpu.VMEM((S5_CHUNK, 256, nc), F32)],
        compiler_params=_params(("parallel",)),
        name="s5",
    )(uj, mt, ft, et, a16, s0)


def _toeplitz_kernel(w_ref, o_ref):
    w = w_ref[...]
    for t in range(S5_CHUNK):
        lo = S5_GROUP * (S5_CHUNK - 1 - t)
        o_ref[S5_GROUP * t:S5_GROUP * (t + 1), :] = w[:, lo:lo + S5_CHUNK * S5_GROUP].astype(BF16)


def _toeplitz(strip):
    depth, ng, rows, width = strip.shape
    n = S5_CHUNK * S5_GROUP
    return pl.pallas_call(
        _toeplitz_kernel,
        out_shape=jax.ShapeDtypeStruct((depth, ng, n, n), BF16),
        grid=(depth, ng),
        in_specs=[pl.BlockSpec((None, None, rows, width), lambda d, g: (d, g, 0, 0))],
        out_specs=pl.BlockSpec((None, None, n, n), lambda d, g: (d, g, 0, 0)),
        compiler_params=_params(("parallel", "parallel")),
        name="toeplitz",
    )(strip)


def _s5_matrices(lam_re, lam_im, log_dt, b_re, b_im, c_re, c_im, d_skip):
    L = S5_CHUNK
    depth = lam_re.shape[0]
    hp = lax.Precision.HIGHEST
    lam = lax.complex(lam_re, lam_im)
    ldt = lam * jnp.exp(log_dt)
    bb = ((jnp.exp(ldt) - 1.0) / lam)[..., None] * lax.complex(b_re, b_im)
    cc = lax.complex(c_re, c_im)
    k = jnp.arange(L + 1, dtype=F32)
    apow = jnp.exp(ldt[:, None] * k[None, :, None, None, None])
    kern = jnp.real(jnp.einsum('ndghp,nkdgp,ndgpi->ndkgih', cc, apow[:, :L], bb, precision=hp))
    idx = jnp.arange(L)
    dg = d_skip.reshape(depth, 1, S5_GROUPS, 1, S5_GROUP) * jnp.eye(S5_GROUP, dtype=F32)[None, None, None]
    lag0 = kern[:, 0, :1] + kern[:, 1, :1] + dg
    by_lag = jnp.concatenate([kern[:, 0, :0:-1], lag0, kern[:, 1, 1:]], axis=1)
    strip = by_lag.transpose(0, 2, 4, 1, 3).reshape(depth, S5_GROUPS, S5_GROUP, (2 * L - 1) * S5_GROUP)
    mt = _toeplitz(jnp.pad(strip, ((0, 0), (0, 0), (0, 0), (0, S5_GROUP))))
    zf = apow[:, L - 1 - idx, 0][..., None] * bb[:, 0][:, None]
    zb = apow[:, idx, 1][..., None] * bb[:, 1][:, None]
    fmat = jnp.concatenate([jnp.real(zf), jnp.real(zb), jnp.imag(zf), jnp.imag(zb)], axis=3)
    fmat = fmat.transpose(0, 2, 1, 4, 3).reshape(depth, S5_GROUPS, L * S5_GROUP, 4 * S5_STATE)
    wf = cc[:, 0][:, None] * apow[:, 1 + idx, 0][:, :, :, None, :]
    wb = cc[:, 1][:, None] * apow[:, L - idx, 1][:, :, :, None, :]
    emat = jnp.concatenate([jnp.real(wf), jnp.real(wb), -jnp.imag(wf), -jnp.imag(wb)], axis=4)
    emat = emat.transpose(0, 2, 4, 1, 3).reshape(depth, S5_GROUPS, 4 * S5_STATE, L * S5_GROUP)
    al = apow[:, L]
    a16 = jnp.stack([jnp.concatenate([jnp.real(al[:, 0]), jnp.real(al[:, 1])], axis=-1),
                     jnp.concatenate([jnp.imag(al[:, 0]), jnp.imag(al[:, 1])], axis=-1)], axis=1)
    tr = lambda w: jnp.swapaxes(w, -1, -2).astype(BF16)
    return mt, tr(fmat), tr(emat), a16.astype(F32)


def _attn_kernel(*refs, grp, bq, sub, has_sink, has_seq, window, t_seq, bk):
    refs = list(refs)
    sink_ref = refs.pop(0) if has_sink else None
    q_ref, kc_ref, vc_ref = refs[:3]
    ks_ref, vs_ref = (refs[3], refs[4]) if has_seq else (None, None)
    o_ref = refs[-1]
    hk = pl.program_id(1)
    qi = pl.program_id(2)

    def scores(q, k, mask=None):
        s = lax.dot_general(q, k, (((1,), (1,)), ((), ())), preferred_element_type=F32)
        return s if mask is None else jnp.where(mask, s, NEG_INF)

    band_cache = {}

    def col_minus_row(rows, wlen, nq):
        if (rows, wlen, nq) not in band_cache:
            band_cache[rows, wlen, nq] = (lax.broadcasted_iota(jnp.int32, (rows, wlen), 1)
                                          - lax.broadcasted_iota(jnp.int32, (rows, wlen), 0) % nq)
        return band_cache[rows, wlen, nq]

    def logits(r0, nq):
        rows = grp * nq
        q = jnp.concatenate([q_ref[r0:r0 + nq, g * LANE:(g + 1) * LANE] for g in range(grp)], axis=0)
        s_parts, v_parts = [scores(q, kc_ref[...])], [vc_ref[...]]
        if has_seq and window is not None:
            wlen = nq + 2 * window
            q0 = qi * bq + r0
            start = pl.multiple_of(jnp.clip(q0 - window, 0, t_seq - wlen), LANE)
            in_band = jnp.abs(col_minus_row(rows, wlen, nq) + (start - q0)) <= window
            s_parts.append(scores(q, ks_ref[pl.ds(start, wlen), :], in_band))
            v_parts.append(vs_ref[pl.ds(start, wlen), :])
        elif has_seq:
            s_parts.append(scores(q, ks_ref[0:bk, :]))
            v_parts.append(vs_ref[0:bk, :])
        return q, s_parts, v_parts

    def softmax_pv(nq, s_parts, v_parts):
        m = functools.reduce(jnp.maximum, [jnp.max(s, axis=-1, keepdims=True) for s in s_parts])
        if has_sink:
            sink = jnp.concatenate([jnp.full((nq, 1), sink_ref[hk * grp + g], F32) for g in range(grp)], axis=0)
            m = jnp.maximum(m, sink)
        acc = sum(jnp.dot(jnp.exp2(s - m).astype(BF16), v, preferred_element_type=F32)
                  for s, v in zip(s_parts, v_parts))
        if has_sink:
            one_lane = lax.broadcasted_iota(jnp.int32, (1, LANE), 1) == V_ONE
            acc = acc + jnp.where(one_lane, jnp.exp2(sink - m), 0.0)
        return m, acc

    def finish(r0, nq, acc):
        o = acc * (1.0 / acc[:, V_ONE:V_ONE + 1])
        for g in range(grp):
            o_ref[r0:r0 + nq, g * LANE:(g + 1) * LANE] = o[g * nq:(g + 1) * nq].astype(o_ref.dtype)

    if has_seq and window is not None:
        for r0 in range(0, bq, sub):
            _, s_parts, v_parts = logits(r0, sub)
            finish(r0, sub, softmax_pv(sub, s_parts, v_parts)[1])
        return
    q, s_parts, v_parts = logits(0, bq)
    m, acc = softmax_pv(bq, s_parts, v_parts)
    if has_seq:
        def body(j, carry):
            m, acc = carry
            st = pl.multiple_of(j * bk, bk)
            s = scores(q, ks_ref[pl.ds(st, bk), :])
            m_new = jnp.maximum(m, jnp.max(s, axis=-1, keepdims=True))
            p = jnp.exp2(s - m_new).astype(BF16)
            acc = jnp.exp2(m - m_new) * acc + jnp.dot(p, vs_ref[pl.ds(st, bk), :], preferred_element_type=F32)
            return m_new, acc
        m, acc = lax.fori_loop(1, t_seq // bk, body, (m, acc), unroll=True)
    finish(0, bq, acc)


def _attention(q, kc, vc, ks=None, vs=None, *, grp, sink=None, window=None, bq=TILES['attn_q'], bk=TILES['mla_k']):
    b, tq, wq = q.shape
    hkv = wq // (grp * LANE)
    lc = kc.shape[1]
    has_seq = ks is not None
    t_seq = ks.shape[1] if has_seq else 0
    bq = _tile(tq, bq)
    sub = min(bq, TILES['swa_sub'])
    if has_seq and window is None:
        bk = _tile(t_seq, bk)
    kern = functools.partial(_attn_kernel, grp=grp, bq=bq, sub=sub, has_sink=sink is not None, has_seq=has_seq,
                             window=window, t_seq=t_seq, bk=bk)
    in_specs, args = [], []
    if sink is not None:
        in_specs.append(pl.BlockSpec(memory_space=pltpu.SMEM))
        args.append(sink)
    in_specs.append(pl.BlockSpec((None, bq, grp * LANE), lambda bb, h, i: (bb, i, h)))
    args.append(q)
    ctx_spec = pl.BlockSpec((None, lc, LANE), lambda bb, h, i: (bb, 0, h))
    in_specs += [ctx_spec, ctx_spec]
    args += [kc, vc]
    if has_seq:
        seq_spec = pl.BlockSpec((None, t_seq, LANE), lambda bb, h, i: (bb, 0, h))
        in_specs += [seq_spec, seq_spec]
        args += [ks, vs]
    return pl.pallas_call(
        kern,
        out_shape=jax.ShapeDtypeStruct((b, tq, wq), BF16),
        grid=(b, hkv, tq // bq),
        in_specs=in_specs,
        out_specs=pl.BlockSpec((None, bq, grp * LANE), lambda bb, h, i: (bb, i, h)),
        compiler_params=_params(("parallel", "parallel", "parallel")),
        name="attention",
    )(*args)


def _split3(x):
    hi = x.astype(BF16)
    r = x - hi.astype(F32)
    mid = r.astype(BF16)
    lo = (r - mid.astype(F32)).astype(BF16)
    return hi, mid, lo


def _hg_kernel(gp_ref, qf_ref, zf_ref, vf_ref, qb_ref, zb_ref, vb_ref, s0f_ref, s0b_ref,
               of_ref, ob_ref, sf_ref, sb_ref, st_ref, *, nchunk):
    i = pl.program_id(1)
    c = HG_CHUNK
    w = HG_HEADS * HG_DK

    @pl.when(i == 0)
    def _():
        st_ref[0] = s0f_ref[...]
        st_ref[1] = s0b_ref[...]

    ri = lax.broadcasted_iota(jnp.int32, (c, c), 0)
    ci = lax.broadcasted_iota(jnp.int32, (c, c), 1)
    ti = lax.broadcasted_iota(jnp.int32, (c, HG_HEADS * c), 0)
    si = lax.broadcasted_iota(jnp.int32, (c, HG_HEADS * c), 1) % c
    lane = lax.broadcasted_iota(jnp.int32, (1, w), 1)
    hmask = [(lane // HG_DK) == h for h in range(HG_HEADS)]
    r2 = lax.broadcasted_iota(jnp.int32, (w, w), 0) // HG_DV
    c2 = lax.broadcasted_iota(jnp.int32, (w, w), 1) // HG_DK
    diag = r2 == c2
    dirs = {
        True: dict(q=qf_ref, z=zf_ref, v=vf_ref, o=of_ref, gp=tuple(gp_ref[r:r + 1, :] for r in range(3)),
                   tri=jnp.where(ci <= ri, 1.0, 0.0).astype(BF16), causal=si <= ti),
        False: dict(q=qb_ref, z=zb_ref, v=vb_ref, o=ob_ref, gp=tuple(gp_ref[r:r + 1, :] for r in range(3, 6)),
                    tri=jnp.where(ci >= ri, 1.0, 0.0).astype(BF16), causal=si >= ti),
    }
    units = []
    for n in range(nchunk):
        units += [(True, n), (False, nchunk - 1 - n)]

    ph1 = []
    for fwd, n in units:
        dd = dirs[fwd]
        sl = slice(n * c, (n + 1) * c)
        z = dd['z'][sl, :]
        log_lb, l1p, om = dd['gp']
        ls = jnp.minimum(z, 0.0) - jnp.log(1.0 + jnp.exp(-jnp.abs(z)))
        bterm = l1p + ls
        lf = jnp.maximum(log_lb, bterm) + jnp.log(1.0 + jnp.exp(-jnp.abs(log_lb - bterm)))
        k = om / (1.0 + jnp.exp(z))
        gsum = sum(jnp.dot(dd['tri'], part, preferred_element_type=F32) for part in _split3(lf))
        ph1.append((k, gsum))

    ph2 = []
    for (fwd, n), (k, gsum) in zip(units, ph1):
        dd = dirs[fwd]
        sl = slice(n * c, (n + 1) * c)
        q, v = dd['q'][sl, :], dd['v'][sl, :]
        tot = gsum[c - 1:c, :] if fwd else gsum[0:1, :]
        gm = gsum[c // 2:c // 2 + 1, :]
        qd = (q * jnp.exp(gsum - gm)).astype(BF16)
        kinv = k * jnp.exp(gm - gsum)
        q_in = (q * jnp.exp(gsum)).astype(BF16)
        k_end = (k * jnp.exp(tot - gsum)).astype(BF16)
        kstack = jnp.concatenate([jnp.where(hm, kinv, 0.0) for hm in hmask], axis=0).astype(BF16)
        vstack = jnp.concatenate([jnp.where(hm, v, 0.0) for hm in hmask], axis=0).astype(BF16)
        ph2.append((qd, kstack, vstack, q_in, k_end, v.astype(BF16), jnp.exp(tot)))

    ph3 = []
    for (fwd, n), (qd, kstack, vstack, q_in, k_end, vb, dec) in zip(units, ph2):
        att = lax.dot_general(qd, kstack, (((1,), (1,)), ((), ())), preferred_element_type=F32)
        att = jnp.where(dirs[fwd]['causal'], att, 0.0).astype(BF16)
        o_intra = jnp.dot(att, vstack, preferred_element_type=F32)
        kv_t = lax.dot_general(vb, k_end, (((0,), (0,)), ((), ())), preferred_element_type=F32)
        ph3.append((o_intra, jnp.where(diag, kv_t, 0.0)))

    st = {True: st_ref[0], False: st_ref[1]}
    for (fwd, n), (_, _, _, q_in, _, _, dec), (o_intra, kv_t) in zip(units, ph2, ph3):
        sl = slice(n * c, (n + 1) * c)
        o_inter = lax.dot_general(q_in, st[fwd].astype(BF16), (((1,), (1,)), ((), ())), preferred_element_type=F32)
        dirs[fwd]['o'][sl, :] = (o_intra + o_inter).astype(dirs[fwd]['o'].dtype)
        st[fwd] = st[fwd] * dec + kv_t
    st_ref[0] = st[True]
    st_ref[1] = st[False]

    @pl.when(i == pl.num_programs(1) - 1)
    def _():
        sf_ref[...] = st[True]
        sb_ref[...] = st[False]


def _hgrn(q, zf, zb, v, gp, s0f, s0b):
    b, t, w = q.shape
    tb = _tile(t, TILES['hgrn'])
    nblk = t // tb
    fw = pl.BlockSpec((None, tb, w), lambda bb, i: (bb, i, 0))
    bw = pl.BlockSpec((None, tb, w), lambda bb, i: (bb, nblk - 1 - i, 0))
    st = pl.BlockSpec((None, w, w), lambda bb, i: (bb, 0, 0))
    return pl.pallas_call(
        functools.partial(_hg_kernel, nchunk=tb // HG_CHUNK),
        out_shape=(jax.ShapeDtypeStruct((b, t, w), BF16), jax.ShapeDtypeStruct((b, t, w), BF16),
                   jax.ShapeDtypeStruct((b, w, w), F32), jax.ShapeDtypeStruct((b, w, w), F32)),
        grid=(b, nblk),
        in_specs=[_full((8, w)), fw, fw, fw, bw, bw, bw, st, st],
        out_specs=(fw, bw, st, st),
        scratch_shapes=[pltpu.VMEM((2, w, w), F32)],
        compiler_params=_params(("parallel", "arbitrary")),
        name="hgrn2",
    )(gp, q, zf, v, q, zb, v, s0f, s0b)


def _gelu_tanh(x):
    return 0.5 * x * (1.0 + jnp.tanh(math.sqrt(2.0 / math.pi) * (x + 0.044715 * (x * x * x))))


def _outproj_kernel(x_ref, g1_ref, ya_ref, yb_ref, of_ref, ob_ref, hg_ref, yd_ref,
                    wglu_ref, bglu_ref, hn_ref, pavg_ref, wo_ref, o_ref, ys_ref):
    for j in range(S5_CHUNK):
        for v in range(2):
            ys_ref[v, pl.ds(j, ys_ref.shape[1] // S5_CHUNK, stride=S5_CHUNK), :] = ya_ref[j, :, v * LANE:(v + 1) * LANE]
    ya = _gelu_tanh(jnp.concatenate([ys_ref[0], ys_ref[1]], axis=1))
    gl = jnp.dot(ya.astype(BF16), wglu_ref[...], preferred_element_type=F32) + bglu_ref[...]
    o = of_ref[...].astype(F32) + ob_ref[...].astype(F32)
    o2 = o * o
    hi = o2.astype(BF16)
    lo = (o2 - hi.astype(F32)).astype(BF16)
    ms = (jnp.dot(hi, pavg_ref[...], preferred_element_type=F32)
          + jnp.dot(lo, pavg_ref[...], preferred_element_type=F32))
    y = (jnp.dot(yb_ref[...], wo_ref[256:768, :], preferred_element_type=F32)
         + jnp.dot(yd_ref[...], wo_ref[1024:1536, :], preferred_element_type=F32))
    ya = ya * (1.0 / (1.0 + jnp.exp(-gl)))
    gate = hg_ref[...].astype(F32)
    yc = o * lax.rsqrt(ms + EPS) * hn_ref[...] * (gate * (1.0 / (1.0 + jnp.exp(-gate))))
    y = (y + jnp.dot(ya.astype(BF16), wo_ref[0:256, :], preferred_element_type=F32)
         + jnp.dot(yc.astype(BF16), wo_ref[768:1024, :], preferred_element_type=F32))
    o_ref[...] = x_ref[...] + g1_ref[...] * y


def _outproj(x, g1, ya, yb, of, ob, hg, yd, wglu, bglu, hn, pavg, wo_p, li):
    b, t, d = x.shape
    tt = _tile(t, TILES['outproj'])
    xs = lambda n: pl.BlockSpec((None, tt, n), lambda bb, i: (bb, i, 0))
    vec = pl.BlockSpec((None, 1, d), lambda bb, i: (bb, 0, 0))
    return pl.pallas_call(
        _outproj_kernel,
        out_shape=jax.ShapeDtypeStruct((b, t, d), F32),
        grid=(b, t // tt),
        in_specs=[xs(d), vec, pl.BlockSpec((None, S5_CHUNK, tt // S5_CHUNK, 256), lambda bb, i: (bb, 0, i, 0)),
                  xs(512), xs(256), xs(256), xs(256), xs(512),
                  _layer(wglu, li), _full((1, 256)), _full((1, 256)), _full(pavg.shape), _layer(wo_p, li)],
        out_specs=xs(d),
        scratch_shapes=[pltpu.VMEM((2, tt, LANE), F32)],
        compiler_params=_params(("parallel", "parallel")),
        name="outproj",
    )(x, g1, ya, yb, of, ob, hg, yd, wglu, bglu, hn, pavg, wo_p)


def _ffn_kernel(x_ref, sc_ref, sh_ref, g2_ref, ng_ref, wup_ref, wdn_ref, fg_ref, o_ref, acc_ref, *, hidden, ck, final):
    x = x_ref[...]
    hb = (_rms(x, ng_ref[...]) * (1.0 + sc_ref[...]) + sh_ref[...]).astype(BF16)
    for j in range(hidden // ck):
        gate = jnp.dot(hb, wup_ref[:, j * ck:(j + 1) * ck], preferred_element_type=F32)
        up = jnp.dot(hb, wup_ref[:, hidden + j * ck:hidden + (j + 1) * ck], preferred_element_type=F32)
        a = (gate * (1.0 / (1.0 + jnp.exp(-gate))) * up).astype(BF16)
        part = jnp.dot(a, wdn_ref[j * ck:(j + 1) * ck, :], preferred_element_type=F32)
        if j == 0:
            acc_ref[...] = part
        else:
            acc_ref[...] += part
    y = x + g2_ref[...] * acc_ref[...]
    if final:
        y = _rms(y, fg_ref[...])
    o_ref[...] = y


def _ffn(x, sc, sh, g2, ng, wup, wdn, fg, final, li):
    b, t, d = x.shape
    hidden = wdn.shape[1]
    tt = _tile(t, TILES['ffn'])
    xs = pl.BlockSpec((None, tt, d), lambda bb, i: (bb, i, 0))
    vec = pl.BlockSpec((None, 1, d), lambda bb, i: (bb, 0, 0))
    return pl.pallas_call(
        functools.partial(_ffn_kernel, hidden=hidden, ck=_tile(hidden, TILES['ffn_hidden']), final=final),
        out_shape=jax.ShapeDtypeStruct((b, t, d), F32),
        grid=(b, t // tt),
        in_specs=[xs, vec, vec, vec, _full((1, d)), _layer(wup, li), _layer(wdn, li), _full((1, d))],
        out_specs=xs,
        scratch_shapes=[pltpu.VMEM((tt, d), F32)],
        compiler_params=_params(("parallel", "parallel")),
        name="ffn",
    )(x, sc, sh, g2, ng, wup, wdn, fg)


def _pad_heads(w, heads, dim):
    w = w.reshape(w.shape[:-1] + (heads, dim))
    w = jnp.pad(w, [(0, 0)] * (w.ndim - 1) + [(0, LANE - dim)])
    return w.reshape(w.shape[:-2] + (heads * LANE,))


def _layer_weights(w_in, w_out, mla_w_qb, mla_w_kvb):
    depth, d, n_in = w_in.shape
    qscale = jnp.ones((n_in,), F32).at[_C_SQ:_C_SQ + SWA_HEADS * SWA_HEAD_DIM].set(SWA_HEAD_DIM ** -0.5 * LOG2E)
    w_in_p = jnp.pad(w_in * qscale, ((0, 0), (0, 0), (0, _N_INP - n_in))).astype(BF16)
    wqb_p = _pad_heads(mla_w_qb * (MLA_SCALE * LOG2E), MLA_HEADS, MLA_NOPE + MLA_ROPE).astype(BF16)
    kvb = mla_w_kvb.reshape(depth, MLA_KV_RANK, MLA_HEADS, MLA_NOPE + MLA_V)
    wk = _pad_heads(kvb[..., :MLA_NOPE].reshape(depth, MLA_KV_RANK, -1), MLA_HEADS, MLA_NOPE)
    wv = _pad_heads(kvb[..., MLA_NOPE:].reshape(depth, MLA_KV_RANK, -1), MLA_HEADS, MLA_V)
    wkv_p = jnp.concatenate([wk, wv], axis=-1).astype(BF16)
    assert w_out.shape[1] == 4 * 256
    pad_rows = lambda w, heads, dim: jnp.swapaxes(_pad_heads(jnp.swapaxes(w, -1, -2), heads, dim), -1, -2)
    wo_p = jnp.concatenate([w_out[:, 0:256], pad_rows(w_out[:, 256:512], SWA_HEADS, SWA_HEAD_DIM),
                            w_out[:, 512:768], pad_rows(w_out[:, 768:1024], MLA_HEADS, MLA_V)], axis=1).astype(BF16)
    return w_in_p, wqb_p, wkv_p, wo_p


def _rope_tables(length, dim, lo, ident, repeat=1):
    n_freq = dim // 4
    rows = length // GRID_W
    row = jnp.repeat(jnp.arange(rows, dtype=F32), GRID_W)
    col = jnp.tile(jnp.arange(GRID_W, dtype=F32), rows)
    inv = ROPE_BASE ** (-jnp.arange(n_freq, dtype=F32) / n_freq)
    ang = jnp.stack([row[:, None] * inv, col[:, None] * inv], axis=1)
    cos, sin = jnp.cos(ang), jnp.sin(ang)
    z = jnp.zeros_like(sin)
    cos_l = jnp.stack([cos, cos], axis=2).reshape(length, dim)
    sina = jnp.stack([-sin, z], axis=2).reshape(length, dim)
    sinb = jnp.stack([z, sin], axis=2).reshape(length, dim)
    if ident:
        cos_l, sina, sinb = jnp.ones_like(cos_l), jnp.zeros_like(sina), jnp.zeros_like(sinb)
    cos_l, sina, sinb = (jnp.tile(a, (1, repeat)) for a in (cos_l, sina, sinb))
    pad = lambda a, fill: jnp.pad(a, ((0, 0), (lo, LANE - lo - dim * repeat)), constant_values=fill)
    return jnp.stack([pad(cos_l, 1.0), pad(sina, 0.0), pad(sinb, 0.0)], axis=0)


def kernel(x, c, ctx, c_ctx, w_mod, b_mod, norm1_g, norm2_g, w_in, w_out, s5_lam_re, s5_lam_im, s5_log_dt,
           s5_b_re, s5_b_im, s5_c_re, s5_c_im, s5_d, s5_w_glu, s5_b_glu, swa_sink, hg_lb, hg_norm_g,
           mla_q_norm_g, mla_w_qb, mla_kv_norm_g, mla_w_kvb, ffn_w_up, ffn_w_down, final_norm_g):
    b, t, d = x.shape
    lc = ctx.shape[1]
    depth = w_mod.shape[0]

    cc = jnp.zeros((8, d), F32).at[:b].set(c).at[b].set(c_ctx)
    mods = _modulation(cc, w_mod, b_mod)

    lb_cum = jnp.cumsum(jax.nn.softmax(hg_lb.astype(F32), axis=1), axis=1)
    lb = lb_cum - lb_cum[:, :1]

    tab_swa = _rope_tables(t, SWA_HEAD_DIM, 0, False, repeat=2)
    tab_mla = _rope_tables(t, MLA_ROPE, MLA_NOPE, False)
    tab_swa_c = _rope_tables(lc, SWA_HEAD_DIM, 0, True, repeat=2)
    tab_mla_c = _rope_tables(lc, MLA_ROPE, MLA_NOPE, True)
    pavg = jnp.kron(jnp.eye(HG_HEADS, dtype=F32), jnp.full((HG_DV, HG_DV), 1.0 / HG_DV, F32)).astype(BF16)
    zero_st = jnp.zeros((b, HG_HEADS * HG_DV, HG_HEADS * HG_DK), F32)
    zero_s5 = jnp.zeros((b, S5_GROUPS, 4 * S5_STATE), F32)

    w_in_p, wqb_p, wkv_p, wo_p = _layer_weights(w_in, w_out, mla_w_qb, mla_w_kvb)
    mats = _s5_matrices(s5_lam_re, s5_lam_im, s5_log_dt, s5_b_re, s5_b_im, s5_c_re, s5_c_im, s5_d)
    wglu = s5_w_glu.astype(BF16)
    wup = ffn_w_up.astype(BF16)
    wdn = ffn_w_down.astype(BF16)
    fg = final_norm_g.reshape(1, d)

    for i in range(depth):
        need_ctx = i < depth - 1
        mod = mods[i, :b].reshape(b, 6, 1, d)
        mod_c = jnp.broadcast_to(mods[i, b].reshape(1, 6, 1, d), (b, 6, 1, d))
        sh1, sc1, g1, sh2, sc2, g2 = (mod[:, j] for j in range(6))
        csh1, csc1, cg1, csh2, csc2, cg2 = (mod_c[:, j] for j in range(6))
        n1 = norm1_g[i].reshape(1, d)
        qg = mla_q_norm_g[i].reshape(1, -1)
        kvg = mla_kv_norm_g[i].reshape(1, -1)
        px = _inproj(x, sc1, sh1, n1, w_in_p, tab_swa, tab_mla, qg, wqb_p, kvg, wkv_p, i)
        pc = _inproj(ctx, csc1, csh1, n1, w_in_p, tab_swa_c, tab_mla_c, qg, wqb_p, kvg, wkv_p, i)
        (xu, xsq, xsk, xsv, xhq, xhzf, xhzb, xhi, xhg, xmq, xmk, xmv) = px
        (cu, csq, csk, csv, chq, chzf, chzb, chi, chg, cmq, cmk, cmv) = pc

        ya_c, s5_fin = _s5_scan(cu, mats, zero_s5, i)
        ya, _ = _s5_scan(xu, mats, s5_fin, i)

        grp = SWA_HEADS // SWA_KV_HEADS
        sink = swa_sink[i].astype(F32) * LOG2E
        yb = _attention(xsq, csk, csv, xsk, xsv, grp=grp, sink=sink, window=SWA_WINDOW, bq=TILES['swa_q'])
        lbf, lbb = lb[0, i], lb[1, i]
        gp = jnp.stack([jnp.log(lbf), jnp.log1p(-lbf), 1.0 - lbf,
                        jnp.log(lbb), jnp.log1p(-lbb), 1.0 - lbb, lbf, lbb], axis=0)
        of_c, ob_c, stf, stb = _hgrn(chq, chzf, chzb, chi, gp, zero_st, zero_st)
        of, ob, _, _ = _hgrn(xhq, xhzf, xhzb, xhi, gp, stf, stb)
        yd = _attention(xmq, cmk, cmv, xmk, xmv, grp=1, bq=TILES['mla_q'])

        bglu = s5_b_glu[i].reshape(1, -1)
        hn = jnp.tile(hg_norm_g[i], HG_HEADS).reshape(1, -1)
        x = _outproj(x, g1, ya, yb, of, ob, xhg, yd, wglu, bglu, hn, pavg, wo_p, i)
        n2 = norm2_g[i].reshape(1, d)
        x = _ffn(x, sc2, sh2, g2, n2, wup, wdn, fg, not need_ctx, i)
        if need_ctx:
            yb_c = _attention(csq, csk, csv, grp=grp, sink=sink)
            yd_c = _attention(cmq, cmk, cmv, grp=1)
            ctx = _outproj(ctx, cg1, ya_c, yb_c, of_c, ob_c, chg, yd_c, wglu, bglu, hn, pavg, wo_p, i)
            ctx = _ffn(ctx, csc2, csh2, cg2, n2, wup, wdn, fg, False, i)
    return x
```

```python
import functools
import math

import jax
import jax.numpy as jnp
from jax import lax
from jax.experimental import pallas as pl
from jax.experimental.pallas import tpu as pltpu

F32 = jnp.float32
BF16 = jnp.bfloat16

EPS = 1e-6
NEG_INF = -1e30
ROPE_BASE = 10000.0
GRID_W = 64
LANE = 128
VMEM_V7X = 64 * 1024 * 1024
VMEM_LIMIT = VMEM_V7X - 8 * 1024 * 1024

S5_CH, S5_GROUP, S5_STATE = 256, 16, 64
S5_GROUPS = S5_CH // S5_GROUP
S5_CHUNK = 16
SWA_HEADS, SWA_KV_HEADS, SWA_HEAD_DIM, SWA_WINDOW = 4, 2, 64, 128
HG_HEADS, HG_DK, HG_DV = 4, 64, 64
HG_CHUNK = 64
MLA_HEADS, MLA_Q_RANK, MLA_KV_RANK = 4, 256, 128
MLA_NOPE, MLA_ROPE, MLA_V = 64, 32, 64
MLA_SCALE = (MLA_NOPE + MLA_ROPE) ** -0.5
LOG2E = 1.4426950408889634
V_ONE = 64


TILES = dict(
    mod_cols=1536,
    inproj=1024,
    hgrn=1024,
    outproj=512,
    ffn=512,
    ffn_hidden=256,
    attn_q=512,
    swa_q=1024,
    swa_sub=LANE,
    mla_q=1024,
    mla_k=2048,
)
S5_SCAN_UNROLL = 8


def _tile(n, pref):
    t = min(n, pref)
    assert n % t == 0, (n, pref)
    return t


def _params(sem):
    return pltpu.CompilerParams(dimension_semantics=sem, vmem_limit_bytes=VMEM_LIMIT)


def _full(shape):
    nd = len(shape)
    return pl.BlockSpec(shape, lambda *_: (0,) * nd, pipeline_mode=pl.Buffered(1))


def _layer(w, li):
    nd = w.ndim - 1
    return pl.BlockSpec((None,) + w.shape[1:], lambda *_: (li,) + (0,) * nd, pipeline_mode=pl.Buffered(1))


def _mod_kernel(c_ref, w_ref, b_ref, o_ref):
    c = c_ref[...]
    s = c * (1.0 / (1.0 + jnp.exp(-c)))
    o_ref[...] = jnp.dot(s.astype(BF16), w_ref[...].astype(BF16), preferred_element_type=F32) + b_ref[...]


def _modulation(cc, w_mod, b_mod):
    depth, d, n = w_mod.shape
    tn = _tile(n, TILES['mod_cols'])
    return pl.pallas_call(
        _mod_kernel,
        out_shape=jax.ShapeDtypeStruct((depth, 8, n), F32),
        grid=(depth, n // tn),
        in_specs=[pl.BlockSpec((8, d), lambda l, j: (0, 0)),
                  pl.BlockSpec((None, d, tn), lambda l, j: (l, 0, j)),
                  pl.BlockSpec((None, 1, tn), lambda l, j: (l, 0, j))],
        out_specs=pl.BlockSpec((None, 8, tn), lambda l, j: (l, 0, j)),
        compiler_params=_params(("parallel", "parallel")),
        name="modulation",
    )(cc, w_mod, b_mod.reshape(depth, 1, n))


def _rope_block(x, t_ref, half):
    return (x * t_ref[0] + pltpu.roll(x, LANE - half, 1) * t_ref[1]
            + pltpu.roll(x, half, 1) * t_ref[2])


def _rms(x, g):
    return x * lax.rsqrt(jnp.mean(x * x, axis=-1, keepdims=True) + EPS) * g


_C_U = 0
_C_SQ = 256
_C_SKV = 512
_C_HG = 768
_C_CQ = _C_HG + 5 * 256
_C_CKV = _C_CQ + 256
_N_INP = _C_CKV + 256


def _inproj_kernel(x_ref, sc_ref, sh_ref, g_ref, w_ref, ts_ref, tm_ref, qg_ref, wqb_ref, kvg_ref, wkv_ref,
                   u_ref, sq_ref, sk_ref, sv_ref, hq_ref, hzf_ref, hzb_ref, hi_ref, hg_ref,
                   mq_ref, mk_ref, mv_ref, us_ref):
    h = _rms(x_ref[...], g_ref[...]) * (1.0 + sc_ref[...]) + sh_ref[...]
    hb = h.astype(BF16)

    def proj(lo, n):
        return jnp.dot(hb, w_ref[:, lo:lo + n], preferred_element_type=F32)

    cq = _rms(proj(_C_CQ, 256), qg_ref[...]).astype(BF16)
    ckv_kr = proj(_C_CKV, 256)
    ckv = _rms(ckv_kr[:, :LANE], kvg_ref[...]).astype(BF16)
    kr = pltpu.roll(ckv_kr[:, LANE:], MLA_NOPE, 1)

    u = proj(_C_U, 256)
    for v in range(2):
        us_ref[v] = u[:, v * LANE:(v + 1) * LANE]
    for j in range(S5_CHUNK):
        for v in range(2):
            u_ref[j, :, v * LANE:(v + 1) * LANE] = us_ref[v, pl.ds(j, us_ref.shape[1] // S5_CHUNK, stride=S5_CHUNK), :]
    lane = lax.broadcasted_iota(jnp.int32, (1, LANE), 1)
    low = lane < SWA_HEAD_DIM
    one = jnp.where(lane == V_ONE, 1.0, 0.0)

    def spread(pair, fill):
        return [jnp.where(low, blk, fill).astype(BF16) for blk in (pair, pltpu.roll(pair, SWA_HEAD_DIM, 1))]

    sq = proj(_C_SQ, 256)
    for pp in range(SWA_HEADS // 2):
        heads = spread(_rope_block(sq[:, pp * LANE:(pp + 1) * LANE], ts_ref, 16), 0.0)
        for e in range(2):
            sq_ref[:, (2 * pp + e) * LANE:(2 * pp + e + 1) * LANE] = heads[e]
    skv = proj(_C_SKV, 256)
    for e, blk in enumerate(spread(_rope_block(skv[:, :LANE], ts_ref, 16), 0.0)):
        sk_ref[:, e * LANE:(e + 1) * LANE] = blk
    for e, blk in enumerate(spread(skv[:, LANE:], one)):
        sv_ref[:, e * LANE:(e + 1) * LANE] = blk

    q = jnp.dot(cq, wqb_ref[...], preferred_element_type=F32)
    for hh in range(MLA_HEADS):
        sl = slice(hh * LANE, (hh + 1) * LANE)
        mq_ref[:, sl] = _rope_block(q[:, sl], tm_ref, 8).astype(BF16)
    for i, r in enumerate((hq_ref, hzf_ref, hzb_ref)):
        r[...] = proj(_C_HG + i * 256, 256)
    kv = jnp.dot(ckv, wkv_ref[...], preferred_element_type=F32)
    for hh in range(MLA_HEADS):
        sl = slice(hh * LANE, (hh + 1) * LANE)
        mk_ref[:, sl] = _rope_block(kv[:, sl] + kr, tm_ref, 8).astype(BF16)
        mv_ref[:, sl] = jnp.where(lane == V_ONE, 1.0, kv[:, MLA_HEADS * LANE + hh * LANE:][:, :LANE]).astype(BF16)
    for i, r in enumerate((hi_ref, hg_ref)):
        r[...] = proj(_C_HG + (3 + i) * 256, 256).astype(r.dtype)


def _inproj(x, sc, sh, g, w_in_p, tab_swa, tab_mla, qg, wqb_p, kvg, wkv_p, li):
    b, t, d = x.shape
    tt = _tile(t, TILES['inproj'])
    row = lambda n, dt: jax.ShapeDtypeStruct((b, t, n), dt)
    out_shape = (jax.ShapeDtypeStruct((b, S5_CHUNK, t // S5_CHUNK, 256), F32),
                 row(512, BF16), row(256, BF16), row(256, BF16),
                 row(256, F32), row(256, F32), row(256, F32), row(256, F32), row(256, BF16),
                 row(512, BF16), row(512, BF16), row(512, BF16))
    xs = lambda n: pl.BlockSpec((None, tt, n), lambda i, bb: (bb, i, 0))
    vec = pl.BlockSpec((None, 1, d), lambda i, bb: (bb, 0, 0))
    tab = pl.BlockSpec((3, tt, LANE), lambda i, bb: (0, i, 0))
    return pl.pallas_call(
        _inproj_kernel,
        out_shape=out_shape,
        grid=(t // tt, b),
        in_specs=[xs(d), vec, vec, _full((1, d)), _layer(w_in_p, li), tab, tab,
                  _full((1, 256)), _layer(wqb_p, li), _full((1, 128)), _layer(wkv_p, li)],
        out_specs=(pl.BlockSpec((None, S5_CHUNK, tt // S5_CHUNK, 256), lambda i, bb: (bb, 0, i, 0)),)
        + tuple(xs(s.shape[-1]) for s in out_shape[1:]),
        scratch_shapes=[pltpu.VMEM((2, tt, LANE), F32)],
        compiler_params=_params(("parallel", "parallel")),
        name="inproj",
    )(x, sc, sh, g, w_in_p, tab_swa, tab_mla, qg, wqb_p, kvg, wkv_p)


def _s5_kernel(u_ref, mt_ref, ft_ref, et_ref, a_ref, s0_ref, y_ref, sfin_ref, zs_ref, ut_ref, yt_ref, *, nc):
    ng, half = S5_GROUPS, LANE // S5_GROUP
    for j in range(S5_CHUNK):
        for v in range(2):
            t = u_ref[j, :, v * LANE:(v + 1) * LANE].T
            for gl in range(half):
                ut_ref[half * v + gl, S5_GROUP * j:S5_GROUP * (j + 1), :] = (
                    t[S5_GROUP * gl:S5_GROUP * (gl + 1), :].astype(BF16))
    for g in range(ng):
        z = jnp.dot(ft_ref[g], ut_ref[g], preferred_element_type=F32).T
        for c in range(2):
            zs_ref[c, pl.ds(g, nc, stride=ng), :] = z[:, c * LANE:(c + 1) * LANE]

    fmask = lax.broadcasted_iota(jnp.int32, (1, LANE), 1) < S5_STATE
    ar, ai = a_ref[0], a_ref[1]

    def step(i, s):
        rf = pl.multiple_of(i * ng, ng)
        rb = pl.multiple_of((nc - 1 - i) * ng, ng)
        zre, zim = (jnp.where(fmask, zs_ref[c, pl.ds(rf, ng), :], zs_ref[c, pl.ds(rb, ng), :]) for c in range(2))
        for c in range(2):
            zs_ref[c, pl.ds(rf, ng), 0:S5_STATE] = s[c][:, 0:S5_STATE]
            zs_ref[c, pl.ds(rb, ng), S5_STATE:LANE] = s[c][:, S5_STATE:LANE]
        re, im = s
        return ar * re - ai * im + zre, ar * im + ai * re + zim

    re, im = lax.fori_loop(0, nc, step, (s0_ref[:, :LANE], s0_ref[:, LANE:]), unroll=S5_SCAN_UNROLL)
    sfin_ref[:, :LANE] = re
    sfin_ref[:, LANE:] = im

    for g in range(ng):
        ss = jnp.concatenate([zs_ref[c, pl.ds(g, nc, stride=ng), :] for c in range(2)], axis=1)
        yt = (jnp.dot(mt_ref[g], ut_ref[g], preferred_element_type=F32)
              + jnp.dot(et_ref[g], ss.T.astype(BF16), preferred_element_type=F32))
        for t in range(S5_CHUNK):
            yt_ref[t, S5_GROUP * g:S5_GROUP * (g + 1), :] = yt[S5_GROUP * t:S5_GROUP * (t + 1), :]
    for t in range(S5_CHUNK):
        y_ref[t] = yt_ref[t].T.astype(y_ref.dtype)


def _s5_scan(uj, mats, s0, li):
    mt, ft, et, a16 = mats
    b, _, nc, _ = uj.shape
    slab = pl.BlockSpec((None, S5_CHUNK, nc, 256), lambda bb: (bb, 0, 0, 0), pipeline_mode=pl.Buffered(1))
    st = pl.BlockSpec((None, S5_GROUPS, 256), lambda bb: (bb, 0, 0))
    return pl.pallas_call(
        functools.partial(_s5_kernel, nc=nc),
        out_shape=(jax.ShapeDtypeStruct(uj.shape, BF16), jax.ShapeDtypeStruct((b, S5_GROUPS, 256), F32)),
        grid=(b,),
        in_specs=[slab, _layer(mt, li), _layer(ft, li), _layer(et, li), _layer(a16, li), st],
        out_specs=(slab, st),
        scratch_shapes=[pltpu.VMEM((2, S5_GROUPS * nc, LANE), F32), pltpu.VMEM((S5_GROUPS, 256, nc), BF16),
                        pltpu.VMEM((S5_CHUNK, 256, nc), F32)],
        compiler_params=_params(("parallel",)),
        name="s5",
    )(uj, mt, ft, et, a16, s0)


def _toeplitz_kernel(w_ref, o_ref):
    w = w_ref[...]
    for t in range(S5_CHUNK):
        lo = S5_GROUP * (S5_CHUNK - 1 - t)
        o_ref[S5_GROUP * t:S5_GROUP * (t + 1), :] = w[:, lo:lo + S5_CHUNK * S5_GROUP].astype(BF16)


def _toeplitz(strip):
    depth, ng, rows, width = strip.shape
    n = S5_CHUNK * S5_GROUP
    return pl.pallas_call(
        _toeplitz_kernel,
        out_shape=jax.ShapeDtypeStruct((depth, ng, n, n), BF16),
        grid=(depth, ng),
        in_specs=[pl.BlockSpec((None, None, rows, width), lambda d, g: (d, g, 0, 0))],
        out_specs=pl.BlockSpec((None, None, n, n), lambda d, g: (d, g, 0, 0)),
        compiler_params=_params(("parallel", "parallel")),
        name="toeplitz",
    )(strip)


def _s5_matrices(lam_re, lam_im, log_dt, b_re, b_im, c_re, c_im, d_skip):
    L = S5_CHUNK
    depth = lam_re.shape[0]
    hp = lax.Precision.HIGHEST
    lam = lax.complex(lam_re, lam_im)
    ldt = lam * jnp.exp(log_dt)
    bb = ((jnp.exp(ldt) - 1.0) / lam)[..., None] * lax.complex(b_re, b_im)
    cc = lax.complex(c_re, c_im)
    k = jnp.arange(L + 1, dtype=F32)
    apow = jnp.exp(ldt[:, None] * k[None, :, None, None, None])
    kern = jnp.real(jnp.einsum('ndghp,nkdgp,ndgpi->ndkgih', cc, apow[:, :L], bb, precision=hp))
    idx = jnp.arange(L)
    dg = d_skip.reshape(depth, 1, S5_GROUPS, 1, S5_GROUP) * jnp.eye(S5_GROUP, dtype=F32)[None, None, None]
    lag0 = kern[:, 0, :1] + kern[:, 1, :1] + dg
    by_lag = jnp.concatenate([kern[:, 0, :0:-1], lag0, kern[:, 1, 1:]], axis=1)
    strip = by_lag.transpose(0, 2, 4, 1, 3).reshape(depth, S5_GROUPS, S5_GROUP, (2 * L - 1) * S5_GROUP)
    mt = _toeplitz(jnp.pad(strip, ((0, 0), (0, 0), (0, 0), (0, S5_GROUP))))
    zf = apow[:, L - 1 - idx, 0][..., None] * bb[:, 0][:, None]
    zb = apow[:, idx, 1][..., None] * bb[:, 1][:, None]
    fmat = jnp.concatenate([jnp.real(zf), jnp.real(zb), jnp.imag(zf), jnp.imag(zb)], axis=3)
    fmat = fmat.transpose(0, 2, 1, 4, 3).reshape(depth, S5_GROUPS, L * S5_GROUP, 4 * S5_STATE)
    wf = cc[:, 0][:, None] * apow[:, 1 + idx, 0][:, :, :, None, :]
    wb = cc[:, 1][:, None] * apow[:, L - idx, 1][:, :, :, None, :]
    emat = jnp.concatenate([jnp.real(wf), jnp.real(wb), -jnp.imag(wf), -jnp.imag(wb)], axis=4)
    emat = emat.transpose(0, 2, 4, 1, 3).reshape(depth, S5_GROUPS, 4 * S5_STATE, L * S5_GROUP)
    al = apow[:, L]
    a16 = jnp.stack([jnp.concatenate([jnp.real(al[:, 0]), jnp.real(al[:, 1])], axis=-1),
                     jnp.concatenate([jnp.imag(al[:, 0]), jnp.imag(al[:, 1])], axis=-1)], axis=1)
    tr = lambda w: jnp.swapaxes(w, -1, -2).astype(BF16)
    return mt, tr(fmat), tr(emat), a16.astype(F32)


def _attn_kernel(*refs, grp, bq, sub, has_sink, has_seq, window, t_seq, bk):
    refs = list(refs)
    sink_ref = refs.pop(0) if has_sink else None
    q_ref, kc_ref, vc_ref = refs[:3]
    ks_ref, vs_ref = (refs[3], refs[4]) if has_seq else (None, None)
    o_ref = refs[-1]
    hk = pl.program_id(1)
    qi = pl.program_id(2)

    def scores(q, k, mask=None):
        s = lax.dot_general(q, k, (((1,), (1,)), ((), ())), preferred_element_type=F32)
        return s if mask is None else jnp.where(mask, s, NEG_INF)

    band_cache = {}

    def col_minus_row(rows, wlen, nq):
        if (rows, wlen, nq) not in band_cache:
            band_cache[rows, wlen, nq] = (lax.broadcasted_iota(jnp.int32, (rows, wlen), 1)
                                          - lax.broadcasted_iota(jnp.int32, (rows, wlen), 0) % nq)
        return band_cache[rows, wlen, nq]

    def logits(r0, nq):
        rows = grp * nq
        q = jnp.concatenate([q_ref[r0:r0 + nq, g * LANE:(g + 1) * LANE] for g in range(grp)], axis=0)
        s_parts, v_parts = [scores(q, kc_ref[...])], [vc_ref[...]]
        if has_seq and window is not None:
            wlen = nq + 2 * window
            q0 = qi * bq + r0
            start = pl.multiple_of(jnp.clip(q0 - window, 0, t_seq - wlen), LANE)
            in_band = jnp.abs(col_minus_row(rows, wlen, nq) + (start - q0)) <= window
            s_parts.append(scores(q, ks_ref[pl.ds(start, wlen), :], in_band))
            v_parts.append(vs_ref[pl.ds(start, wlen), :])
        elif has_seq:
            s_parts.append(scores(q, ks_ref[0:bk, :]))
            v_parts.append(vs_ref[0:bk, :])
        return q, s_parts, v_parts

    def softmax_pv(nq, s_parts, v_parts):
        m = functools.reduce(jnp.maximum, [jnp.max(s, axis=-1, keepdims=True) for s in s_parts])
        if has_sink:
            sink = jnp.concatenate([jnp.full((nq, 1), sink_ref[hk * grp + g], F32) for g in range(grp)], axis=0)
            m = jnp.maximum(m, sink)
        acc = sum(jnp.dot(jnp.exp2(s - m).astype(BF16), v, preferred_element_type=F32)
                  for s, v in zip(s_parts, v_parts))
        if has_sink:
            one_lane = lax.broadcasted_iota(jnp.int32, (1, LANE), 1) == V_ONE
            acc = acc + jnp.where(one_lane, jnp.exp2(sink - m), 0.0)
        return m, acc

    def finish(r0, nq, acc):
        o = acc * (1.0 / acc[:, V_ONE:V_ONE + 1])
        for g in range(grp):
            o_ref[r0:r0 + nq, g * LANE:(g + 1) * LANE] = o[g * nq:(g + 1) * nq].astype(o_ref.dtype)

    if has_seq and window is not None:
        for r0 in range(0, bq, sub):
            _, s_parts, v_parts = logits(r0, sub)
            finish(r0, sub, softmax_pv(sub, s_parts, v_parts)[1])
        return
    q, s_parts, v_parts = logits(0, bq)
    m, acc = softmax_pv(bq, s_parts, v_parts)
    if has_seq:
        def body(j, carry):
            m, acc = carry
            st = pl.multiple_of(j * bk, bk)
            s = scores(q, ks_ref[pl.ds(st, bk), :])
            m_new = jnp.maximum(m, jnp.max(s, axis=-1, keepdims=True))
            p = jnp.exp2(s - m_new).astype(BF16)
            acc = jnp.exp2(m - m_new) * acc + jnp.dot(p, vs_ref[pl.ds(st, bk), :], preferred_element_type=F32)
            return m_new, acc
        m, acc = lax.fori_loop(1, t_seq // bk, body, (m, acc), unroll=True)
    finish(0, bq, acc)


def _attention(q, kc, vc, ks=None, vs=None, *, grp, sink=None, window=None, bq=TILES['attn_q'], bk=TILES['mla_k']):
    b, tq, wq = q.shape
    hkv = wq // (grp * LANE)
    lc = kc.shape[1]
    has_seq = ks is not None
    t_seq = ks.shape[1] if has_seq else 0
    bq = _tile(tq, bq)
    sub = min(bq, TILES['swa_sub'])
    if has_seq and window is None:
        bk = _tile(t_seq, bk)
    kern = functools.partial(_attn_kernel, grp=grp, bq=bq, sub=sub, has_sink=sink is not None, has_seq=has_seq,
                             window=window, t_seq=t_seq, bk=bk)
    in_specs, args = [], []
    if sink is not None:
        in_specs.append(pl.BlockSpec(memory_space=pltpu.SMEM))
        args.append(sink)
    in_specs.append(pl.BlockSpec((None, bq, grp * LANE), lambda bb, h, i: (bb, i, h)))
    args.append(q)
    ctx_spec = pl.BlockSpec((None, lc, LANE), lambda bb, h, i: (bb, 0, h))
    in_specs += [ctx_spec, ctx_spec]
    args += [kc, vc]
    if has_seq:
        seq_spec = pl.BlockSpec((None, t_seq, LANE), lambda bb, h, i: (bb, 0, h))
        in_specs += [seq_spec, seq_spec]
        args += [ks, vs]
    return pl.pallas_call(
        kern,
        out_shape=jax.ShapeDtypeStruct((b, tq, wq), BF16),
        grid=(b, hkv, tq // bq),
        in_specs=in_specs,
        out_specs=pl.BlockSpec((None, bq, grp * LANE), lambda bb, h, i: (bb, i, h)),
        compiler_params=_params(("parallel", "parallel", "parallel")),
        name="attention",
    )(*args)


def _split3(x):
    hi = x.astype(BF16)
    r = x - hi.astype(F32)
    mid = r.astype(BF16)
    lo = (r - mid.astype(F32)).astype(BF16)
    return hi, mid, lo


def _hg_kernel(gp_ref, qf_ref, zf_ref, vf_ref, qb_ref, zb_ref, vb_ref, s0f_ref, s0b_ref,
               of_ref, ob_ref, sf_ref, sb_ref, st_ref, *, nchunk):
    i = pl.program_id(1)
    c = HG_CHUNK
    w = HG_HEADS * HG_DK

    @pl.when(i == 0)
    def _():
        st_ref[0] = s0f_ref[...]
        st_ref[1] = s0b_ref[...]

    ri = lax.broadcasted_iota(jnp.int32, (c, c), 0)
    ci = lax.broadcasted_iota(jnp.int32, (c, c), 1)
    ti = lax.broadcasted_iota(jnp.int32, (c, HG_HEADS * c), 0)
    si = lax.broadcasted_iota(jnp.int32, (c, HG_HEADS * c), 1) % c
    lane = lax.broadcasted_iota(jnp.int32, (1, w), 1)
    hmask = [(lane // HG_DK) == h for h in range(HG_HEADS)]
    r2 = lax.broadcasted_iota(jnp.int32, (w, w), 0) // HG_DV
    c2 = lax.broadcasted_iota(jnp.int32, (w, w), 1) // HG_DK
    diag = r2 == c2
    dirs = {
        True: dict(q=qf_ref, z=zf_ref, v=vf_ref, o=of_ref, gp=tuple(gp_ref[r:r + 1, :] for r in range(3)),
                   tri=jnp.where(ci <= ri, 1.0, 0.0).astype(BF16), causal=si <= ti),
        False: dict(q=qb_ref, z=zb_ref, v=vb_ref, o=ob_ref, gp=tuple(gp_ref[r:r + 1, :] for r in range(3, 6)),
                    tri=jnp.where(ci >= ri, 1.0, 0.0).astype(BF16), causal=si >= ti),
    }
    units = []
    for n in range(nchunk):
        units += [(True, n), (False, nchunk - 1 - n)]

    ph1 = []
    for fwd, n in units:
        dd = dirs[fwd]
        sl = slice(n * c, (n + 1) * c)
        z = dd['z'][sl, :]
        log_lb, l1p, om = dd['gp']
        ls = jnp.minimum(z, 0.0) - jnp.log(1.0 + jnp.exp(-jnp.abs(z)))
        bterm = l1p + ls
        lf = jnp.maximum(log_lb, bterm) + jnp.log(1.0 + jnp.exp(-jnp.abs(log_lb - bterm)))
        k = om / (1.0 + jnp.exp(z))
        gsum = sum(jnp.dot(dd['tri'], part, preferred_element_type=F32) for part in _split3(lf))
        ph1.append((k, gsum))

    ph2 = []
    for (fwd, n), (k, gsum) in zip(units, ph1):
        dd = dirs[fwd]
        sl = slice(n * c, (n + 1) * c)
        q, v = dd['q'][sl, :], dd['v'][sl, :]
        tot = gsum[c - 1:c, :] if fwd else gsum[0:1, :]
        gm = gsum[c // 2:c // 2 + 1, :]
        qd = (q * jnp.exp(gsum - gm)).astype(BF16)
        kinv = k * jnp.exp(gm - gsum)
        q_in = (q * jnp.exp(gsum)).astype(BF16)
        k_end = (k * jnp.exp(tot - gsum)).astype(BF16)
        kstack = jnp.concatenate([jnp.where(hm, kinv, 0.0) for hm in hmask], axis=0).astype(BF16)
        vstack = jnp.concatenate([jnp.where(hm, v, 0.0) for hm in hmask], axis=0).astype(BF16)
        ph2.append((qd, kstack, vstack, q_in, k_end, v.astype(BF16), jnp.exp(tot)))

    ph3 = []
    for (fwd, n), (qd, kstack, vstack, q_in, k_end, vb, dec) in zip(units, ph2):
        att = lax.dot_general(qd, kstack, (((1,), (1,)), ((), ())), preferred_element_type=F32)
        att = jnp.where(dirs[fwd]['causal'], att, 0.0).astype(BF16)
        o_intra = jnp.dot(att, vstack, preferred_element_type=F32)
        kv_t = lax.dot_general(vb, k_end, (((0,), (0,)), ((), ())), preferred_element_type=F32)
        ph3.append((o_intra, jnp.where(diag, kv_t, 0.0)))

    st = {True: st_ref[0], False: st_ref[1]}
    for (fwd, n), (_, _, _, q_in, _, _, dec), (o_intra, kv_t) in zip(units, ph2, ph3):
        sl = slice(n * c, (n + 1) * c)
        o_inter = lax.dot_general(q_in, st[fwd].astype(BF16), (((1,), (1,)), ((), ())), preferred_element_type=F32)
        dirs[fwd]['o'][sl, :] = (o_intra + o_inter).astype(dirs[fwd]['o'].dtype)
        st[fwd] = st[fwd] * dec + kv_t
    st_ref[0] = st[True]
    st_ref[1] = st[False]

    @pl.when(i == pl.num_programs(1) - 1)
    def _():
        sf_ref[...] = st[True]
        sb_ref[...] = st[False]


def _hgrn(q, zf, zb, v, gp, s0f, s0b):
    b, t, w = q.shape
    tb = _tile(t, TILES['hgrn'])
    nblk = t // tb
    fw = pl.BlockSpec((None, tb, w), lambda bb, i: (bb, i, 0))
    bw = pl.BlockSpec((None, tb, w), lambda bb, i: (bb, nblk - 1 - i, 0))
    st = pl.BlockSpec((None, w, w), lambda bb, i: (bb, 0, 0))
    return pl.pallas_call(
        functools.partial(_hg_kernel, nchunk=tb // HG_CHUNK),
        out_shape=(jax.ShapeDtypeStruct((b, t, w), BF16), jax.ShapeDtypeStruct((b, t, w), BF16),
                   jax.ShapeDtypeStruct((b, w, w), F32), jax.ShapeDtypeStruct((b, w, w), F32)),
        grid=(b, nblk),
        in_specs=[_full((8, w)), fw, fw, fw, bw, bw, bw, st, st],
        out_specs=(fw, bw, st, st),
        scratch_shapes=[pltpu.VMEM((2, w, w), F32)],
        compiler_params=_params(("parallel", "arbitrary")),
        name="hgrn2",
    )(gp, q, zf, v, q, zb, v, s0f, s0b)


def _gelu_tanh(x):
    return 0.5 * x * (1.0 + jnp.tanh(math.sqrt(2.0 / math.pi) * (x + 0.044715 * (x * x * x))))


def _outproj_kernel(x_ref, g1_ref, ya_ref, yb_ref, of_ref, ob_ref, hg_ref, yd_ref,
                    wglu_ref, bglu_ref, hn_ref, pavg_ref, wo_ref, o_ref, ys_ref):
    for j in range(S5_CHUNK):
        for v in range(2):
            ys_ref[v, pl.ds(j, ys_ref.shape[1] // S5_CHUNK, stride=S5_CHUNK), :] = ya_ref[j, :, v * LANE:(v + 1) * LANE].astype(F32)
    ya = _gelu_tanh(jnp.concatenate([ys_ref[0], ys_ref[1]], axis=1))
    gl = jnp.dot(ya.astype(BF16), wglu_ref[...], preferred_element_type=F32) + bglu_ref[...]
    o = of_ref[...].astype(F32) + ob_ref[...].astype(F32)
    o2 = o * o
    hi = o2.astype(BF16)
    lo = (o2 - hi.astype(F32)).astype(BF16)
    ms = (jnp.dot(hi, pavg_ref[...], preferred_element_type=F32)
          + jnp.dot(lo, pavg_ref[...], preferred_element_type=F32))
    y = (jnp.dot(yb_ref[...], wo_ref[256:768, :], preferred_element_type=F32)
         + jnp.dot(yd_ref[...], wo_ref[1024:1536, :], preferred_element_type=F32))
    ya = ya * (1.0 / (1.0 + jnp.exp(-gl)))
    gate = hg_ref[...].astype(F32)
    yc = o * lax.rsqrt(ms + EPS) * hn_ref[...] * (gate * (1.0 / (1.0 + jnp.exp(-gate))))
    y = (y + jnp.dot(ya.astype(BF16), wo_ref[0:256, :], preferred_element_type=F32)
         + jnp.dot(yc.astype(BF16), wo_ref[768:1024, :], preferred_element_type=F32))
    o_ref[...] = x_ref[...] + g1_ref[...] * y


def _outproj(x, g1, ya, yb, of, ob, hg, yd, wglu, bglu, hn, pavg, wo_p, li):
    b, t, d = x.shape
    tt = _tile(t, TILES['outproj'])
    xs = lambda n: pl.BlockSpec((None, tt, n), lambda bb, i: (bb, i, 0))
    vec = pl.BlockSpec((None, 1, d), lambda bb, i: (bb, 0, 0))
    return pl.pallas_call(
        _outproj_kernel,
        out_shape=jax.ShapeDtypeStruct((b, t, d), F32),
        grid=(b, t // tt),
        in_specs=[xs(d), vec, pl.BlockSpec((None, S5_CHUNK, tt // S5_CHUNK, 256), lambda bb, i: (bb, 0, i, 0)),
                  xs(512), xs(256), xs(256), xs(256), xs(512),
                  _layer(wglu, li), _full((1, 256)), _full((1, 256)), _full(pavg.shape), _layer(wo_p, li)],
        out_specs=xs(d),
        scratch_shapes=[pltpu.VMEM((2, tt, LANE), F32)],
        compiler_params=_params(("parallel", "parallel")),
        name="outproj",
    )(x, g1, ya, yb, of, ob, hg, yd, wglu, bglu, hn, pavg, wo_p)


def _ffn_kernel(x_ref, sc_ref, sh_ref, g2_ref, ng_ref, wup_ref, wdn_ref, fg_ref, o_ref, acc_ref, *, hidden, ck, final):
    x = x_ref[...]
    hb = (_rms(x, ng_ref[...]) * (1.0 + sc_ref[...]) + sh_ref[...]).astype(BF16)
    for j in range(hidden // ck):
        gate = jnp.dot(hb, wup_ref[:, j * ck:(j + 1) * ck], preferred_element_type=F32)
        up = jnp.dot(hb, wup_ref[:, hidden + j * ck:hidden + (j + 1) * ck], preferred_element_type=F32)
        a = (gate * (1.0 / (1.0 + jnp.exp(-gate))) * up).astype(BF16)
        part = jnp.dot(a, wdn_ref[j * ck:(j + 1) * ck, :], preferred_element_type=F32)
        if j == 0:
            acc_ref[...] = part
        else:
            acc_ref[...] += part
    y = x + g2_ref[...] * acc_ref[...]
    if final:
        y = _rms(y, fg_ref[...])
    o_ref[...] = y


def _ffn(x, sc, sh, g2, ng, wup, wdn, fg, final, li):
    b, t, d = x.shape
    hidden = wdn.shape[1]
    tt = _tile(t, TILES['ffn'])
    xs = pl.BlockSpec((None, tt, d), lambda bb, i: (bb, i, 0))
    vec = pl.BlockSpec((None, 1, d), lambda bb, i: (bb, 0, 0))
    return pl.pallas_call(
        functools.partial(_ffn_kernel, hidden=hidden, ck=_tile(hidden, TILES['ffn_hidden']), final=final),
        out_shape=jax.ShapeDtypeStruct((b, t, d), F32),
        grid=(b, t // tt),
        in_specs=[xs, vec, vec, vec, _full((1, d)), _layer(wup, li), _layer(wdn, li), _full((1, d))],
        out_specs=xs,
        scratch_shapes=[pltpu.VMEM((tt, d), F32)],
        compiler_params=_params(("parallel", "parallel")),
        name="ffn",
    )(x, sc, sh, g2, ng, wup, wdn, fg)


def _pad_heads(w, heads, dim):
    w = w.reshape(w.shape[:-1] + (heads, dim))
    w = jnp.pad(w, [(0, 0)] * (w.ndim - 1) + [(0, LANE - dim)])
    return w.reshape(w.shape[:-2] + (heads * LANE,))


def _layer_weights(w_in, w_out, mla_w_qb, mla_w_kvb):
    depth, d, n_in = w_in.shape
    qscale = jnp.ones((n_in,), F32).at[_C_SQ:_C_SQ + SWA_HEADS * SWA_HEAD_DIM].set(SWA_HEAD_DIM ** -0.5 * LOG2E)
    w_in_p = jnp.pad(w_in * qscale, ((0, 0), (0, 0), (0, _N_INP - n_in))).astype(BF16)
    wqb_p = _pad_heads(mla_w_qb * (MLA_SCALE * LOG2E), MLA_HEADS, MLA_NOPE + MLA_ROPE).astype(BF16)
    kvb = mla_w_kvb.reshape(depth, MLA_KV_RANK, MLA_HEADS, MLA_NOPE + MLA_V)
    wk = _pad_heads(kvb[..., :MLA_NOPE].reshape(depth, MLA_KV_RANK, -1), MLA_HEADS, MLA_NOPE)
    wv = _pad_heads(kvb[..., MLA_NOPE:].reshape(depth, MLA_KV_RANK, -1), MLA_HEADS, MLA_V)
    wkv_p = jnp.concatenate([wk, wv], axis=-1).astype(BF16)
    assert w_out.shape[1] == 4 * 256
    pad_rows = lambda w, heads, dim: jnp.swapaxes(_pad_heads(jnp.swapaxes(w, -1, -2), heads, dim), -1, -2)
    wo_p = jnp.concatenate([w_out[:, 0:256], pad_rows(w_out[:, 256:512], SWA_HEADS, SWA_HEAD_DIM),
                            w_out[:, 512:768], pad_rows(w_out[:, 768:1024], MLA_HEADS, MLA_V)], axis=1).astype(BF16)
    return w_in_p, wqb_p, wkv_p, wo_p


def _rope_tables(length, dim, lo, ident, repeat=1):
    n_freq = dim // 4
    rows = length // GRID_W
    row = jnp.repeat(jnp.arange(rows, dtype=F32), GRID_W)
    col = jnp.tile(jnp.arange(GRID_W, dtype=F32), rows)
    inv = ROPE_BASE ** (-jnp.arange(n_freq, dtype=F32) / n_freq)
    ang = jnp.stack([row[:, None] * inv, col[:, None] * inv], axis=1)
    cos, sin = jnp.cos(ang), jnp.sin(ang)
    z = jnp.zeros_like(sin)
    cos_l = jnp.stack([cos, cos], axis=2).reshape(length, dim)
    sina = jnp.stack([-sin, z], axis=2).reshape(length, dim)
    sinb = jnp.stack([z, sin], axis=2).reshape(length, dim)
    if ident:
        cos_l, sina, sinb = jnp.ones_like(cos_l), jnp.zeros_like(sina), jnp.zeros_like(sinb)
    cos_l, sina, sinb = (jnp.tile(a, (1, repeat)) for a in (cos_l, sina, sinb))
    pad = lambda a, fill: jnp.pad(a, ((0, 0), (lo, LANE - lo - dim * repeat)), constant_values=fill)
    return jnp.stack([pad(cos_l, 1.0), pad(sina, 0.0), pad(sinb, 0.0)], axis=0)


def kernel(x, c, ctx, c_ctx, w_mod, b_mod, norm1_g, norm2_g, w_in, w_out, s5_lam_re, s5_lam_im, s5_log_dt,
           s5_b_re, s5_b_im, s5_c_re, s5_c_im, s5_d, s5_w_glu, s5_b_glu, swa_sink, hg_lb, hg_norm_g,
           mla_q_norm_g, mla_w_qb, mla_kv_norm_g, mla_w_kvb, ffn_w_up, ffn_w_down, final_norm_g):
    b, t, d = x.shape
    lc = ctx.shape[1]
    depth = w_mod.shape[0]

    cc = jnp.zeros((8, d), F32).at[:b].set(c).at[b].set(c_ctx)
    mods = _modulation(cc, w_mod, b_mod)

    lb_cum = jnp.cumsum(jax.nn.softmax(hg_lb.astype(F32), axis=1), axis=1)
    lb = lb_cum - lb_cum[:, :1]

    tab_swa = _rope_tables(t, SWA_HEAD_DIM, 0, False, repeat=2)
    tab_mla = _rope_tables(t, MLA_ROPE, MLA_NOPE, False)
    tab_swa_c = _rope_tables(lc, SWA_HEAD_DIM, 0, True, repeat=2)
    tab_mla_c = _rope_tables(lc, MLA_ROPE, MLA_NOPE, True)
    pavg = jnp.kron(jnp.eye(HG_HEADS, dtype=F32), jnp.full((HG_DV, HG_DV), 1.0 / HG_DV, F32)).astype(BF16)
    zero_st = jnp.zeros((b, HG_HEADS * HG_DV, HG_HEADS * HG_DK), F32)
    zero_s5 = jnp.zeros((b, S5_GROUPS, 4 * S5_STATE), F32)

    w_in_p, wqb_p, wkv_p, wo_p = _layer_weights(w_in, w_out, mla_w_qb, mla_w_kvb)
    mats = _s5_matrices(s5_lam_re, s5_lam_im, s5_log_dt, s5_b_re, s5_b_im, s5_c_re, s5_c_im, s5_d)
    wglu = s5_w_glu.astype(BF16)
    wup = ffn_w_up.astype(BF16)
    wdn = ffn_w_down.astype(BF16)
    fg = final_norm_g.reshape(1, d)

    for i in range(depth):
        need_ctx = i < depth - 1
        mod = mods[i, :b].reshape(b, 6, 1, d)
        mod_c = jnp.broadcast_to(mods[i, b].reshape(1, 6, 1, d), (b, 6, 1, d))
        sh1, sc1, g1, sh2, sc2, g2 = (mod[:, j] for j in range(6))
        csh1, csc1, cg1, csh2, csc2, cg2 = (mod_c[:, j] for j in range(6))
        n1 = norm1_g[i].reshape(1, d)
        qg = mla_q_norm_g[i].reshape(1, -1)
        kvg = mla_kv_norm_g[i].reshape(1, -1)
        px = _inproj(x, sc1, sh1, n1, w_in_p, tab_swa, tab_mla, qg, wqb_p, kvg, wkv_p, i)
        pc = _inproj(ctx, csc1, csh1, n1, w_in_p, tab_swa_c, tab_mla_c, qg, wqb_p, kvg, wkv_p, i)
        (xu, xsq, xsk, xsv, xhq, xhzf, xhzb, xhi, xhg, xmq, xmk, xmv) = px
        (cu, csq, csk, csv, chq, chzf, chzb, chi, chg, cmq, cmk, cmv) = pc

        ya_c, s5_fin = _s5_scan(cu, mats, zero_s5, i)
        ya, _ = _s5_scan(xu, mats, s5_fin, i)

        grp = SWA_HEADS // SWA_KV_HEADS
        sink = swa_sink[i].astype(F32) * LOG2E
        yb = _attention(xsq, csk, csv, xsk, xsv, grp=grp, sink=sink, window=SWA_WINDOW, bq=TILES['swa_q'])
        lbf, lbb = lb[0, i], lb[1, i]
        gp = jnp.stack([jnp.log(lbf), jnp.log1p(-lbf), 1.0 - lbf,
                        jnp.log(lbb), jnp.log1p(-lbb), 1.0 - lbb, lbf, lbb], axis=0)
        of_c, ob_c, stf, stb = _hgrn(chq, chzf, chzb, chi, gp, zero_st, zero_st)
        of, ob, _, _ = _hgrn(xhq, xhzf, xhzb, xhi, gp, stf, stb)
        yd = _attention(xmq, cmk, cmv, xmk, xmv, grp=1, bq=TILES['mla_q'])

        bglu = s5_b_glu[i].reshape(1, -1)
        hn = jnp.tile(hg_norm_g[i], HG_HEADS).reshape(1, -1)
        x = _outproj(x, g1, ya, yb, of, ob, xhg, yd, wglu, bglu, hn, pavg, wo_p, i)
        n2 = norm2_g[i].reshape(1, d)
        x = _ffn(x, sc2, sh2, g2, n2, wup, wdn, fg, not need_ctx, i)
        if need_ctx:
            yb_c = _attention(csq, csk, csv, grp=grp, sink=sink)
            yd_c = _attention(cmq, cmk, cmv, grp=1)
            ctx = _outproj(ctx, cg1, ya_c, yb_c, of_c, ob_c, chg, yd_c, wglu, bglu, hn, pavg, wo_p, i)
            ctx = _ffn(ctx, csc2, csh2, cg2, n2, wup, wdn, fg, False, i)
    return x
```
